```python
import math, functools
import jax, jax.numpy as jnp
from jax import lax
import numpy as np

D_MODEL = 1024
BATCH = 16
SEQ = 2048
DEPTH = 2
DEC_BATCH = 32
DEC_SEQ = 8
PAST_LEN = 16384
PAGE_SIZE = 128

GLA_HEADS = 4
GLA_DK = 128
GLA_DV = 256
GLA_RANK = 16
GLA_TAU = 16.0
GLA_CHUNK = 64
CONV_DIM = D_MODEL
CONV_W = 3
NSA_HEADS = 16
NSA_KV_HEADS = 4
NSA_GROUP = NSA_HEADS // NSA_KV_HEADS
HEAD_DIM = 64
CMP_BLOCK = 32
CMP_STRIDE = 16
CMP_HIDDEN = 128
SLC_BLOCK = 64
SLC_TOPK = 16
WINDOW = 512
NSA_QCHUNK = 16
REL_BUCKETS = 32
REL_MAX_DIST = 128
N_EXPERTS = 64
TOP_K = 8
N_GROUPS = 8
TOPK_GROUPS = 4
EXPERT_HIDDEN = 256
SHARED_HIDDEN = 256
ROUTED_SCALE = 2.5
MOE_BLOCK = 128
DN_ALPHA = (2 * DEPTH) ** 0.25
DN_BETA = (8 * DEPTH) ** -0.25
LN_EPS = 1e-5
NEG_INF = -1e30

GLA_QK = GLA_HEADS * GLA_DK
GLA_V = GLA_HEADS * GLA_DV
NSA_Q = NSA_HEADS * HEAD_DIM
NSA_KV = NSA_KV_HEADS * HEAD_DIM
IN_SIZES = (GLA_QK, GLA_QK, GLA_V, GLA_V, GLA_RANK, CONV_DIM, CONV_DIM, CONV_DIM, NSA_Q, NSA_KV, NSA_KV, NSA_KV, NSA_KV, NSA_KV, NSA_KV, 3 * NSA_HEADS, 3 * D_MODEL)
IN_TOTAL = sum(IN_SIZES)

kernel_name = 'hybrid_gla_conv_nsa_moe_step'


def layer_norm(x, g, b):
    xf = x.astype(jnp.float32)
    mu = xf.mean(-1, keepdims=True)
    var = jnp.square(xf - mu).mean(-1, keepdims=True)
    return ((xf - mu) * lax.rsqrt(var + LN_EPS) * g + b).astype(x.dtype)


def masked_softmax(logits, mask):
    p = jax.nn.softmax(jnp.where(mask, logits.astype(jnp.float32), NEG_INF), axis=-1)
    return jnp.where(mask, p, 0.0)


def rel_bucket(dist):
    n = jnp.maximum(dist, 0)
    exact = REL_BUCKETS // 2
    big = exact + (jnp.log(jnp.maximum(n, 1).astype(jnp.float32) / exact)
                   / math.log(REL_MAX_DIST / exact) * (REL_BUCKETS - exact)).astype(jnp.int32)
    return jnp.where(n < exact, n, jnp.minimum(big, REL_BUCKETS - 1))


def gla_scan(q, k, v, log_a, s0):
    b, t = q.shape[:2]
    c = math.gcd(t, GLA_CHUNK)
    nc = t // c
    causal = jnp.tril(jnp.ones((c, c), dtype=bool))

    def to_chunks(a):
        return a.reshape(b, nc, c, *a.shape[2:]).swapaxes(0, 1)

    def step(s, inp):
        qc, kc, vc, ac = inp
        cum = jnp.cumsum(ac, axis=1)
        inter = jnp.einsum('bthk,bhkv->bthv', qc * jnp.exp(cum), s)
        diff = cum[:, :, None] - cum[:, None, :]
        decay = jnp.exp(jnp.where(causal[None, :, :, None, None], diff, -jnp.inf))
        att = jnp.einsum('bthk,bshk,btshk->bhts', qc, kc, decay)
        intra = jnp.einsum('bhts,bshv->bthv', att, vc)
        last = cum[:, -1]
        s = jnp.exp(last)[..., None] * s + jnp.einsum('bshk,bshv->bhkv', kc * jnp.exp(last[:, None] - cum), vc)
        return s, inter + intra

    s_fin, o = lax.scan(step, s0, (to_chunks(q), to_chunks(k), to_chunks(v), to_chunks(log_a)))
    return o.swapaxes(0, 1).reshape(b, t, *v.shape[2:]), s_fin


def gla_output(o, gate, norm_g):
    mu = o.mean(-1, keepdims=True)
    var = jnp.square(o - mu).mean(-1, keepdims=True)
    on = ((o - mu) * lax.rsqrt(var + LN_EPS)).reshape(*o.shape[:2], -1)
    return on * norm_g * jax.nn.silu(gate.astype(jnp.float32))


def short_conv(b_gate, c_gate, x_in, prev, w):
    u = c_gate * x_in
    up = jnp.concatenate([prev, u], axis=1)
    t = u.shape[1]
    y = up[:, 0:t] * w[0]
    for j in range(1, CONV_W):
        y = y + up[:, j:j + t] * w[j]
    return b_gate * y, up[:, t:]


def compress(k, pe, w1, w2):
    b, kvh, length, hd = k.shape
    n_chunks = length // CMP_STRIDE
    pieces = CMP_BLOCK // CMP_STRIDE
    n_blocks = n_chunks - pieces + 1
    ch = k.reshape(b, kvh, n_chunks, CMP_STRIDE * hd)
    rows = CMP_STRIDE * hd
    hid = pe.reshape(-1) @ w1
    for r in range(pieces):
        hid = hid + (ch @ w1[r * rows:(r + 1) * rows])[:, :, r:r + n_blocks]
    return jax.nn.gelu(hid) @ w2


def slc_overlap(n_cmp, n_slc):
    start = jnp.arange(n_cmp) * CMP_STRIDE
    blk = jnp.arange(n_slc) * SLC_BLOCK
    return ((start[:, None] < blk[None, :] + SLC_BLOCK) & (start[:, None] + CMP_BLOCK > blk[None, :])).astype(jnp.float32)


def nsa_attend(q, qpos, gates, kc, vc, overlap, gather_slc, kw, vw, wpos, tbl):
    b, nq, kvh, grp, hd = q.shape
    scale = hd ** -0.5
    f32 = jnp.float32
    hi = jnp.arange(kvh)[:, None, None]
    cmp_end = CMP_STRIDE * jnp.arange(kc.shape[2]) + CMP_BLOCK - 1
    dist_c = qpos[:, None] - cmp_end[None, :]
    lc = jnp.einsum('bqhgd,bhnd->bqhgn', q, kc).astype(f32) * scale + jnp.moveaxis(tbl[rel_bucket(dist_c)], 1, -1)
    pc = masked_softmax(lc, (dist_c >= 0)[:, None, None, :])
    o_c = jnp.einsum('bqhgn,bhnd->bqhgd', pc, vc)
    imp = jnp.einsum('bqhn,ns->bqhs', pc.sum(3), overlap)
    blk = jnp.arange(overlap.shape[1])
    cur = (qpos // SLC_BLOCK)[:, None]
    forced = (blk[None] == 0) | (blk[None] == cur) | (blk[None] == cur - 1)
    visible = blk[None] * SLC_BLOCK <= qpos[:, None]
    score = jnp.where(forced[:, None], -NEG_INF, imp)
    score = jnp.where(visible[:, None], score, NEG_INF)
    top_s, idx = lax.top_k(score, min(SLC_TOPK, overlap.shape[1]))
    ks, vs = gather_slc(idx)
    kpos = idx[..., None] * SLC_BLOCK + jnp.arange(SLC_BLOCK)
    dist_s = qpos[None, :, None, None, None] - kpos
    mask_s = (dist_s >= 0) & (top_s > 0.5 * NEG_INF)[..., None]
    ls = jnp.einsum('bqhgd,bqhnsd->bqhgns', q, ks).astype(f32) * scale + jnp.moveaxis(tbl[rel_bucket(dist_s), hi], -1, 3)
    ps = masked_softmax(ls.reshape(b, nq, kvh, grp, -1), mask_s.reshape(b, nq, kvh, 1, -1))
    o_s = jnp.einsum('bqhgm,bqhmd->bqhgd', ps, vs.reshape(b, nq, kvh, -1, hd))
    dist_w = qpos[:, None] - wpos[None, :]
    mask_w = (dist_w >= 0) & (dist_w < WINDOW) & (wpos >= 0)[None, :]
    lwin = jnp.einsum('bqhgd,bhwd->bqhgw', q, kw).astype(f32) * scale + jnp.moveaxis(tbl[rel_bucket(dist_w)], 1, -1)
    pw = masked_softmax(lwin, mask_w[:, None, None, :])
    o_w = jnp.einsum('bqhgw,bhwd->bqhgd', pw, vw)
    o = gates[..., 0:1] * o_c + gates[..., 1:2] * o_s + gates[..., 2:3] * o_w
    return o.astype(q.dtype)


def nsa_prompt(q, gates, kc_r, vc_r, ks_r, vs_r, kw_r, vw_r, lw, tbl):
    b, s = q.shape[:2]
    kc = compress(kc_r, lw['cmp_pe_k'], lw['cmp_w1_k'], lw['cmp_w2_k'])
    vc = compress(vc_r, lw['cmp_pe_v'], lw['cmp_w1_v'], lw['cmp_w2_v'])
    nsb = s // SLC_BLOCK
    ks_blk = ks_r.reshape(b, NSA_KV_HEADS, nsb, SLC_BLOCK, HEAD_DIM)
    vs_blk = vs_r.reshape(b, NSA_KV_HEADS, nsb, SLC_BLOCK, HEAD_DIM)
    bi = jnp.arange(b)[:, None, None, None]
    hi = jnp.arange(NSA_KV_HEADS)[None, None, :, None]

    def gather_slc(idx):
        return ks_blk[bi, hi, idx], vs_blk[bi, hi, idx]

    overlap = slc_overlap(kc.shape[2], nsb)
    kw_pad = jnp.pad(kw_r, ((0, 0), (0, 0), (WINDOW, 0), (0, 0)))
    vw_pad = jnp.pad(vw_r, ((0, 0), (0, 0), (WINDOW, 0), (0, 0)))
    n_chunks = s // NSA_QCHUNK
    span = WINDOW + NSA_QCHUNK

    def chunks(a):
        return a.reshape(b, n_chunks, NSA_QCHUNK, *a.shape[2:]).swapaxes(0, 1)

    def body(inp):
        qc, gc, c0 = inp
        return nsa_attend(qc, c0 + jnp.arange(NSA_QCHUNK), gc, kc, vc, overlap, gather_slc,
                          lax.dynamic_slice_in_dim(kw_pad, c0, span, axis=2),
                          lax.dynamic_slice_in_dim(vw_pad, c0, span, axis=2),
                          c0 - WINDOW + jnp.arange(span), tbl)

    o = lax.map(body, (chunks(q), chunks(gates), jnp.arange(n_chunks) * NSA_QCHUNK))
    o = o.swapaxes(0, 1).reshape(b, s, *q.shape[2:])
    wb = min(WINDOW, s)
    return o, kw_r[:, :, s - wb:], vw_r[:, :, s - wb:]


def nsa_sample(q, gates, kc_r, vc_r, ks_r, vs_r, kw_r, vw_r, lw, tbl, layer, pool_ck, pool_cv, pool_sk, pool_sv, win_k, win_v, page_table):
    b, t = q.shape[:2]
    past = page_table.shape[1] * PAGE_SIZE
    t_cmp = -(-t // CMP_STRIDE) * CMP_STRIDE
    nnb = -(-t // SLC_BLOCK)
    npb = past // SLC_BLOCK
    bpp = PAGE_SIZE // SLC_BLOCK

    def past_rows(pool):
        g = pool[layer, page_table]
        return g.transpose(0, 2, 1, 3, 4).reshape(b, NSA_KV_HEADS, past, HEAD_DIM)

    def pad_t(a, n):
        return jnp.pad(a, ((0, 0), (0, 0), (0, n - a.shape[2]), (0, 0)))

    kc = compress(jnp.concatenate([past_rows(pool_ck), pad_t(kc_r, t_cmp)], axis=2), lw['cmp_pe_k'], lw['cmp_w1_k'], lw['cmp_w2_k'])
    vc = compress(jnp.concatenate([past_rows(pool_cv), pad_t(vc_r, t_cmp)], axis=2), lw['cmp_pe_v'], lw['cmp_w1_v'], lw['cmp_w2_v'])
    new_ks = pad_t(ks_r, nnb * SLC_BLOCK).reshape(b, NSA_KV_HEADS, nnb, SLC_BLOCK, HEAD_DIM)
    new_vs = pad_t(vs_r, nnb * SLC_BLOCK).reshape(b, NSA_KV_HEADS, nnb, SLC_BLOCK, HEAD_DIM)
    bi = jnp.arange(b)[:, None, None, None]
    hi = jnp.arange(NSA_KV_HEADS)[None, None, :, None]
    offs = jnp.arange(SLC_BLOCK)

    def gather_slc(idx):
        in_past = (idx < npb)[..., None, None]
        pidx = jnp.minimum(idx, npb - 1)
        page = page_table[bi, pidx // bpp][..., None]
        rows = ((pidx % bpp) * SLC_BLOCK)[..., None] + offs
        nidx = jnp.clip(idx - npb, 0, nnb - 1)
        ks = jnp.where(in_past, pool_sk[layer, page, hi[..., None], rows], new_ks[bi, hi, nidx])
        vs = jnp.where(in_past, pool_sv[layer, page, hi[..., None], rows], new_vs[bi, hi, nidx])
        return ks, vs

    kw = jnp.concatenate([win_k.astype(kw_r.dtype), kw_r], axis=2)
    vw = jnp.concatenate([win_v.astype(vw_r.dtype), vw_r], axis=2)
    wb = win_k.shape[2]
    o = nsa_attend(q, past + jnp.arange(t), gates, kc, vc, slc_overlap(kc.shape[2], npb + nnb), gather_slc,
                   kw, vw, past - wb + jnp.arange(wb + t), tbl)
    return o, kw[:, :, -wb:], vw[:, :, -wb:]


def token_mixer(h, lw, gla_s0, conv_prev, nsa_fn):
    b, t, _ = h.shape
    f32 = jnp.float32
    (gq, gk, gv, gg, ga, cb, cc, cx, nq, nkc, nvc, nks, nvs, nkw, nvw, ngt, mgt) = jnp.split(
        h @ lw['w_in'], np.cumsum(IN_SIZES)[:-1].tolist(), axis=-1)
    q = gq.reshape(b, t, GLA_HEADS, GLA_DK).astype(f32) * GLA_DK ** -0.5
    k = gk.reshape(b, t, GLA_HEADS, GLA_DK).astype(f32)
    v = gv.reshape(b, t, GLA_HEADS, GLA_DV).astype(f32)
    log_a = (jax.nn.log_sigmoid((ga @ lw['gla_wa2'] + lw['gla_ba']).astype(f32)) / GLA_TAU).reshape(b, t, GLA_HEADS, GLA_DK)
    o, s_fin = gla_scan(q, k, v, log_a, gla_s0.astype(f32))
    o_gla = gla_output(o, gg, lw['gla_norm_g']).astype(h.dtype)
    o_conv, conv_state = short_conv(cb, cc, cx, conv_prev.astype(h.dtype), lw['conv_w'])
    def rows(a):
        return a.reshape(b, t, NSA_KV_HEADS, HEAD_DIM).transpose(0, 2, 1, 3)
    kc_r, vc_r, ks_r, vs_r, kw_r, vw_r = [rows(a) for a in (nkc, nvc, nks, nvs, nkw, nvw)]
    q_n = nq.reshape(b, t, NSA_KV_HEADS, NSA_GROUP, HEAD_DIM)
    g_n = jax.nn.sigmoid(ngt).reshape(b, t, NSA_KV_HEADS, NSA_GROUP, 3)
    o_n, win_k, win_v = nsa_fn(q_n, g_n, kc_r, vc_r, ks_r, vs_r, kw_r, vw_r)
    s_a, s_b, s_c = jnp.split(jax.nn.sigmoid(mgt), 3, axis=-1)
    merged = (s_a * (o_gla @ lw['w_br_gla']) + s_b * (o_conv @ lw['w_br_conv'])
              + s_c * (o_n.reshape(b, t, NSA_Q) @ lw['w_br_nsa']))
    return merged @ lw['w_o'], (kc_r, vc_r, ks_r, vs_r, win_k, win_v, s_fin.astype(h.dtype), conv_state)


def moe_dispatch(hf, idx, wts, wg, wu, wd):
    n, d = hf.shape
    a = n * TOP_K
    fe = idx.reshape(-1)
    ft = jnp.arange(a) // TOP_K
    fw = wts.reshape(-1)
    order = jnp.argsort(fe)
    se, st, sw = fe[order], ft[order], fw[order]
    counts = jnp.zeros((N_EXPERTS,), jnp.int32).at[fe].add(1)
    starts = jnp.cumsum(counts) - counts
    padded = (counts + MOE_BLOCK - 1) // MOE_BLOCK * MOE_BLOCK
    pend = jnp.cumsum(padded)
    pstart = pend - padded
    dest = pstart[se] + jnp.arange(a) - starts[se]
    nb = -(-a // MOE_BLOCK) + N_EXPERTS
    rows = nb * MOE_BLOCK
    row_tok = jnp.full((rows,), n, jnp.int32).at[dest].set(st)
    row_w = jnp.zeros((rows,), hf.dtype).at[dest].set(sw)
    blk_e = jnp.minimum(jnp.searchsorted(pend, jnp.arange(nb) * MOE_BLOCK, side='right'), N_EXPERTS - 1)
    h_pad = jnp.concatenate([hf, jnp.zeros((1, d), hf.dtype)], axis=0)

    def expert_block(inp):
        tok, e = inp
        xb = h_pad[tok]
        return (jax.nn.silu(xb @ wg[e]) * (xb @ wu[e])) @ wd[e]

    y = lax.map(expert_block, (row_tok.reshape(nb, MOE_BLOCK), blk_e))
    return jnp.zeros((n + 1, d), y.dtype).at[row_tok].add(y.reshape(rows, d) * row_w[:, None])[:n]


def moe(h, lw):
    b, t, d = h.shape
    n = b * t
    hf = h.reshape(n, d)
    s = jax.nn.sigmoid((hf @ lw['w_router']).astype(jnp.float32))
    sel = s + lw['router_bias'].astype(jnp.float32)
    per = N_EXPERTS // N_GROUPS
    gscore = lax.top_k(sel.reshape(n, N_GROUPS, per), 2)[0].sum(-1)
    gidx = lax.top_k(gscore, TOPK_GROUPS)[1]
    gmask = jax.nn.one_hot(gidx, N_GROUPS).sum(1) > 0
    sel = jnp.where(jnp.repeat(gmask, per, axis=1), sel, NEG_INF)
    idx = lax.top_k(sel, TOP_K)[1]
    w = jnp.take_along_axis(s, idx, axis=1)
    w = w / w.sum(-1, keepdims=True) * ROUTED_SCALE
    routed = moe_dispatch(hf, idx, w.astype(h.dtype), lw['w_exp_gate'], lw['w_exp_up'], lw['w_exp_down'])
    shared = (jax.nn.silu(hf @ lw['w_sh_gate']) * (hf @ lw['w_sh_up'])) @ lw['w_sh_down']
    return (routed + shared).reshape(b, t, d)


def block(x, c, lw, mixer):
    mod = (jax.nn.silu(c) @ lw['w_ada'] + lw['b_ada'])[:, None, :]
    sh1, sc1, g1, sh2, sc2, g2 = jnp.split(mod, 6, axis=-1)
    mix, state = mixer(x * (1 + sc1) + sh1)
    x = layer_norm(DN_ALPHA * x + g1 * mix, lw['ln1_g'], lw['ln1_b'])
    x = layer_norm(DN_ALPHA * x + g2 * moe(x * (1 + sc2) + sh2, lw), lw['ln2_g'], lw['ln2_b'])
    return x, state


def setup_inputs(seed: int = 0) -> dict:
    key = jax.random.key(seed)
    keys = iter(jax.random.split(key, 64))
    f32 = jnp.float32

    def nrm(shape, scale=1.0):
        return jax.random.normal(next(keys), shape, f32) * scale

    n_pages = PAST_LEN // PAGE_SIZE
    n_pool = (DEC_BATCH * n_pages * 5) // 4
    pool_shape = (DEPTH, n_pool, NSA_KV_HEADS, PAGE_SIZE, HEAD_DIM)
    win_buf = min(WINDOW, PAST_LEN)
    d = D_MODEL
    x_prompt = nrm((BATCH, SEQ, d))
    x_sample = nrm((DEC_BATCH, DEC_SEQ, d))
    cache_cmp_k = nrm(pool_shape)
    cache_cmp_v = nrm(pool_shape)
    cache_slc_k = nrm(pool_shape)
    cache_slc_v = nrm(pool_shape)
    cache_win_k = nrm((DEPTH, DEC_BATCH, NSA_KV_HEADS, win_buf, HEAD_DIM))
    cache_win_v = nrm((DEPTH, DEC_BATCH, NSA_KV_HEADS, win_buf, HEAD_DIM))
    state_gla = nrm((DEPTH, DEC_BATCH, GLA_HEADS, GLA_DK, GLA_DV))
    state_conv = nrm((DEPTH, DEC_BATCH, CONV_W - 1, CONV_DIM))
    page_table = jax.random.permutation(next(keys), n_pool)[:DEC_BATCH * n_pages].reshape(DEC_BATCH, n_pages).astype(jnp.int32)
    return {
        'x_prompt': x_prompt,
        'x_sample': x_sample,
        'cache_cmp_k': cache_cmp_k,
        'cache_cmp_v': cache_cmp_v,
        'cache_slc_k': cache_slc_k,
        'cache_slc_v': cache_slc_v,
        'cache_win_k': cache_win_k,
        'cache_win_v': cache_win_v,
        'state_gla': state_gla,
        'state_conv': state_conv,
        'page_table': page_table,
        'c_prompt': nrm((BATCH, d)),
        'c_sample': nrm((DEC_BATCH, d)),
        'w_ada': nrm((DEPTH, d, 6 * d), 0.5 * d ** -0.5),
        'b_ada': nrm((DEPTH, 6 * d), 0.02),
        'w_in': nrm((DEPTH, d, IN_TOTAL), d ** -0.5),
        'gla_wa2': nrm((DEPTH, GLA_RANK, GLA_QK), GLA_RANK ** -0.5),
        'gla_ba': nrm((DEPTH, GLA_QK), 0.1),
        'gla_norm_g': 1.0 + nrm((DEPTH, GLA_V), 0.02),
        'conv_w': nrm((DEPTH, CONV_W, CONV_DIM), CONV_W ** -0.5),
        'cmp_pe_k': nrm((DEPTH, CMP_BLOCK, HEAD_DIM), 0.02),
        'cmp_pe_v': nrm((DEPTH, CMP_BLOCK, HEAD_DIM), 0.02),
        'cmp_w1_k': nrm((DEPTH, CMP_BLOCK * HEAD_DIM, CMP_HIDDEN), (CMP_BLOCK * HEAD_DIM) ** -0.5),
        'cmp_w2_k': nrm((DEPTH, CMP_HIDDEN, HEAD_DIM), CMP_HIDDEN ** -0.5),
        'cmp_w1_v': nrm((DEPTH, CMP_BLOCK * HEAD_DIM, CMP_HIDDEN), (CMP_BLOCK * HEAD_DIM) ** -0.5),
        'cmp_w2_v': nrm((DEPTH, CMP_HIDDEN, HEAD_DIM), CMP_HIDDEN ** -0.5),
        'rel_bias': nrm((REL_BUCKETS, NSA_HEADS), 0.5),
        'w_br_gla': nrm((DEPTH, GLA_V, d), DN_BETA * GLA_V ** -0.5),
        'w_br_conv': nrm((DEPTH, CONV_DIM, d), DN_BETA * CONV_DIM ** -0.5),
        'w_br_nsa': nrm((DEPTH, NSA_Q, d), DN_BETA * NSA_Q ** -0.5),
        'w_o': nrm((DEPTH, d, d), DN_BETA * d ** -0.5),
        'ln1_g': 1.0 + nrm((DEPTH, d), 0.02),
        'ln1_b': nrm((DEPTH, d), 0.02),
        'ln2_g': 1.0 + nrm((DEPTH, d), 0.02),
        'ln2_b': nrm((DEPTH, d), 0.02),
        'w_router': nrm((DEPTH, d, N_EXPERTS), d ** -0.5),
        'router_bias': nrm((DEPTH, N_EXPERTS), 0.01),
        'w_exp_gate': nrm((DEPTH, N_EXPERTS, d, EXPERT_HIDDEN), d ** -0.5),
        'w_exp_up': nrm((DEPTH, N_EXPERTS, d, EXPERT_HIDDEN), d ** -0.5),
        'w_exp_down': nrm((DEPTH, N_EXPERTS, EXPERT_HIDDEN, d), DN_BETA * EXPERT_HIDDEN ** -0.5),
        'w_sh_gate': nrm((DEPTH, d, SHARED_HIDDEN), d ** -0.5),
        'w_sh_up': nrm((DEPTH, d, SHARED_HIDDEN), d ** -0.5),
        'w_sh_down': nrm((DEPTH, SHARED_HIDDEN, d), DN_BETA * SHARED_HIDDEN ** -0.5),
    }


def reference(x_prompt, x_sample, cache_cmp_k, cache_cmp_v, cache_slc_k, cache_slc_v, cache_win_k, cache_win_v,
              state_gla, state_conv, page_table, c_prompt, c_sample, w_ada, b_ada, w_in, gla_wa2, gla_ba,
              gla_norm_g, conv_w, cmp_pe_k, cmp_pe_v, cmp_w1_k, cmp_w2_k, cmp_w1_v, cmp_w2_v, rel_bias,
              w_br_gla, w_br_conv, w_br_nsa, w_o, ln1_g, ln1_b, ln2_g, ln2_b, w_router, router_bias,
              w_exp_gate, w_exp_up, w_exp_down, w_sh_gate, w_sh_up, w_sh_down):
    tbl = rel_bias.reshape(REL_BUCKETS, NSA_KV_HEADS, NSA_GROUP)
    xp, xs = x_prompt, x_sample
    p_states, s_states = [], []
    for l in range(DEPTH):
        lw = {
            'w_ada': w_ada[l], 'b_ada': b_ada[l], 'w_in': w_in[l], 'gla_wa2': gla_wa2[l], 'gla_ba': gla_ba[l],
            'gla_norm_g': gla_norm_g[l], 'conv_w': conv_w[l], 'cmp_pe_k': cmp_pe_k[l], 'cmp_pe_v': cmp_pe_v[l],
            'cmp_w1_k': cmp_w1_k[l], 'cmp_w2_k': cmp_w2_k[l], 'cmp_w1_v': cmp_w1_v[l], 'cmp_w2_v': cmp_w2_v[l],
            'w_br_gla': w_br_gla[l], 'w_br_conv': w_br_conv[l], 'w_br_nsa': w_br_nsa[l], 'w_o': w_o[l],
            'ln1_g': ln1_g[l], 'ln1_b': ln1_b[l], 'ln2_g': ln2_g[l], 'ln2_b': ln2_b[l],
            'w_router': w_router[l], 'router_bias': router_bias[l], 'w_exp_gate': w_exp_gate[l],
            'w_exp_up': w_exp_up[l], 'w_exp_down': w_exp_down[l], 'w_sh_gate': w_sh_gate[l],
            'w_sh_up': w_sh_up[l], 'w_sh_down': w_sh_down[l],
        }
        prompt_mixer = functools.partial(
            token_mixer, lw=lw,
            gla_s0=jnp.zeros((xp.shape[0], GLA_HEADS, GLA_DK, GLA_DV), jnp.float32),
            conv_prev=jnp.zeros((xp.shape[0], CONV_W - 1, CONV_DIM), xp.dtype),
            nsa_fn=functools.partial(nsa_prompt, lw=lw, tbl=tbl))
        sample_mixer = functools.partial(
            token_mixer, lw=lw, gla_s0=state_gla[l], conv_prev=state_conv[l],
            nsa_fn=functools.partial(nsa_sample, lw=lw, tbl=tbl, layer=l, pool_ck=cache_cmp_k, pool_cv=cache_cmp_v,
                                     pool_sk=cache_slc_k, pool_sv=cache_slc_v, win_k=cache_win_k[l],
                                     win_v=cache_win_v[l], page_table=page_table))
        xp, st = block(xp, c_prompt, lw, prompt_mixer)
        p_states.append(st)
        xs, st = block(xs, c_sample, lw, sample_mixer)
        s_states.append(st)
    p_cmp_k, p_cmp_v, p_slc_k, p_slc_v, p_win_k, p_win_v, p_gla, p_conv = [jnp.stack(z) for z in zip(*p_states)]
    s_cmp_k, s_cmp_v, s_slc_k, s_slc_v, s_win_k, s_win_v, s_gla, s_conv = [jnp.stack(z) for z in zip(*s_states)]
    return (xp, xs, p_cmp_k, p_cmp_v, p_slc_k, p_slc_v, p_win_k, p_win_v, p_gla, p_conv,
            s_cmp_k, s_cmp_v, s_slc_k, s_slc_v, s_win_k, s_win_v, s_gla, s_conv)
```

```python
import math, functools
import jax, jax.numpy as jnp
from jax import lax
import numpy as np
from jax.experimental import pallas as pl
from jax.experimental.pallas import tpu as pltpu

D_MODEL = 1024
DEPTH = 2
PAGE_SIZE = 128
GLA_HEADS = 4
GLA_DK = 128
GLA_DV = 256
GLA_RANK = 16
GLA_TAU = 16.0
GLA_CHUNK = 64
CONV_DIM = D_MODEL
CONV_W = 3
NSA_HEADS = 16
NSA_KV_HEADS = 4
NSA_GROUP = NSA_HEADS // NSA_KV_HEADS
HEAD_DIM = 64
CMP_BLOCK = 32
CMP_STRIDE = 16
CMP_HIDDEN = 128
SLC_BLOCK = 64
SLC_TOPK = 16
WINDOW = 512
NSA_QCHUNK = 16
REL_BUCKETS = 32
REL_MAX_DIST = 128
N_EXPERTS = 64
TOP_K = 8
N_GROUPS = 8
TOPK_GROUPS = 4
EXPERT_HIDDEN = 256
SHARED_HIDDEN = 256
ROUTED_SCALE = 2.5
MOE_BLOCK = 128
DN_ALPHA = (2 * DEPTH) ** 0.25
LN_EPS = 1e-5
NEG_INF = -1e30

GLA_QK = GLA_HEADS * GLA_DK
GLA_V = GLA_HEADS * GLA_DV
NSA_Q = NSA_HEADS * HEAD_DIM
NSA_KV = NSA_KV_HEADS * HEAD_DIM
IN_SIZES = (GLA_QK, GLA_QK, GLA_V, GLA_V, GLA_RANK, CONV_DIM, CONV_DIM, CONV_DIM, NSA_Q, NSA_KV, NSA_KV, NSA_KV, NSA_KV, NSA_KV, NSA_KV, 3 * NSA_HEADS, 3 * D_MODEL)
IN_TOTAL = sum(IN_SIZES)


BF16 = jnp.bfloat16
F32 = jnp.float32
LANES = 128
NSA_TILE = 128
GLA_ROWS = 256
VMEM_LIMIT = 48 * 1024 * 1024

PROJ_NAMES = ('gq', 'gk', 'gv', 'gg', 'ga', 'cb', 'cc', 'cx', 'nq', 'nkc', 'nvc', 'nks', 'nvs', 'nkw', 'nvw', 'ngt', 'mgt')
PROJ_ORDER = ('gv', 'gg', 'cb', 'cc', 'cx', 'nq', 'mgt', 'gq', 'gk', 'nkc', 'nvc', 'nks', 'nvs', 'nkw', 'nvw', 'ga', 'ngt')
PROJ_SIZE = dict(zip(PROJ_NAMES, IN_SIZES))
PROJ_SRC = dict(zip(PROJ_NAMES, np.cumsum((0,) + IN_SIZES[:-1]).tolist()))
PROJ_OFF = dict(zip(PROJ_ORDER, np.cumsum([0] + [PROJ_SIZE[n] for n in PROJ_ORDER[:-1]]).tolist()))
PROJ_TN = 512
PROJ_COLS = -(-IN_TOTAL // PROJ_TN) * PROJ_TN


def _dot_nt(a, b):
    return lax.dot_general(a, b, (((1,), (1,)), ((), ())), preferred_element_type=F32)


def _dot(a, b):
    return jnp.dot(a, b, preferred_element_type=F32)


def _dot_hi(a, b):
    return jnp.dot(a, b, precision=lax.Precision.HIGHEST, preferred_element_type=F32)


def _in_proj_kernel(x_ref, sc_ref, sh_ref, w_ref, o_ref, h_ref):
    @pl.when(pl.program_id(2) == 0)
    def _():
        h_ref[...] = (x_ref[0] * (1.0 + sc_ref[0]) + sh_ref[0]).astype(BF16)
    o_ref[0] = _dot(h_ref[...], w_ref[...])


def in_proj(x, sc, sh, w_in):
    b, t, d = x.shape
    w = jnp.concatenate([w_in[:, PROJ_SRC[n]:PROJ_SRC[n] + PROJ_SIZE[n]] for n in PROJ_ORDER]
                        + [jnp.zeros((d, PROJ_COLS - IN_TOTAL), w_in.dtype)], axis=1).astype(BF16)
    tm = math.gcd(t, 1024)
    return pl.pallas_call(
        _in_proj_kernel,
        grid=(b, t // tm, PROJ_COLS // PROJ_TN),
        in_specs=[pl.BlockSpec((1, tm, d), lambda i, j, k: (i, j, 0)),
                  pl.BlockSpec((1, 1, d), lambda i, j, k: (i, 0, 0)),
                  pl.BlockSpec((1, 1, d), lambda i, j, k: (i, 0, 0)),
                  pl.BlockSpec((d, PROJ_TN), lambda i, j, k: (0, k))],
        out_specs=pl.BlockSpec((1, tm, PROJ_TN), lambda i, j, k: (i, j, k)),
        out_shape=jax.ShapeDtypeStruct((b, t, PROJ_COLS), F32),
        scratch_shapes=[pltpu.VMEM((tm, d), BF16)],
        compiler_params=pltpu.CompilerParams(vmem_limit_bytes=VMEM_LIMIT),
        name="in_proj",
    )(x, sc.reshape(b, 1, d), sh.reshape(b, 1, d), w)


def proj_cols(y, name):
    return y[..., PROJ_OFF[name]:PROJ_OFF[name] + PROJ_SIZE[name]]


def _gla_kernel(q_ref, k_ref, v_ref, gg_ref, sm_ref, wa2_ref, ba_ref, ng_ref, s0_ref, o_ref, sfin_ref, st_ref,
                *, chunk, nchunks):
    tb = pl.program_id(2)
    c = chunk

    @pl.when(tb == 0)
    def _():
        st_ref[...] = s0_ref[0, 0].T

    row = lax.broadcasted_iota(jnp.int32, (c, c), 0)
    col = lax.broadcasted_iota(jnp.int32, (c, c), 1)
    causal = row >= col
    tri = jnp.where(causal, 1.0, 0.0)
    ga_off = PROJ_OFF['ga'] % LANES
    for ci in range(nchunks):
        sl = slice(ci * c, (ci + 1) * c)
        q = q_ref[0, sl, :] * (GLA_DK ** -0.5)
        k = k_ref[0, sl, :]
        v = v_ref[0, sl, :].astype(BF16)
        ga = sm_ref[0, sl, ga_off:ga_off + GLA_RANK]
        x = _dot_hi(ga, wa2_ref[...]) + ba_ref[...]
        log_a = (jnp.minimum(x, 0.0) - jnp.log1p(jnp.exp(-jnp.abs(x)))) * (1.0 / GLA_TAU)
        cum = _dot_hi(tri, log_a)
        last = cum[c - 1:c, :]
        mid = cum[c // 2 - 1:c // 2, :]
        st = st_ref[...]
        inter = _dot_nt((q * jnp.exp(cum)).astype(BF16), st.astype(BF16))
        att = _dot_nt((q * jnp.exp(cum - mid)).astype(BF16), (k * jnp.exp(mid - cum)).astype(BF16))
        att = jnp.where(causal, att, 0.0)
        o = inter + _dot(att.astype(BF16), v)
        kd = (k * jnp.exp(last - cum)).astype(BF16)
        st_ref[...] = st * jnp.exp(last) + lax.dot_general(v, kd, (((0,), (0,)), ((), ())),
                                                           preferred_element_type=F32)
        mu = jnp.mean(o, axis=-1, keepdims=True)
        var = jnp.mean(jnp.square(o - mu), axis=-1, keepdims=True)
        gate = gg_ref[0, sl, :]
        o_ref[0, sl, :] = (o - mu) * lax.rsqrt(var + LN_EPS) * ng_ref[...] * (gate * jax.nn.sigmoid(gate))

    @pl.when(tb == pl.num_programs(2) - 1)
    def _():
        sfin_ref[0, 0] = st_ref[...].T


def gla_pallas(y, lw, s0):
    b, t, _ = y.shape
    c = math.gcd(t, GLA_CHUNK)
    r = math.gcd(t, GLA_ROWS)
    qb, kb = PROJ_OFF['gq'] // GLA_DK, PROJ_OFF['gk'] // GLA_DK
    vb, gb = PROJ_OFF['gv'] // GLA_DV, PROJ_OFF['gg'] // GLA_DV
    sb = PROJ_OFF['ga'] // LANES
    return pl.pallas_call(
        functools.partial(_gla_kernel, chunk=c, nchunks=r // c),
        grid=(b, GLA_HEADS, t // r),
        in_specs=[pl.BlockSpec((1, r, GLA_DK), lambda i, h, j: (i, j, qb + h)),
                  pl.BlockSpec((1, r, GLA_DK), lambda i, h, j: (i, j, kb + h)),
                  pl.BlockSpec((1, r, GLA_DV), lambda i, h, j: (i, j, vb + h)),
                  pl.BlockSpec((1, r, GLA_DV), lambda i, h, j: (i, j, gb + h)),
                  pl.BlockSpec((1, r, LANES), lambda i, h, j: (i, j, sb)),
                  pl.BlockSpec((GLA_RANK, GLA_DK), lambda i, h, j: (0, h)),
                  pl.BlockSpec((1, GLA_DK), lambda i, h, j: (0, h)),
                  pl.BlockSpec((1, GLA_DV), lambda i, h, j: (0, h)),
                  pl.BlockSpec((1, 1, GLA_DK, GLA_DV), lambda i, h, j: (i, h, 0, 0))],
        out_specs=[pl.BlockSpec((1, r, GLA_DV), lambda i, h, j: (i, j, h)),
                   pl.BlockSpec((1, 1, GLA_DK, GLA_DV), lambda i, h, j: (i, h, 0, 0))],
        out_shape=[jax.ShapeDtypeStruct((b, t, GLA_V), F32),
                   jax.ShapeDtypeStruct((b, GLA_HEADS, GLA_DK, GLA_DV), F32)],
        scratch_shapes=[pltpu.VMEM((GLA_DV, GLA_DK), F32)],
        name="gla_scan",
    )(y, y, y, y, y, lw['gla_wa2'], lw['gla_ba'].reshape(1, -1), lw['gla_norm_g'].reshape(1, -1), s0.astype(F32))


def _split3(x):
    hi = x.astype(BF16)
    r = x - hi.astype(F32)
    mid = r.astype(BF16)
    lo = (r - mid.astype(F32)).astype(BF16)
    return hi, mid, lo


def _gelu_tanh(x):
    return 0.5 * x * (1.0 + jnp.tanh(math.sqrt(2.0 / math.pi) * (x + 0.044715 * (x * x * x))))


def _compress_kernel(chk_ref, chv_ref, pek_ref, pev_ref, w1k_ref, w1v_ref, w2k_ref, w2v_ref, ok_ref, ov_ref):
    rows = CMP_STRIDE * HEAD_DIM
    for ch_ref, pe_ref, w1_ref, w2_ref, o_ref in ((chk_ref, pek_ref, w1k_ref, w2k_ref, ok_ref),
                                                   (chv_ref, pev_ref, w1v_ref, w2v_ref, ov_ref)):
        ch = ch_ref[0, 0].astype(BF16)
        nch = ch.shape[0]
        w1 = w1_ref[...]
        pe = jnp.broadcast_to(pe_ref[...], (8, pe_ref.shape[1])).astype(BF16)
        hid = _dot(pe, w1)[0:1, :]
        a0 = _dot(ch, w1[0:rows, :])
        a1 = _dot(ch, w1[rows:2 * rows, :])
        hid = hid + a0 + pltpu.roll(a1, nch - 1, axis=0)
        o_ref[0, 0] = _dot(_gelu_tanh(hid).astype(BF16), w2_ref[...])


def compress_pair(kc_r, vc_r, lw):
    b, kvh, length, hd = kc_r.shape
    nch = length // CMP_STRIDE
    chk = kc_r.reshape(b, kvh, nch, CMP_STRIDE * hd)
    chv = vc_r.reshape(b, kvh, nch, CMP_STRIDE * hd)
    ch_spec = pl.BlockSpec((1, 1, nch, CMP_STRIDE * hd), lambda i, j: (i, j, 0, 0))
    full = lambda a: pl.BlockSpec(a.shape, lambda i, j: (0,) * a.ndim)
    pek = lw['cmp_pe_k'].reshape(1, -1)
    pev = lw['cmp_pe_v'].reshape(1, -1)
    w1k, w1v = lw['cmp_w1_k'].astype(BF16), lw['cmp_w1_v'].astype(BF16)
    w2k, w2v = lw['cmp_w2_k'].astype(BF16), lw['cmp_w2_v'].astype(BF16)
    o_spec = pl.BlockSpec((1, 1, nch, hd), lambda i, j: (i, j, 0, 0))
    o_shape = jax.ShapeDtypeStruct((b, kvh, nch, hd), F32)
    return pl.pallas_call(
        _compress_kernel,
        grid=(b, kvh),
        in_specs=[ch_spec, ch_spec, full(pek), full(pev), full(w1k), full(w1v), full(w2k), full(w2v)],
        out_specs=[o_spec, o_spec],
        out_shape=[o_shape, o_shape],
        name="nsa_compress",
    )(chk, chv, pek, pev, w1k, w1v, w2k, w2v)


def _online_update(m, l, acc, s, mask, v):
    s = jnp.where(mask, s, NEG_INF)
    m_new = jnp.maximum(m, jnp.max(s, axis=-1, keepdims=True))
    alpha = jnp.exp(m - m_new)
    p = jnp.where(mask, jnp.exp(s - m_new), 0.0)
    l = alpha * l + jnp.sum(p, axis=-1, keepdims=True)
    acc = alpha * acc + _dot(p.astype(BF16), v)
    return m_new, l, acc


def _nsa_prompt_kernel(q_ref, g_ref, kc_ref, vc_ref, ks_ref, vs_ref, kw_ref, vw_ref, bc_ref, bt_ref, ov_ref, o_ref,
                       *, nsb):
    tq = NSA_TILE
    grp = NSA_GROUP
    ncp = kc_ref.shape[2]
    qi = pl.program_id(2)
    q0 = qi * tq
    q = q_ref[0] * (HEAD_DIM ** -0.5)
    gates = jax.nn.sigmoid(g_ref[0, 0])
    qpos = q0 + lax.broadcasted_iota(jnp.int32, (tq, 1), 0)
    qs = [q[:, g * HEAD_DIM:(g + 1) * HEAD_DIM].astype(BF16) for g in range(grp)]

    kc = kc_ref[0, 0].astype(BF16)
    vc = vc_ref[0, 0].astype(BF16)
    cend = CMP_STRIDE * lax.broadcasted_iota(jnp.int32, (1, ncp), 1) + (CMP_BLOCK - 1)
    mask_c = qpos >= cend
    psum = jnp.zeros((tq, ncp), F32)
    o_c = []
    for g in range(grp):
        lc = jnp.where(mask_c, _dot_nt(qs[g], kc) + bc_ref[0, g], NEG_INF)
        m = jnp.max(lc, axis=-1, keepdims=True)
        p = jnp.where(mask_c, jnp.exp(lc - m), 0.0)
        l = jnp.sum(p, axis=-1, keepdims=True)
        pc = p / jnp.where(l > 0.0, l, 1.0)
        psum = psum + pc
        o_c.append(_dot(pc.astype(BF16), vc))

    ov = ov_ref[...]
    hi, mid, lo = _split3(psum)
    imp = _dot(hi, ov) + _dot(mid, ov) + _dot(lo, ov)
    blk = lax.broadcasted_iota(jnp.int32, (1, 128), 1)
    cur = lax.shift_right_logical(qpos, 6)
    forced = (blk == 0) | (blk == cur) | (blk == cur - 1)
    visible = blk * SLC_BLOCK <= qpos
    score = jnp.where(forced, -NEG_INF, imp)
    score = jnp.where(visible, score, NEG_INF)
    rank = jnp.zeros((tq, 128), jnp.int32)
    for i in range(nsb):
        si = score[:, i:i + 1]
        beats = (si > score) | ((si == score) & (blk > i))
        rank = rank + beats.astype(jnp.int32)
    sel = jnp.where(visible & (rank < SLC_TOPK), 1.0, 0.0).astype(BF16)

    zero1 = jnp.zeros((tq, 1), F32)
    neg1 = jnp.full((tq, 1), NEG_INF, F32)
    zacc = jnp.zeros((tq, HEAD_DIM), F32)

    def slc_body(kt, carry):
        ms, ls, accs = carry
        k0 = pl.multiple_of(kt * tq, tq)
        k_t = ks_ref[0, 0, pl.ds(k0, tq), :].astype(BF16)
        v_t = vs_ref[0, 0, pl.ds(k0, tq), :].astype(BF16)
        kpos = k0 + lax.broadcasted_iota(jnp.int32, (1, tq), 1)
        kblk = lax.shift_right_logical(k0 + lax.broadcasted_iota(jnp.int32, (128, tq), 1), 6)
        expand = jnp.where(lax.broadcasted_iota(jnp.int32, (128, tq), 0) == kblk, 1.0, 0.0).astype(BF16)
        mask = (_dot(sel, expand) > 0.5) & (kpos <= qpos)
        didx = jnp.minimum(qi - kt, 2)
        out = [_online_update(ms[g], ls[g], accs[g], _dot_nt(qs[g], k_t) + bt_ref[0, g, didx], mask, v_t)
               for g in range(grp)]
        return tuple(o[0] for o in out), tuple(o[1] for o in out), tuple(o[2] for o in out)

    init = ((neg1,) * grp, (zero1,) * grp, (zacc,) * grp)
    _, ls_s, acc_s = lax.fori_loop(0, qi + 1, slc_body, init)

    nwt = WINDOW // tq + 1
    ms_w, ls_w, acc_w = [neg1] * grp, [zero1] * grp, [zacc] * grp
    for j in range(nwt):
        dt = nwt - 1 - j
        kt = qi - dt
        k0 = pl.multiple_of(jnp.maximum(kt, 0) * tq, tq)
        k_t = kw_ref[0, 0, pl.ds(k0, tq), :].astype(BF16)
        v_t = vw_ref[0, 0, pl.ds(k0, tq), :].astype(BF16)
        dist = qpos - (k0 + lax.broadcasted_iota(jnp.int32, (1, tq), 1))
        mask = (dist >= 0) & (dist < WINDOW) & (kt >= 0)
        for g in range(grp):
            ms_w[g], ls_w[g], acc_w[g] = _online_update(
                ms_w[g], ls_w[g], acc_w[g], _dot_nt(qs[g], k_t) + bt_ref[0, g, min(dt, 2)], mask, v_t)

    outs = []
    for g in range(grp):
        o_s = acc_s[g] / jnp.where(ls_s[g] > 0.0, ls_s[g], 1.0)
        o_w = acc_w[g] / jnp.where(ls_w[g] > 0.0, ls_w[g], 1.0)
        outs.append(gates[:, 3 * g:3 * g + 1] * o_c[g] + gates[:, 3 * g + 1:3 * g + 2] * o_s
                    + gates[:, 3 * g + 2:3 * g + 3] * o_w)
    o_ref[0] = jnp.concatenate(outs, axis=-1)


def nsa_bias_tables(tbl, t, ncp):
    tq = NSA_TILE
    dist_c = jnp.arange(t)[:, None] - (CMP_STRIDE * jnp.arange(ncp) + CMP_BLOCK - 1)[None, :]
    bc = jnp.transpose(tbl[rel_bucket(dist_c)], (2, 3, 0, 1))
    rc = jnp.arange(tq)[:, None] - jnp.arange(tq)[None, :]
    tiles = jnp.stack([rc, rc + tq, jnp.full_like(rc, REL_MAX_DIST)])
    bt = jnp.transpose(tbl[rel_bucket(tiles)], (3, 4, 0, 1, 2))
    return bc, bt


def nsa_prompt_pallas(y, q_off, ngt, kc_r, vc_r, ks_r, vs_r, kw_r, vw_r, lw, bias_tables):
    b, t, _ = y.shape
    kvh, grp, hd, tq = NSA_KV_HEADS, NSA_GROUP, HEAD_DIM, NSA_TILE
    qb = q_off // (grp * hd)
    kc, vc = compress_pair(kc_r, vc_r, lw)
    ncp = kc.shape[2]
    nsb = t // SLC_BLOCK
    bc, bt = bias_tables
    gates = ngt.reshape(b, t, kvh, grp * 3).transpose(0, 2, 1, 3)
    ci = jnp.arange(ncp)[:, None] * CMP_STRIDE
    sj = jnp.arange(128)[None, :] * SLC_BLOCK
    ov = ((ci < sj + SLC_BLOCK) & (ci + CMP_BLOCK > sj) & (jnp.arange(ncp)[:, None] < ncp - 1)
          & (jnp.arange(128)[None, :] < nsb)).astype(BF16)
    row_spec = pl.BlockSpec((1, 1, t, hd), lambda i, j, k: (i, j, 0, 0))
    cmp_spec = pl.BlockSpec((1, 1, ncp, hd), lambda i, j, k: (i, j, 0, 0))
    return pl.pallas_call(
        functools.partial(_nsa_prompt_kernel, nsb=nsb),
        grid=(b, kvh, t // tq),
        in_specs=[pl.BlockSpec((1, tq, grp * hd), lambda i, j, k: (i, k, qb + j)),
                  pl.BlockSpec((1, 1, tq, grp * 3), lambda i, j, k: (i, j, k, 0)),
                  cmp_spec, cmp_spec, row_spec, row_spec, row_spec, row_spec,
                  pl.BlockSpec((1, grp, tq, ncp), lambda i, j, k: (j, 0, k, 0)),
                  pl.BlockSpec((1, grp, 3, tq, tq), lambda i, j, k: (j, 0, 0, 0, 0)),
                  pl.BlockSpec(ov.shape, lambda i, j, k: (0, 0))],
        out_specs=pl.BlockSpec((1, tq, grp * hd), lambda i, j, k: (i, k, j)),
        out_shape=jax.ShapeDtypeStruct((b, t, kvh * grp * hd), F32),
        compiler_params=pltpu.CompilerParams(vmem_limit_bytes=VMEM_LIMIT),
        name="nsa_prompt",
    )(y, gates, kc, vc, ks_r, vs_r, kw_r, vw_r, bc, bt, ov)


def layer_norm(x, g, b):
    xf = x.astype(jnp.float32)
    mu = xf.mean(-1, keepdims=True)
    var = jnp.square(xf - mu).mean(-1, keepdims=True)
    return ((xf - mu) * lax.rsqrt(var + LN_EPS) * g + b).astype(x.dtype)


def masked_softmax(logits, mask):
    p = jax.nn.softmax(jnp.where(mask, logits.astype(jnp.float32), NEG_INF), axis=-1)
    return jnp.where(mask, p, 0.0)


def rel_bucket(dist):
    n = jnp.maximum(dist, 0)
    exact = REL_BUCKETS // 2
    big = exact + (jnp.log(jnp.maximum(n, 1).astype(jnp.float32) / exact)
                   / math.log(REL_MAX_DIST / exact) * (REL_BUCKETS - exact)).astype(jnp.int32)
    return jnp.where(n < exact, n, jnp.minimum(big, REL_BUCKETS - 1))


def gla_scan(q, k, v, log_a, s0):
    b, t = q.shape[:2]
    c = math.gcd(t, GLA_CHUNK)
    nc = t // c
    causal = jnp.tril(jnp.ones((c, c), dtype=bool))

    def to_chunks(a):
        return a.reshape(b, nc, c, *a.shape[2:]).swapaxes(0, 1)

    def step(s, inp):
        qc, kc, vc, ac = inp
        cum = jnp.cumsum(ac, axis=1)
        inter = jnp.einsum('bthk,bhkv->bthv', qc * jnp.exp(cum), s)
        diff = cum[:, :, None] - cum[:, None, :]
        decay = jnp.exp(jnp.where(causal[None, :, :, None, None], diff, -jnp.inf))
        att = jnp.einsum('bthk,bshk,btshk->bhts', qc, kc, decay)
        intra = jnp.einsum('bhts,bshv->bthv', att, vc)
        last = cum[:, -1]
        s = jnp.exp(last)[..., None] * s + jnp.einsum('bshk,bshv->bhkv', kc * jnp.exp(last[:, None] - cum), vc)
        return s, inter + intra

    s_fin, o = lax.scan(step, s0, (to_chunks(q), to_chunks(k), to_chunks(v), to_chunks(log_a)))
    return o.swapaxes(0, 1).reshape(b, t, *v.shape[2:]), s_fin


def gla_output(o, gate, norm_g):
    mu = o.mean(-1, keepdims=True)
    var = jnp.square(o - mu).mean(-1, keepdims=True)
    on = ((o - mu) * lax.rsqrt(var + LN_EPS)).reshape(*o.shape[:2], -1)
    return on * norm_g * jax.nn.silu(gate.astype(jnp.float32))


def short_conv(b_gate, c_gate, x_in, prev, w):
    u = c_gate * x_in
    up = jnp.concatenate([prev, u], axis=1)
    t = u.shape[1]
    y = up[:, 0:t] * w[0]
    for j in range(1, CONV_W):
        y = y + up[:, j:j + t] * w[j]
    return b_gate * y, up[:, t:]


def compress(k, pe, w1, w2):
    b, kvh, length, hd = k.shape
    n_chunks = length // CMP_STRIDE
    pieces = CMP_BLOCK // CMP_STRIDE
    n_blocks = n_chunks - pieces + 1
    ch = k.reshape(b, kvh, n_chunks, CMP_STRIDE * hd)
    rows = CMP_STRIDE * hd
    hid = pe.reshape(-1) @ w1
    for r in range(pieces):
        hid = hid + (ch @ w1[r * rows:(r + 1) * rows])[:, :, r:r + n_blocks]
    return jax.nn.gelu(hid) @ w2


def slc_overlap(n_cmp, n_slc):
    start = jnp.arange(n_cmp) * CMP_STRIDE
    blk = jnp.arange(n_slc) * SLC_BLOCK
    return ((start[:, None] < blk[None, :] + SLC_BLOCK) & (start[:, None] + CMP_BLOCK > blk[None, :])).astype(jnp.float32)


def nsa_attend(q, qpos, gates, kc, vc, overlap, gather_slc, kw, vw, wpos, tbl):
    b, nq, kvh, grp, hd = q.shape
    scale = hd ** -0.5
    f32 = jnp.float32
    hi = jnp.arange(kvh)[:, None, None]
    cmp_end = CMP_STRIDE * jnp.arange(kc.shape[2]) + CMP_BLOCK - 1
    dist_c = qpos[:, None] - cmp_end[None, :]
    lc = jnp.einsum('bqhgd,bhnd->bqhgn', q, kc).astype(f32) * scale + jnp.moveaxis(tbl[rel_bucket(dist_c)], 1, -1)
    pc = masked_softmax(lc, (dist_c >= 0)[:, None, None, :])
    o_c = jnp.einsum('bqhgn,bhnd->bqhgd', pc, vc)
    imp = jnp.einsum('bqhn,ns->bqhs', pc.sum(3), overlap)
    blk = jnp.arange(overlap.shape[1])
    cur = (qpos // SLC_BLOCK)[:, None]
    forced = (blk[None] == 0) | (blk[None] == cur) | (blk[None] == cur - 1)
    visible = blk[None] * SLC_BLOCK <= qpos[:, None]
    score = jnp.where(forced[:, None], -NEG_INF, imp)
    score = jnp.where(visible[:, None], score, NEG_INF)
    top_s, idx = lax.top_k(score, min(SLC_TOPK, overlap.shape[1]))
    ks, vs = gather_slc(idx)
    kpos = idx[..., None] * SLC_BLOCK + jnp.arange(SLC_BLOCK)
    dist_s = qpos[None, :, None, None, None] - kpos
    mask_s = (dist_s >= 0) & (top_s > 0.5 * NEG_INF)[..., None]
    ls = jnp.einsum('bqhgd,bqhnsd->bqhgns', q, ks).astype(f32) * scale + jnp.moveaxis(tbl[rel_bucket(dist_s), hi], -1, 3)
    ps = masked_softmax(ls.reshape(b, nq, kvh, grp, -1), mask_s.reshape(b, nq, kvh, 1, -1))
    o_s = jnp.einsum('bqhgm,bqhmd->bqhgd', ps, vs.reshape(b, nq, kvh, -1, hd))
    dist_w = qpos[:, None] - wpos[None, :]
    mask_w = (dist_w >= 0) & (dist_w < WINDOW) & (wpos >= 0)[None, :]
    lwin = jnp.einsum('bqhgd,bhwd->bqhgw', q, kw).astype(f32) * scale + jnp.moveaxis(tbl[rel_bucket(dist_w)], 1, -1)
    pw = masked_softmax(lwin, mask_w[:, None, None, :])
    o_w = jnp.einsum('bqhgw,bhwd->bqhgd', pw, vw)
    o = gates[..., 0:1] * o_c + gates[..., 1:2] * o_s + gates[..., 2:3] * o_w
    return o.astype(q.dtype)


def nsa_prompt(q, gates, kc_r, vc_r, ks_r, vs_r, kw_r, vw_r, lw, tbl):
    b, s = q.shape[:2]
    kc = compress(kc_r, lw['cmp_pe_k'], lw['cmp_w1_k'], lw['cmp_w2_k'])
    vc = compress(vc_r, lw['cmp_pe_v'], lw['cmp_w1_v'], lw['cmp_w2_v'])
    nsb = s // SLC_BLOCK
    ks_blk = ks_r.reshape(b, NSA_KV_HEADS, nsb, SLC_BLOCK, HEAD_DIM)
    vs_blk = vs_r.reshape(b, NSA_KV_HEADS, nsb, SLC_BLOCK, HEAD_DIM)
    bi = jnp.arange(b)[:, None, None, None]
    hi = jnp.arange(NSA_KV_HEADS)[None, None, :, None]

    def gather_slc(idx):
        return ks_blk[bi, hi, idx], vs_blk[bi, hi, idx]

    overlap = slc_overlap(kc.shape[2], nsb)
    kw_pad = jnp.pad(kw_r, ((0, 0), (0, 0), (WINDOW, 0), (0, 0)))
    vw_pad = jnp.pad(vw_r, ((0, 0), (0, 0), (WINDOW, 0), (0, 0)))
    n_chunks = s // NSA_QCHUNK
    span = WINDOW + NSA_QCHUNK

    def chunks(a):
        return a.reshape(b, n_chunks, NSA_QCHUNK, *a.shape[2:]).swapaxes(0, 1)

    def body(inp):
        qc, gc, c0 = inp
        return nsa_attend(qc, c0 + jnp.arange(NSA_QCHUNK), gc, kc, vc, overlap, gather_slc,
                          lax.dynamic_slice_in_dim(kw_pad, c0, span, axis=2),
                          lax.dynamic_slice_in_dim(vw_pad, c0, span, axis=2),
                          c0 - WINDOW + jnp.arange(span), tbl)

    o = lax.map(body, (chunks(q), chunks(gates), jnp.arange(n_chunks) * NSA_QCHUNK))
    o = o.swapaxes(0, 1).reshape(b, s, *q.shape[2:])
    wb = min(WINDOW, s)
    return o, kw_r[:, :, s - wb:], vw_r[:, :, s - wb:]


def nsa_sample(q, gates, kc_r, vc_r, ks_r, vs_r, kw_r, vw_r, lw, tbl, layer, pool_ck, pool_cv, pool_sk, pool_sv, win_k, win_v, page_table):
    b, t = q.shape[:2]
    past = page_table.shape[1] * PAGE_SIZE
    t_cmp = -(-t // CMP_STRIDE) * CMP_STRIDE
    nnb = -(-t // SLC_BLOCK)
    npb = past // SLC_BLOCK
    bpp = PAGE_SIZE // SLC_BLOCK

    def past_rows(pool):
        g = pool[layer, page_table]
        return g.transpose(0, 2, 1, 3, 4).reshape(b, NSA_KV_HEADS, past, HEAD_DIM)

    def pad_t(a, n):
        return jnp.pad(a, ((0, 0), (0, 0), (0, n - a.shape[2]), (0, 0)))

    kc = compress(jnp.concatenate([past_rows(pool_ck), pad_t(kc_r, t_cmp)], axis=2), lw['cmp_pe_k'], lw['cmp_w1_k'], lw['cmp_w2_k'])
    vc = compress(jnp.concatenate([past_rows(pool_cv), pad_t(vc_r, t_cmp)], axis=2), lw['cmp_pe_v'], lw['cmp_w1_v'], lw['cmp_w2_v'])
    new_ks = pad_t(ks_r, nnb * SLC_BLOCK).reshape(b, NSA_KV_HEADS, nnb, SLC_BLOCK, HEAD_DIM)
    new_vs = pad_t(vs_r, nnb * SLC_BLOCK).reshape(b, NSA_KV_HEADS, nnb, SLC_BLOCK, HEAD_DIM)
    bi = jnp.arange(b)[:, None, None, None]
    hi = jnp.arange(NSA_KV_HEADS)[None, None, :, None]
    offs = jnp.arange(SLC_BLOCK)

    def gather_slc(idx):
        in_past = (idx < npb)[..., None, None]
        pidx = jnp.minimum(idx, npb - 1)
        page = page_table[bi, pidx // bpp][..., None]
        rows = ((pidx % bpp) * SLC_BLOCK)[..., None] + offs
        nidx = jnp.clip(idx - npb, 0, nnb - 1)
        ks = jnp.where(in_past, pool_sk[layer, page, hi[..., None], rows], new_ks[bi, hi, nidx])
        vs = jnp.where(in_past, pool_sv[layer, page, hi[..., None], rows], new_vs[bi, hi, nidx])
        return ks, vs

    kw = jnp.concatenate([win_k.astype(kw_r.dtype), kw_r], axis=2)
    vw = jnp.concatenate([win_v.astype(vw_r.dtype), vw_r], axis=2)
    wb = win_k.shape[2]
    o = nsa_attend(q, past + jnp.arange(t), gates, kc, vc, slc_overlap(kc.shape[2], npb + nnb), gather_slc,
                   kw, vw, past - wb + jnp.arange(wb + t), tbl)
    return o, kw[:, :, -wb:], vw[:, :, -wb:]


def token_mixer(h, lw, gla_s0, conv_prev, nsa_fn):
    b, t, _ = h.shape
    f32 = jnp.float32
    (gq, gk, gv, gg, ga, cb, cc, cx, nq, nkc, nvc, nks, nvs, nkw, nvw, ngt, mgt) = jnp.split(
        in_proj(h, lw['w_in']), np.cumsum(IN_SIZES)[:-1].tolist(), axis=-1)
    q = gq.reshape(b, t, GLA_HEADS, GLA_DK).astype(f32) * GLA_DK ** -0.5
    k = gk.reshape(b, t, GLA_HEADS, GLA_DK).astype(f32)
    v = gv.reshape(b, t, GLA_HEADS, GLA_DV).astype(f32)
    log_a = (jax.nn.log_sigmoid((ga @ lw['gla_wa2'] + lw['gla_ba']).astype(f32)) / GLA_TAU).reshape(b, t, GLA_HEADS, GLA_DK)
    o, s_fin = gla_scan(q, k, v, log_a, gla_s0.astype(f32))
    o_gla = gla_output(o, gg, lw['gla_norm_g']).astype(h.dtype)
    o_conv, conv_state = short_conv(cb, cc, cx, conv_prev.astype(h.dtype), lw['conv_w'])

    def rows(a):
        return a.reshape(b, t, NSA_KV_HEADS, HEAD_DIM).transpose(0, 2, 1, 3)
    kc_r, vc_r, ks_r, vs_r, kw_r, vw_r = [rows(a) for a in (nkc, nvc, nks, nvs, nkw, nvw)]
    q_n = nq.reshape(b, t, NSA_KV_HEADS, NSA_GROUP, HEAD_DIM)
    g_n = jax.nn.sigmoid(ngt).reshape(b, t, NSA_KV_HEADS, NSA_GROUP, 3)
    o_n, win_k, win_v = nsa_fn(q_n, g_n, kc_r, vc_r, ks_r, vs_r, kw_r, vw_r)
    s_a, s_b, s_c = jnp.split(jax.nn.sigmoid(mgt), 3, axis=-1)
    merged = (s_a * (o_gla @ lw['w_br_gla']) + s_b * (o_conv @ lw['w_br_conv'])
              + s_c * (o_n.reshape(b, t, NSA_Q) @ lw['w_br_nsa']))
    return merged @ lw['w_o'], (kc_r, vc_r, ks_r, vs_r, win_k, win_v, s_fin.astype(h.dtype), conv_state)


def moe_dispatch(hf, idx, wts, wg, wu, wd):
    n, d = hf.shape
    a = n * TOP_K
    fe = idx.reshape(-1)
    ft = jnp.arange(a) // TOP_K
    fw = wts.reshape(-1)
    order = jnp.argsort(fe)
    se, st, sw = fe[order], ft[order], fw[order]
    counts = jnp.zeros((N_EXPERTS,), jnp.int32).at[fe].add(1)
    starts = jnp.cumsum(counts) - counts
    padded = (counts + MOE_BLOCK - 1) // MOE_BLOCK * MOE_BLOCK
    pend = jnp.cumsum(padded)
    pstart = pend - padded
    dest = pstart[se] + jnp.arange(a) - starts[se]
    nb = -(-a // MOE_BLOCK) + N_EXPERTS
    rows = nb * MOE_BLOCK
    row_tok = jnp.full((rows,), n, jnp.int32).at[dest].set(st)
    row_w = jnp.zeros((rows,), hf.dtype).at[dest].set(sw)
    blk_e = jnp.minimum(jnp.searchsorted(pend, jnp.arange(nb) * MOE_BLOCK, side='right'), N_EXPERTS - 1)
    h_pad = jnp.concatenate([hf, jnp.zeros((1, d), hf.dtype)], axis=0)

    def expert_block(inp):
        tok, e = inp
        xb = h_pad[tok]
        return (jax.nn.silu(xb @ wg[e]) * (xb @ wu[e])) @ wd[e]

    y = lax.map(expert_block, (row_tok.reshape(nb, MOE_BLOCK), blk_e))
    return jnp.zeros((n + 1, d), y.dtype).at[row_tok].add(y.reshape(rows, d) * row_w[:, None])[:n]


def moe(h, lw):
    b, t, d = h.shape
    n = b * t
    hf = h.reshape(n, d)
    s = jax.nn.sigmoid((hf @ lw['w_router']).astype(jnp.float32))
    sel = s + lw['router_bias'].astype(jnp.float32)
    per = N_EXPERTS // N_GROUPS
    gscore = lax.top_k(sel.reshape(n, N_GROUPS, per), 2)[0].sum(-1)
    gidx = lax.top_k(gscore, TOPK_GROUPS)[1]
    gmask = jax.nn.one_hot(gidx, N_GROUPS).sum(1) > 0
    sel = jnp.where(jnp.repeat(gmask, per, axis=1), sel, NEG_INF)
    idx = lax.top_k(sel, TOP_K)[1]
    w = jnp.take_along_axis(s, idx, axis=1)
    w = w / w.sum(-1, keepdims=True) * ROUTED_SCALE
    routed = moe_dispatch(hf, idx, w.astype(h.dtype), lw['w_exp_gate'], lw['w_exp_up'], lw['w_exp_down'])
    shared = (jax.nn.silu(hf @ lw['w_sh_gate']) * (hf @ lw['w_sh_up'])) @ lw['w_sh_down']
    return (routed + shared).reshape(b, t, d)


def block(x, c, lw, mixer):
    mod = (jax.nn.silu(c) @ lw['w_ada'] + lw['b_ada'])[:, None, :]
    sh1, sc1, g1, sh2, sc2, g2 = jnp.split(mod, 6, axis=-1)
    mix, state = mixer(x * (1 + sc1) + sh1)
    x = layer_norm(DN_ALPHA * x + g1 * mix, lw['ln1_g'], lw['ln1_b'])
    x = layer_norm(DN_ALPHA * x + g2 * moe(x * (1 + sc2) + sh2, lw), lw['ln2_g'], lw['ln2_b'])
    return x, state


def kernel(x_prompt, x_sample, cache_cmp_k, cache_cmp_v, cache_slc_k, cache_slc_v, cache_win_k, cache_win_v,
           state_gla, state_conv, page_table, c_prompt, c_sample, w_ada, b_ada, w_in, gla_wa2, gla_ba,
           gla_norm_g, conv_w, cmp_pe_k, cmp_pe_v, cmp_w1_k, cmp_w2_k, cmp_w1_v, cmp_w2_v, rel_bias,
           w_br_gla, w_br_conv, w_br_nsa, w_o, ln1_g, ln1_b, ln2_g, ln2_b, w_router, router_bias,
           w_exp_gate, w_exp_up, w_exp_down, w_sh_gate, w_sh_up, w_sh_down):
    tbl = rel_bias.reshape(REL_BUCKETS, NSA_KV_HEADS, NSA_GROUP)
    xp, xs = x_prompt, x_sample
    p_states, s_states = [], []
    for l in range(DEPTH):
        lw = {
            'w_ada': w_ada[l], 'b_ada': b_ada[l], 'w_in': w_in[l], 'gla_wa2': gla_wa2[l], 'gla_ba': gla_ba[l],
            'gla_norm_g': gla_norm_g[l], 'conv_w': conv_w[l], 'cmp_pe_k': cmp_pe_k[l], 'cmp_pe_v': cmp_pe_v[l],
            'cmp_w1_k': cmp_w1_k[l], 'cmp_w2_k': cmp_w2_k[l], 'cmp_w1_v': cmp_w1_v[l], 'cmp_w2_v': cmp_w2_v[l],
            'w_br_gla': w_br_gla[l], 'w_br_conv': w_br_conv[l], 'w_br_nsa': w_br_nsa[l], 'w_o': w_o[l],
            'ln1_g': ln1_g[l], 'ln1_b': ln1_b[l], 'ln2_g': ln2_g[l], 'ln2_b': ln2_b[l],
            'w_router': w_router[l], 'router_bias': router_bias[l], 'w_exp_gate': w_exp_gate[l],
            'w_exp_up': w_exp_up[l], 'w_exp_down': w_exp_down[l], 'w_sh_gate': w_sh_gate[l],
            'w_sh_up': w_sh_up[l], 'w_sh_down': w_sh_down[l],
        }
        prompt_mixer = functools.partial(
            token_mixer, lw=lw,
            gla_s0=jnp.zeros((xp.shape[0], GLA_HEADS, GLA_DK, GLA_DV), jnp.float32),
            conv_prev=jnp.zeros((xp.shape[0], CONV_W - 1, CONV_DIM), xp.dtype),
            nsa_fn=functools.partial(nsa_prompt, lw=lw, tbl=tbl))
        sample_mixer = functools.partial(
            token_mixer, lw=lw, gla_s0=state_gla[l], conv_prev=state_conv[l],
            nsa_fn=functools.partial(nsa_sample, lw=lw, tbl=tbl, layer=l, pool_ck=cache_cmp_k, pool_cv=cache_cmp_v,
                                     pool_sk=cache_slc_k, pool_sv=cache_slc_v, win_k=cache_win_k[l],
                                     win_v=cache_win_v[l], page_table=page_table))
        xp, st = block(xp, c_prompt, lw, prompt_mixer)
        p_states.append(st)
        xs, st = block(xs, c_sample, lw, sample_mixer)
        s_states.append(st)
    p_out = [jnp.stack(z) for z in zip(*p_states)]
    s_out = [jnp.stack(z) for z in zip(*s_states)]
    return (xp, xs, *p_out, *s_out)


MERGE_ROWS = 256
CONV_HALO = 8


def _layer_norm_rows(z, g, b):
    mu = jnp.mean(z, axis=-1, keepdims=True)
    var = jnp.mean(jnp.square(z - mu), axis=-1, keepdims=True)
    return (z - mu) * lax.rsqrt(var + LN_EPS) * g + b


def _merge_kernel(x_ref, cb_ref, cc_ref, cx_ref, ccp_ref, cxp_ref, prev_ref, ma_ref, mb_ref, mc_ref, og_ref, on_ref,
                  cw_ref, wg_ref, wc_ref, wn_ref, wo_ref, g1_ref, sc2_ref, sh2_ref, lng_ref, lnb_ref, wr_ref,
                  x1_ref, h2_ref, s_ref, cst_ref):
    j = pl.program_id(1)
    tm = x_ref.shape[1]
    u = cc_ref[0] * cx_ref[0]
    halo = jnp.where(j == 0, prev_ref[0], ccp_ref[0] * cxp_ref[0])
    p1 = halo[CONV_HALO - 1:CONV_HALO, :]
    p2 = halo[CONV_HALO - 2:CONV_HALO - 1, :]
    rid = lax.broadcasted_iota(jnp.int32, (tm, 1), 0)
    u1 = jnp.where(rid == 0, p1, pltpu.roll(u, 1, axis=0))
    u2 = jnp.where(rid == 0, p2, jnp.where(rid == 1, p1, pltpu.roll(u, 2, axis=0)))
    cw = cw_ref[...]
    o_conv = cb_ref[0] * (u2 * cw[0:1, :] + u1 * cw[1:2, :] + u * cw[2:3, :])
    cst_ref[0] = u[tm - CONV_HALO:tm, :]
    merged = (jax.nn.sigmoid(ma_ref[0]) * _dot(og_ref[0].astype(BF16), wg_ref[...])
              + jax.nn.sigmoid(mb_ref[0]) * _dot(o_conv.astype(BF16), wc_ref[...])
              + jax.nn.sigmoid(mc_ref[0]) * _dot(on_ref[0].astype(BF16), wn_ref[...]))
    mix = _dot(merged.astype(BF16), wo_ref[...])
    x1 = _layer_norm_rows(DN_ALPHA * x_ref[0] + g1_ref[0] * mix, lng_ref[...], lnb_ref[...])
    x1_ref[0] = x1
    h2 = x1 * (1.0 + sc2_ref[0]) + sh2_ref[0]
    h2_ref[0] = h2.astype(h2_ref.dtype)
    s_ref[0] = jax.nn.sigmoid(_dot_hi(h2, wr_ref[...]))


def merge_pallas(x, y, o_gla, o_n, conv_prev, g1, sc2, sh2, lw):
    b, t, d = x.shape
    tm = math.gcd(t, MERGE_ROWS)
    halo = CONV_HALO
    hb = tm // halo
    prev = jnp.concatenate([jnp.zeros((b, halo - (CONV_W - 1), d), F32), conv_prev.astype(F32)], axis=1)
    col = lambda name, k=0: PROJ_OFF[name] // d + k
    tile = lambda cb_: pl.BlockSpec((1, tm, d), lambda i, j: (i, j, cb_))
    halo_spec = lambda cb_: pl.BlockSpec((1, halo, d), lambda i, j: (i, jnp.maximum(j * hb - 1, 0), cb_))
    vec = pl.BlockSpec((1, 1, d), lambda i, j: (i, 0, 0))
    const = lambda a: pl.BlockSpec(a.shape, lambda i, j: (0,) * a.ndim)
    wts = [lw['w_br_gla'].astype(BF16), lw['w_br_conv'].astype(BF16), lw['w_br_nsa'].astype(BF16), lw['w_o'].astype(BF16)]
    lng, lnb = lw['ln1_g'].reshape(1, d), lw['ln1_b'].reshape(1, d)
    x1, h2, s, cst = pl.pallas_call(
        _merge_kernel,
        grid=(b, t // tm),
        in_specs=[tile(0), tile(col('cb')), tile(col('cc')), tile(col('cx')), halo_spec(col('cc')), halo_spec(col('cx')),
                  pl.BlockSpec((1, halo, d), lambda i, j: (i, 0, 0)),
                  tile(col('mgt', 0)), tile(col('mgt', 1)), tile(col('mgt', 2)), tile(0), tile(0),
                  const(lw['conv_w'])] + [const(w) for w in wts] + [vec, vec, vec, const(lng), const(lnb),
                                                                    const(lw['w_router'])],
        out_specs=[tile(0), tile(0), pl.BlockSpec((1, tm, N_EXPERTS), lambda i, j: (i, j, 0)),
                   pl.BlockSpec((1, halo, d), lambda i, j: (i, 0, 0))],
        out_shape=[jax.ShapeDtypeStruct((b, t, d), F32), jax.ShapeDtypeStruct((b, t, d), BF16 if tm % 16 == 0 else F32),
                   jax.ShapeDtypeStruct((b, t, N_EXPERTS), F32), jax.ShapeDtypeStruct((b, halo, d), F32)],
        compiler_params=pltpu.CompilerParams(vmem_limit_bytes=VMEM_LIMIT),
        name="merge_ln1",
    )(x, y, y, y, y, y, prev, y, y, y, o_gla, o_n, lw['conv_w'], *wts,
      g1.reshape(b, 1, d), sc2.reshape(b, 1, d), sh2.reshape(b, 1, d), lng, lnb, lw['w_router'])
    return x1, h2, s, cst[:, halo - (CONV_W - 1):]


def route(s, router_bias):
    n = s.shape[0]
    sel = s + router_bias.astype(F32)
    per = N_EXPERTS // N_GROUPS
    gscore = lax.top_k(sel.reshape(n, N_GROUPS, per), 2)[0].sum(-1)
    gidx = lax.top_k(gscore, TOPK_GROUPS)[1]
    gmask = jax.nn.one_hot(gidx, N_GROUPS).sum(1) > 0
    sel = jnp.where(jnp.repeat(gmask, per, axis=1), sel, NEG_INF)
    idx = lax.top_k(sel, TOP_K)[1]
    w = jnp.take_along_axis(s, idx, axis=1)
    return idx, w / w.sum(-1, keepdims=True) * ROUTED_SCALE


def _expert_act(xb, wg, wu):
    hg = _dot(xb, wg)
    return hg * jax.nn.sigmoid(hg) * _dot(xb, wu)


def _grouped_expert_kernel(be_ref, x_ref, wg_ref, wu_ref, wd_ref, o_ref):
    act = _expert_act(x_ref[...], wg_ref[0], wu_ref[0])
    o_ref[...] = _dot(act.astype(BF16), wd_ref[0]).astype(o_ref.dtype)


def moe_routed_sorted(h2, idx, w, lw):
    n, d = h2.shape
    a = n * TOP_K
    fe = idx.reshape(-1)
    order = jnp.argsort(fe)
    se = fe[order]
    st = order // TOP_K
    counts = jnp.zeros((N_EXPERTS,), jnp.int32).at[fe].add(1)
    starts = jnp.cumsum(counts) - counts
    padded = (counts + MOE_BLOCK - 1) // MOE_BLOCK * MOE_BLOCK
    pend = jnp.cumsum(padded)
    pstart = pend - padded
    dest = pstart[se] + jnp.arange(a) - starts[se]
    nb = -(-a // MOE_BLOCK) + N_EXPERTS
    rows = nb * MOE_BLOCK
    row_tok = jnp.zeros((rows,), jnp.int32).at[dest].set(st)
    slot = jnp.zeros((a,), jnp.int32).at[order].set(dest).reshape(n, TOP_K)
    blk_e = jnp.minimum(jnp.searchsorted(pend, jnp.arange(nb) * MOE_BLOCK, side='right'), N_EXPERTS - 1).astype(jnp.int32)
    xs = h2.astype(BF16)[row_tok]
    wg, wu, wd = lw['w_exp_gate'].astype(BF16), lw['w_exp_up'].astype(BF16), lw['w_exp_down'].astype(BF16)
    hdim = wg.shape[2]
    y = pl.pallas_call(
        _grouped_expert_kernel,
        grid_spec=pltpu.PrefetchScalarGridSpec(
            num_scalar_prefetch=1,
            grid=(nb,),
            in_specs=[pl.BlockSpec((MOE_BLOCK, d), lambda i, be: (i, 0)),
                      pl.BlockSpec((1, d, hdim), lambda i, be: (be[i], 0, 0)),
                      pl.BlockSpec((1, d, hdim), lambda i, be: (be[i], 0, 0)),
                      pl.BlockSpec((1, hdim, d), lambda i, be: (be[i], 0, 0))],
            out_specs=pl.BlockSpec((MOE_BLOCK, d), lambda i, be: (i, 0))),
        out_shape=jax.ShapeDtypeStruct((rows, d), BF16),
        name="moe_grouped",
    )(blk_e, xs, wg, wu, wd)
    return jnp.einsum('nk,nkd->nd', w, y[slot].astype(F32))


def _dense_expert_kernel(x_ref, wt_ref, wg_ref, wu_ref, wd_ref, o_ref):
    @pl.when(pl.program_id(0) == 0)
    def _():
        o_ref[...] = jnp.zeros_like(o_ref)
    act = _expert_act(x_ref[...].astype(BF16), wg_ref[0].astype(BF16), wu_ref[0].astype(BF16))
    o_ref[...] += _dot((act * wt_ref[0]).astype(BF16), wd_ref[0].astype(BF16))


def moe_routed_dense(h2, idx, w, lw):
    n, d = h2.shape
    wdense = jnp.zeros((n, N_EXPERTS), F32).at[jnp.arange(n)[:, None], idx].add(w)
    wt = wdense.T.reshape(N_EXPERTS, n, 1)
    hdim = lw['w_exp_gate'].shape[2]
    return pl.pallas_call(
        _dense_expert_kernel,
        grid=(N_EXPERTS,),
        in_specs=[pl.BlockSpec((n, d), lambda e: (0, 0)),
                  pl.BlockSpec((1, n, 1), lambda e: (e, 0, 0)),
                  pl.BlockSpec((1, d, hdim), lambda e: (e, 0, 0)),
                  pl.BlockSpec((1, d, hdim), lambda e: (e, 0, 0)),
                  pl.BlockSpec((1, hdim, d), lambda e: (e, 0, 0))],
        out_specs=pl.BlockSpec((n, d), lambda e: (0, 0)),
        out_shape=jax.ShapeDtypeStruct((n, d), F32),
        name="moe_dense",
    )(h2, wt, lw['w_exp_gate'], lw['w_exp_up'], lw['w_exp_down'])


def _moe_tail_kernel(x1_ref, h2_ref, r_ref, g2_ref, wg_ref, wu_ref, wd_ref, lng_ref, lnb_ref, o_ref):
    act = _expert_act(h2_ref[0].astype(BF16), wg_ref[...], wu_ref[...])
    shared = _dot(act.astype(BF16), wd_ref[...])
    o_ref[0] = _layer_norm_rows(DN_ALPHA * x1_ref[0] + g2_ref[0] * (r_ref[0] + shared), lng_ref[...], lnb_ref[...])


def moe_tail_pallas(x1, h2, routed, g2, lw):
    b, t, d = x1.shape
    tm = math.gcd(t, MERGE_ROWS)
    tile = pl.BlockSpec((1, tm, d), lambda i, j: (i, j, 0))
    const = lambda a: pl.BlockSpec(a.shape, lambda i, j: (0,) * a.ndim)
    wg, wu, wd = lw['w_sh_gate'].astype(BF16), lw['w_sh_up'].astype(BF16), lw['w_sh_down'].astype(BF16)
    lng, lnb = lw['ln2_g'].reshape(1, d), lw['ln2_b'].reshape(1, d)
    return pl.pallas_call(
        _moe_tail_kernel,
        grid=(b, t // tm),
        in_specs=[tile, tile, tile, pl.BlockSpec((1, 1, d), lambda i, j: (i, 0, 0)),
                  const(wg), const(wu), const(wd), const(lng), const(lnb)],
        out_specs=tile,
        out_shape=jax.ShapeDtypeStruct((b, t, d), F32),
        name="moe_tail_ln2",
    )(x1, h2, routed.reshape(b, t, d), g2.reshape(b, 1, d), wg, wu, wd, lng, lnb)


def layer_block(x, c, lw, gla_s0, conv_prev, nsa_fn, sorted_moe):
    b, t, d = x.shape
    mod = jax.nn.silu(c) @ lw['w_ada'] + lw['b_ada']
    sh1, sc1, g1, sh2, sc2, g2 = jnp.split(mod, 6, axis=-1)
    y = in_proj(x, sc1, sh1, lw['w_in'])
    o_gla, s_fin = gla_pallas(y, lw, gla_s0)

    def rows(name):
        return proj_cols(y, name).reshape(b, t, NSA_KV_HEADS, HEAD_DIM).transpose(0, 2, 1, 3)
    kv_rows = [rows(nm) for nm in ('nkc', 'nvc', 'nks', 'nvs', 'nkw', 'nvw')]
    o_n, win_k, win_v = nsa_fn(y, *kv_rows)
    x1, h2, s, conv_state = merge_pallas(x, y, o_gla, o_n, conv_prev, g1, sc2, sh2, lw)
    idx, w = route(s.reshape(b * t, N_EXPERTS), lw['router_bias'])
    routed = (moe_routed_sorted if sorted_moe else moe_routed_dense)(h2.reshape(b * t, d), idx, w, lw)
    x2 = moe_tail_pallas(x1, h2, routed, g2, lw)
    return x2, (*kv_rows[:4], win_k, win_v, s_fin, conv_state)


def kernel(x_prompt, x_sample, cache_cmp_k, cache_cmp_v, cache_slc_k, cache_slc_v, cache_win_k, cache_win_v,
           state_gla, state_conv, page_table, c_prompt, c_sample, w_ada, b_ada, w_in, gla_wa2, gla_ba,
           gla_norm_g, conv_w, cmp_pe_k, cmp_pe_v, cmp_w1_k, cmp_w2_k, cmp_w1_v, cmp_w2_v, rel_bias,
           w_br_gla, w_br_conv, w_br_nsa, w_o, ln1_g, ln1_b, ln2_g, ln2_b, w_router, router_bias,
           w_exp_gate, w_exp_up, w_exp_down, w_sh_gate, w_sh_up, w_sh_down):
    params = dict(w_ada=w_ada, b_ada=b_ada, w_in=w_in, gla_wa2=gla_wa2, gla_ba=gla_ba, gla_norm_g=gla_norm_g,
                  conv_w=conv_w, cmp_pe_k=cmp_pe_k, cmp_pe_v=cmp_pe_v, cmp_w1_k=cmp_w1_k, cmp_w2_k=cmp_w2_k,
                  cmp_w1_v=cmp_w1_v, cmp_w2_v=cmp_w2_v, w_br_gla=w_br_gla, w_br_conv=w_br_conv, w_br_nsa=w_br_nsa,
                  w_o=w_o, ln1_g=ln1_g, ln1_b=ln1_b, ln2_g=ln2_g, ln2_b=ln2_b, w_router=w_router,
                  router_bias=router_bias, w_exp_gate=w_exp_gate, w_exp_up=w_exp_up, w_exp_down=w_exp_down,
                  w_sh_gate=w_sh_gate, w_sh_up=w_sh_up, w_sh_down=w_sh_down)
    tbl = rel_bias.reshape(REL_BUCKETS, NSA_KV_HEADS, NSA_GROUP)
    bp, tp, _ = x_prompt.shape
    bs, ts, _ = x_sample.shape
    bias_tables = nsa_bias_tables(tbl, tp, tp // CMP_STRIDE)
    xp, xs = x_prompt, x_sample
    p_states, s_states = [], []
    for l in range(DEPTH):
        lw = {k: v[l] for k, v in params.items()}

        def prompt_nsa(y, kc_r, vc_r, ks_r, vs_r, kw_r, vw_r):
            o_n = nsa_prompt_pallas(y, PROJ_OFF['nq'], proj_cols(y, 'ngt'), kc_r, vc_r, ks_r, vs_r, kw_r, vw_r,
                                    lw, bias_tables)
            wb = min(WINDOW, tp)
            return o_n, kw_r[:, :, tp - wb:], vw_r[:, :, tp - wb:]

        def sample_nsa(y, kc_r, vc_r, ks_r, vs_r, kw_r, vw_r):
            q_n = proj_cols(y, 'nq').reshape(bs, ts, NSA_KV_HEADS, NSA_GROUP, HEAD_DIM)
            g_n = jax.nn.sigmoid(proj_cols(y, 'ngt')).reshape(bs, ts, NSA_KV_HEADS, NSA_GROUP, 3)
            o_n, win_k, win_v = nsa_sample(q_n, g_n, kc_r, vc_r, ks_r, vs_r, kw_r, vw_r, lw, tbl, l, cache_cmp_k,
                                           cache_cmp_v, cache_slc_k, cache_slc_v, cache_win_k[l], cache_win_v[l],
                                           page_table)
            return o_n.reshape(bs, ts, NSA_Q), win_k, win_v

        xp, st = layer_block(xp, c_prompt, lw, jnp.zeros((bp, GLA_HEADS, GLA_DK, GLA_DV), F32),
                             jnp.zeros((bp, CONV_W - 1, CONV_DIM), F32), prompt_nsa, True)
        p_states.append(st)
        xs, st = layer_block(xs, c_sample, lw, state_gla[l], state_conv[l], sample_nsa, False)
        s_states.append(st)
    p_out = [jnp.stack(z) for z in zip(*p_states)]
    s_out = [jnp.stack(z) for z in zip(*s_states)]
    return (xp, xs, *p_out, *s_out)
```

```python
import math, functools
import jax, jax.numpy as jnp
from jax import lax
import numpy as np
from jax.experimental import pallas as pl
from jax.experimental.pallas import tpu as pltpu

D_MODEL = 1024
DEPTH = 2
PAGE_SIZE = 128
GLA_HEADS = 4
GLA_DK = 128
GLA_DV = 256
GLA_RANK = 16
GLA_TAU = 16.0
GLA_CHUNK = 64
CONV_DIM = D_MODEL
CONV_W = 3
NSA_HEADS = 16
NSA_KV_HEADS = 4
NSA_GROUP = NSA_HEADS // NSA_KV_HEADS
HEAD_DIM = 64
CMP_BLOCK = 32
CMP_STRIDE = 16
CMP_HIDDEN = 128
SLC_BLOCK = 64
SLC_TOPK = 16
WINDOW = 512
NSA_QCHUNK = 16
REL_BUCKETS = 32
REL_MAX_DIST = 128
N_EXPERTS = 64
TOP_K = 8
N_GROUPS = 8
TOPK_GROUPS = 4
EXPERT_HIDDEN = 256
SHARED_HIDDEN = 256
ROUTED_SCALE = 2.5
MOE_BLOCK = 128
DN_ALPHA = (2 * DEPTH) ** 0.25
LN_EPS = 1e-5
NEG_INF = -1e30

GLA_QK = GLA_HEADS * GLA_DK
GLA_V = GLA_HEADS * GLA_DV
NSA_Q = NSA_HEADS * HEAD_DIM
NSA_KV = NSA_KV_HEADS * HEAD_DIM
IN_SIZES = (GLA_QK, GLA_QK, GLA_V, GLA_V, GLA_RANK, CONV_DIM, CONV_DIM, CONV_DIM, NSA_Q, NSA_KV, NSA_KV, NSA_KV, NSA_KV, NSA_KV, NSA_KV, 3 * NSA_HEADS, 3 * D_MODEL)
IN_TOTAL = sum(IN_SIZES)


BF16 = jnp.bfloat16
F32 = jnp.float32
LANES = 128
NSA_TILE = 128
GLA_ROWS = 256
VMEM_LIMIT = 48 * 1024 * 1024

PROJ_NAMES = ('gq', 'gk', 'gv', 'gg', 'ga', 'cb', 'cc', 'cx', 'nq', 'nkc', 'nvc', 'nks', 'nvs', 'nkw', 'nvw', 'ngt', 'mgt')
PROJ_ORDER = ('gv', 'gg', 'cb', 'cc', 'cx', 'nq', 'mgt', 'gq', 'gk', 'nkc', 'nvc', 'nks', 'nvs', 'nkw', 'nvw', 'ga', 'ngt')
PROJ_SIZE = dict(zip(PROJ_NAMES, IN_SIZES))
PROJ_SRC = dict(zip(PROJ_NAMES, np.cumsum((0,) + IN_SIZES[:-1]).tolist()))
PROJ_OFF = dict(zip(PROJ_ORDER, np.cumsum([0] + [PROJ_SIZE[n] for n in PROJ_ORDER[:-1]]).tolist()))
PROJ_TN = 512
PROJ_COLS = -(-IN_TOTAL // PROJ_TN) * PROJ_TN


def _dot_nt(a, b):
    return lax.dot_general(a, b, (((1,), (1,)), ((), ())), preferred_element_type=F32)


def _dot(a, b):
    return jnp.dot(a, b, preferred_element_type=F32)


def _dot_hi(a, b):
    return jnp.dot(a, b, precision=lax.Precision.HIGHEST, preferred_element_type=F32)


def _in_proj_kernel(x_ref, sc_ref, sh_ref, w_ref, o_ref, h_ref):
    @pl.when(pl.program_id(2) == 0)
    def _():
        h_ref[...] = (x_ref[0] * (1.0 + sc_ref[0]) + sh_ref[0]).astype(BF16)
    o_ref[0] = _dot(h_ref[...], w_ref[...])


def in_proj(x, sc, sh, w_in):
    b, t, d = x.shape
    w = jnp.concatenate([w_in[:, PROJ_SRC[n]:PROJ_SRC[n] + PROJ_SIZE[n]] for n in PROJ_ORDER]
                        + [jnp.zeros((d, PROJ_COLS - IN_TOTAL), w_in.dtype)], axis=1).astype(BF16)
    tm = math.gcd(t, 1024)
    return pl.pallas_call(
        _in_proj_kernel,
        grid=(b, t // tm, PROJ_COLS // PROJ_TN),
        in_specs=[pl.BlockSpec((1, tm, d), lambda i, j, k: (i, j, 0)),
                  pl.BlockSpec((1, 1, d), lambda i, j, k: (i, 0, 0)),
                  pl.BlockSpec((1, 1, d), lambda i, j, k: (i, 0, 0)),
                  pl.BlockSpec((d, PROJ_TN), lambda i, j, k: (0, k))],
        out_specs=pl.BlockSpec((1, tm, PROJ_TN), lambda i, j, k: (i, j, k)),
        out_shape=jax.ShapeDtypeStruct((b, t, PROJ_COLS), F32),
        scratch_shapes=[pltpu.VMEM((tm, d), BF16)],
        compiler_params=pltpu.CompilerParams(vmem_limit_bytes=VMEM_LIMIT),
        name="in_proj",
    )(x, sc.reshape(b, 1, d), sh.reshape(b, 1, d), w)


def proj_cols(y, name):
    return y[..., PROJ_OFF[name]:PROJ_OFF[name] + PROJ_SIZE[name]]


def _gla_kernel(q_ref, k_ref, v_ref, gg_ref, sm_ref, wa2_ref, ba_ref, ng_ref, s0_ref, o_ref, sfin_ref, st_ref,
                *, chunk, nchunks):
    tb = pl.program_id(2)
    c = chunk

    @pl.when(tb == 0)
    def _():
        st_ref[...] = s0_ref[0, 0].T

    row = lax.broadcasted_iota(jnp.int32, (c, c), 0)
    col = lax.broadcasted_iota(jnp.int32, (c, c), 1)
    causal = row >= col
    tri = jnp.where(causal, 1.0, 0.0)
    ga_off = PROJ_OFF['ga'] % LANES
    for ci in range(nchunks):
        sl = slice(ci * c, (ci + 1) * c)
        q = q_ref[0, sl, :] * (GLA_DK ** -0.5)
        k = k_ref[0, sl, :]
        v = v_ref[0, sl, :].astype(BF16)
        ga = sm_ref[0, sl, ga_off:ga_off + GLA_RANK]
        x = _dot_hi(ga, wa2_ref[...]) + ba_ref[...]
        log_a = (jnp.minimum(x, 0.0) - jnp.log1p(jnp.exp(-jnp.abs(x)))) * (1.0 / GLA_TAU)
        cum = _dot_hi(tri, log_a)
        last = cum[c - 1:c, :]
        mid = cum[c // 2 - 1:c // 2, :]
        st = st_ref[...]
        inter = _dot_nt((q * jnp.exp(cum)).astype(BF16), st.astype(BF16))
        att = _dot_nt((q * jnp.exp(cum - mid)).astype(BF16), (k * jnp.exp(mid - cum)).astype(BF16))
        att = jnp.where(causal, att, 0.0)
        o = inter + _dot(att.astype(BF16), v)
        kd = (k * jnp.exp(last - cum)).astype(BF16)
        st_ref[...] = st * jnp.exp(last) + lax.dot_general(v, kd, (((0,), (0,)), ((), ())),
                                                           preferred_element_type=F32)
        mu = jnp.mean(o, axis=-1, keepdims=True)
        var = jnp.mean(jnp.square(o - mu), axis=-1, keepdims=True)
        gate = gg_ref[0, sl, :]
        o_ref[0, sl, :] = (o - mu) * lax.rsqrt(var + LN_EPS) * ng_ref[...] * (gate * jax.nn.sigmoid(gate))

    @pl.when(tb == pl.num_programs(2) - 1)
    def _():
        sfin_ref[0, 0] = st_ref[...].T


def gla_pallas(y, lw, s0):
    b, t, _ = y.shape
    c = math.gcd(t, GLA_CHUNK)
    r = math.gcd(t, GLA_ROWS)
    qb, kb = PROJ_OFF['gq'] // GLA_DK, PROJ_OFF['gk'] // GLA_DK
    vb, gb = PROJ_OFF['gv'] // GLA_DV, PROJ_OFF['gg'] // GLA_DV
    sb = PROJ_OFF['ga'] // LANES
    return pl.pallas_call(
        functools.partial(_gla_kernel, chunk=c, nchunks=r // c),
        grid=(b, GLA_HEADS, t // r),
        in_specs=[pl.BlockSpec((1, r, GLA_DK), lambda i, h, j: (i, j, qb + h)),
                  pl.BlockSpec((1, r, GLA_DK), lambda i, h, j: (i, j, kb + h)),
                  pl.BlockSpec((1, r, GLA_DV), lambda i, h, j: (i, j, vb + h)),
                  pl.BlockSpec((1, r, GLA_DV), lambda i, h, j: (i, j, gb + h)),
                  pl.BlockSpec((1, r, LANES), lambda i, h, j: (i, j, sb)),
                  pl.BlockSpec((GLA_RANK, GLA_DK), lambda i, h, j: (0, h)),
                  pl.BlockSpec((1, GLA_DK), lambda i, h, j: (0, h)),
                  pl.BlockSpec((1, GLA_DV), lambda i, h, j: (0, h)),
                  pl.BlockSpec((1, 1, GLA_DK, GLA_DV), lambda i, h, j: (i, h, 0, 0))],
        out_specs=[pl.BlockSpec((1, r, GLA_DV), lambda i, h, j: (i, j, h)),
                   pl.BlockSpec((1, 1, GLA_DK, GLA_DV), lambda i, h, j: (i, h, 0, 0))],
        out_shape=[jax.ShapeDtypeStruct((b, t, GLA_V), F32),
                   jax.ShapeDtypeStruct((b, GLA_HEADS, GLA_DK, GLA_DV), F32)],
        scratch_shapes=[pltpu.VMEM((GLA_DV, GLA_DK), F32)],
        name="gla_scan",
    )(y, y, y, y, y, lw['gla_wa2'], lw['gla_ba'].reshape(1, -1), lw['gla_norm_g'].reshape(1, -1), s0.astype(F32))


def _split3(x):
    hi = x.astype(BF16)
    r = x - hi.astype(F32)
    mid = r.astype(BF16)
    lo = (r - mid.astype(F32)).astype(BF16)
    return hi, mid, lo


def _gelu_tanh(x):
    return 0.5 * x * (1.0 + jnp.tanh(math.sqrt(2.0 / math.pi) * (x + 0.044715 * (x * x * x))))


def _compress_kernel(chk_ref, chv_ref, pek_ref, pev_ref, w1k_ref, w1v_ref, w2k_ref, w2v_ref, ok_ref, ov_ref):
    rows = CMP_STRIDE * HEAD_DIM
    for ch_ref, pe_ref, w1_ref, w2_ref, o_ref in ((chk_ref, pek_ref, w1k_ref, w2k_ref, ok_ref),
                                                   (chv_ref, pev_ref, w1v_ref, w2v_ref, ov_ref)):
        ch = ch_ref[0, 0].astype(BF16)
        nch = ch.shape[0]
        w1 = w1_ref[...]
        pe = jnp.broadcast_to(pe_ref[...], (8, pe_ref.shape[1])).astype(BF16)
        hid = _dot(pe, w1)[0:1, :]
        a0 = _dot(ch, w1[0:rows, :])
        a1 = _dot(ch, w1[rows:2 * rows, :])
        hid = hid + a0 + pltpu.roll(a1, nch - 1, axis=0)
        o_ref[0, 0] = _dot(_gelu_tanh(hid).astype(BF16), w2_ref[...])


def compress_pair(kc_r, vc_r, lw):
    b, kvh, length, hd = kc_r.shape
    nch = length // CMP_STRIDE
    chk = kc_r.reshape(b, kvh, nch, CMP_STRIDE * hd)
    chv = vc_r.reshape(b, kvh, nch, CMP_STRIDE * hd)
    ch_spec = pl.BlockSpec((1, 1, nch, CMP_STRIDE * hd), lambda i, j: (i, j, 0, 0))
    full = lambda a: pl.BlockSpec(a.shape, lambda i, j: (0,) * a.ndim)
    pek = lw['cmp_pe_k'].reshape(1, -1)
    pev = lw['cmp_pe_v'].reshape(1, -1)
    w1k, w1v = lw['cmp_w1_k'].astype(BF16), lw['cmp_w1_v'].astype(BF16)
    w2k, w2v = lw['cmp_w2_k'].astype(BF16), lw['cmp_w2_v'].astype(BF16)
    o_spec = pl.BlockSpec((1, 1, nch, hd), lambda i, j: (i, j, 0, 0))
    o_shape = jax.ShapeDtypeStruct((b, kvh, nch, hd), F32)
    return pl.pallas_call(
        _compress_kernel,
        grid=(b, kvh),
        in_specs=[ch_spec, ch_spec, full(pek), full(pev), full(w1k), full(w1v), full(w2k), full(w2v)],
        out_specs=[o_spec, o_spec],
        out_shape=[o_shape, o_shape],
        name="nsa_compress",
    )(chk, chv, pek, pev, w1k, w1v, w2k, w2v)


def _dot_tn(a, b):
    return lax.dot_general(a, b, (((0,), (0,)), ((), ())), preferred_element_type=F32)


def _softmax_step(m, l, acc_ref, s, v):
    m_new = jnp.maximum(m, jnp.max(s, axis=0, keepdims=True))
    alpha = jnp.exp(m - m_new)
    p = jnp.exp(s - m_new)
    l = alpha * l + jnp.sum(p, axis=0, keepdims=True)
    acc_ref[...] = alpha * acc_ref[...] + _dot_tn(v, p.astype(BF16))
    return m_new, l


def _nsa_t_kernel(q_ref, g_ref, kc_ref, vc_ref, ks_ref, vs_ref, kw_ref, vw_ref, bc_ref, bt_ref, ov_ref, o_ref,
                  acc_ref, *, nsb):
    tq = NSA_TILE
    grp = NSA_GROUP
    hd = HEAD_DIM
    wq = grp * tq
    ncp = kc_ref.shape[2]
    qi = pl.program_id(2)
    q0 = qi * tq
    q = q_ref[0] * (hd ** -0.5)
    q4 = jnp.concatenate([q[:, g * hd:(g + 1) * hd] for g in range(grp)], axis=0).astype(BF16)
    lane_q = lax.broadcasted_iota(jnp.int32, (1, tq), 1)
    qpos1 = q0 + lane_q
    qpos = jnp.concatenate([qpos1] * grp, axis=1)

    kc = kc_ref[0, 0].astype(BF16)
    vc = vc_ref[0, 0].astype(BF16)
    cend = CMP_STRIDE * lax.broadcasted_iota(jnp.int32, (ncp, 1), 0) + (CMP_BLOCK - 1)
    mask_c = cend <= qpos
    lc = jnp.where(mask_c, _dot_nt(kc, q4) + bc_ref[0, 0], NEG_INF)
    mc = jnp.max(lc, axis=0, keepdims=True)
    pc = jnp.where(mask_c, jnp.exp(lc - mc), 0.0)
    lsum = jnp.sum(pc, axis=0, keepdims=True)
    pc = pc / jnp.where(lsum > 0.0, lsum, 1.0)
    o_c = _dot_tn(vc, pc.astype(BF16))
    psum = pc[:, 0:tq]
    for g in range(1, grp):
        psum = psum + pc[:, g * tq:(g + 1) * tq]

    ov = ov_ref[...]
    hi, mid, lo = _split3(psum)
    imp = (_dot(ov, hi) + _dot(ov, mid) + _dot(ov, lo))[0:nsb, :]
    blk = lax.broadcasted_iota(jnp.int32, (nsb, 1), 0)
    cur = lax.shift_right_logical(qpos1, 6)
    forced = (blk == 0) | (blk == cur) | (blk == cur - 1)
    visible = blk * SLC_BLOCK <= qpos1
    score = jnp.where(forced, -NEG_INF, imp)
    score = jnp.where(visible, score, NEG_INF)
    rank = jnp.zeros((nsb, tq), jnp.int32)
    for i in range(nsb):
        si = score[i:i + 1, :]
        beats = (si > score) | ((si == score) & (blk > i))
        rank = rank + beats.astype(jnp.int32)
    sel1 = jnp.where(visible & (rank < SLC_TOPK), 1.0, 0.0).astype(BF16)
    sel1 = jnp.concatenate([sel1, jnp.zeros((LANES - nsb, tq), BF16)], axis=0)
    sel = jnp.concatenate([sel1] * grp, axis=1)

    neg = jnp.full((1, wq), NEG_INF, F32)
    zero = jnp.zeros((1, wq), F32)
    krow = lax.broadcasted_iota(jnp.int32, (tq, 1), 0)
    lane4 = jnp.concatenate([lane_q] * grp, axis=1)

    def slc_tile(kt, causal, m, l):
        k0 = pl.multiple_of(kt * tq, tq)
        k_t = ks_ref[0, 0, pl.ds(k0, tq), :].astype(BF16)
        v_t = vs_ref[0, 0, pl.ds(k0, tq), :].astype(BF16)
        kblk = lax.shift_right_logical(k0 + krow, 6)
        expand = jnp.where(kblk == lax.broadcasted_iota(jnp.int32, (1, LANES), 1), 1.0, 0.0).astype(BF16)
        mask = _dot(expand, sel) > 0.5
        if causal:
            mask = mask & (krow <= lane4)
        s = jnp.where(mask, _dot_nt(k_t, q4) + bt_ref[0, jnp.minimum(qi - kt, 2)], NEG_INF)
        return _softmax_step(m, l, acc_ref, s, v_t)

    acc_ref[...] = jnp.zeros_like(acc_ref)
    m_s, l_s = lax.fori_loop(0, qi, lambda kt, c: slc_tile(kt, False, *c), (neg, zero))
    m_s, l_s = slc_tile(qi, True, m_s, l_s)
    o_s = acc_ref[...] / l_s

    acc_ref[...] = jnp.zeros_like(acc_ref)
    nwt = WINDOW // tq
    m_w, l_w = neg, zero
    for dt in range(nwt, -1, -1):
        kt = qi - dt
        k0 = pl.multiple_of(jnp.maximum(kt, 0) * tq, tq)
        k_t = kw_ref[0, 0, pl.ds(k0, tq), :].astype(BF16)
        v_t = vw_ref[0, 0, pl.ds(k0, tq), :].astype(BF16)
        s = _dot_nt(k_t, q4) + bt_ref[0, min(dt, 2)]
        if dt == nwt:
            s = jnp.where((krow > lane4) & (kt >= 0), s, NEG_INF)
        elif dt == 0:
            s = jnp.where(krow <= lane4, s, NEG_INF)
        else:
            s = jnp.where(kt >= 0, s, NEG_INF)
        m_w, l_w = _softmax_step(m_w, l_w, acc_ref, s, v_t)
    o_w = acc_ref[...] / l_w

    gates = jax.nn.sigmoid(g_ref[0, 0])
    outs = []
    for g in range(grp):
        sl = slice(g * tq, (g + 1) * tq)
        outs.append(gates[:, 3 * g:3 * g + 1] * o_c[:, sl].T + gates[:, 3 * g + 1:3 * g + 2] * o_s[:, sl].T
                    + gates[:, 3 * g + 2:3 * g + 3] * o_w[:, sl].T)
    o_ref[0] = jnp.concatenate(outs, axis=-1)


def nsa_bias_tables(tbl, t, ncp):
    tq = NSA_TILE
    kvh, grp = tbl.shape[1], tbl.shape[2]
    dist_c = jnp.arange(t)[None, :] - (CMP_STRIDE * jnp.arange(ncp) + CMP_BLOCK - 1)[:, None]
    bc = tbl[rel_bucket(dist_c)]
    bc = bc.reshape(ncp, t // tq, tq, kvh, grp).transpose(3, 1, 0, 4, 2).reshape(kvh, t // tq, ncp, grp * tq)
    cr = jnp.arange(tq)[None, :] - jnp.arange(tq)[:, None]
    tiles = jnp.stack([cr, cr + tq, jnp.full_like(cr, REL_MAX_DIST)])
    bt = tbl[rel_bucket(tiles)]
    bt = bt.transpose(3, 0, 1, 4, 2).reshape(kvh, 3, tq, grp * tq)
    return bc, bt


def nsa_prompt_pallas(y, q_off, ngt, kc_r, vc_r, ks_r, vs_r, kw_r, vw_r, lw, bias_tables):
    b, t, _ = y.shape
    kvh, grp, hd, tq = NSA_KV_HEADS, NSA_GROUP, HEAD_DIM, NSA_TILE
    qb = q_off // (grp * hd)
    kc, vc = compress_pair(kc_r, vc_r, lw)
    ncp = kc.shape[2]
    nsb = t // SLC_BLOCK
    bc, bt = bias_tables
    gates = ngt.reshape(b, t, kvh, grp * 3).transpose(0, 2, 1, 3)
    ci = jnp.arange(ncp)[None, :] * CMP_STRIDE
    sj = jnp.arange(LANES)[:, None] * SLC_BLOCK
    ov = ((ci < sj + SLC_BLOCK) & (ci + CMP_BLOCK > sj) & (jnp.arange(ncp)[None, :] < ncp - 1)
          & (jnp.arange(LANES)[:, None] < nsb)).astype(BF16)
    row_spec = pl.BlockSpec((1, 1, t, hd), lambda i, j, k: (i, j, 0, 0))
    cmp_spec = pl.BlockSpec((1, 1, ncp, hd), lambda i, j, k: (i, j, 0, 0))
    return pl.pallas_call(
        functools.partial(_nsa_t_kernel, nsb=nsb),
        grid=(b, kvh, t // tq),
        in_specs=[pl.BlockSpec((1, tq, grp * hd), lambda i, j, k: (i, k, qb + j)),
                  pl.BlockSpec((1, 1, tq, grp * 3), lambda i, j, k: (i, j, k, 0)),
                  cmp_spec, cmp_spec, row_spec, row_spec, row_spec, row_spec,
                  pl.BlockSpec((1, 1, ncp, grp * tq), lambda i, j, k: (j, k, 0, 0)),
                  pl.BlockSpec((1, 3, tq, grp * tq), lambda i, j, k: (j, 0, 0, 0)),
                  pl.BlockSpec(ov.shape, lambda i, j, k: (0, 0))],
        out_specs=pl.BlockSpec((1, tq, grp * hd), lambda i, j, k: (i, k, j)),
        out_shape=jax.ShapeDtypeStruct((b, t, kvh * grp * hd), F32),
        scratch_shapes=[pltpu.VMEM((hd, grp * tq), F32)],
        compiler_params=pltpu.CompilerParams(vmem_limit_bytes=VMEM_LIMIT),
        name="nsa_prompt",
    )(y, gates, kc, vc, ks_r, vs_r, kw_r, vw_r, bc, bt, ov)


def layer_norm(x, g, b):
    xf = x.astype(jnp.float32)
    mu = xf.mean(-1, keepdims=True)
    var = jnp.square(xf - mu).mean(-1, keepdims=True)
    return ((xf - mu) * lax.rsqrt(var + LN_EPS) * g + b).astype(x.dtype)


def masked_softmax(logits, mask):
    p = jax.nn.softmax(jnp.where(mask, logits.astype(jnp.float32), NEG_INF), axis=-1)
    return jnp.where(mask, p, 0.0)


def rel_bucket(dist):
    n = jnp.maximum(dist, 0)
    exact = REL_BUCKETS // 2
    big = exact + (jnp.log(jnp.maximum(n, 1).astype(jnp.float32) / exact)
                   / math.log(REL_MAX_DIST / exact) * (REL_BUCKETS - exact)).astype(jnp.int32)
    return jnp.where(n < exact, n, jnp.minimum(big, REL_BUCKETS - 1))


def gla_scan(q, k, v, log_a, s0):
    b, t = q.shape[:2]
    c = math.gcd(t, GLA_CHUNK)
    nc = t // c
    causal = jnp.tril(jnp.ones((c, c), dtype=bool))

    def to_chunks(a):
        return a.reshape(b, nc, c, *a.shape[2:]).swapaxes(0, 1)

    def step(s, inp):
        qc, kc, vc, ac = inp
        cum = jnp.cumsum(ac, axis=1)
        inter = jnp.einsum('bthk,bhkv->bthv', qc * jnp.exp(cum), s)
        diff = cum[:, :, None] - cum[:, None, :]
        decay = jnp.exp(jnp.where(causal[None, :, :, None, None], diff, -jnp.inf))
        att = jnp.einsum('bthk,bshk,btshk->bhts', qc, kc, decay)
        intra = jnp.einsum('bhts,bshv->bthv', att, vc)
        last = cum[:, -1]
        s = jnp.exp(last)[..., None] * s + jnp.einsum('bshk,bshv->bhkv', kc * jnp.exp(last[:, None] - cum), vc)
        return s, inter + intra

    s_fin, o = lax.scan(step, s0, (to_chunks(q), to_chunks(k), to_chunks(v), to_chunks(log_a)))
    return o.swapaxes(0, 1).reshape(b, t, *v.shape[2:]), s_fin


def gla_output(o, gate, norm_g):
    mu = o.mean(-1, keepdims=True)
    var = jnp.square(o - mu).mean(-1, keepdims=True)
    on = ((o - mu) * lax.rsqrt(var + LN_EPS)).reshape(*o.shape[:2], -1)
    return on * norm_g * jax.nn.silu(gate.astype(jnp.float32))


def short_conv(b_gate, c_gate, x_in, prev, w):
    u = c_gate * x_in
    up = jnp.concatenate([prev, u], axis=1)
    t = u.shape[1]
    y = up[:, 0:t] * w[0]
    for j in range(1, CONV_W):
        y = y + up[:, j:j + t] * w[j]
    return b_gate * y, up[:, t:]


def compress(k, pe, w1, w2):
    b, kvh, length, hd = k.shape
    n_chunks = length // CMP_STRIDE
    pieces = CMP_BLOCK // CMP_STRIDE
    n_blocks = n_chunks - pieces + 1
    ch = k.reshape(b, kvh, n_chunks, CMP_STRIDE * hd)
    rows = CMP_STRIDE * hd
    hid = pe.reshape(-1) @ w1
    for r in range(pieces):
        hid = hid + (ch @ w1[r * rows:(r + 1) * rows])[:, :, r:r + n_blocks]
    return jax.nn.gelu(hid) @ w2


def slc_overlap(n_cmp, n_slc):
    start = jnp.arange(n_cmp) * CMP_STRIDE
    blk = jnp.arange(n_slc) * SLC_BLOCK
    return ((start[:, None] < blk[None, :] + SLC_BLOCK) & (start[:, None] + CMP_BLOCK > blk[None, :])).astype(jnp.float32)


def nsa_attend(q, qpos, gates, kc, vc, overlap, gather_slc, kw, vw, wpos, tbl):
    b, nq, kvh, grp, hd = q.shape
    scale = hd ** -0.5
    f32 = jnp.float32
    hi = jnp.arange(kvh)[:, None, None]
    cmp_end = CMP_STRIDE * jnp.arange(kc.shape[2]) + CMP_BLOCK - 1
    dist_c = qpos[:, None] - cmp_end[None, :]
    lc = jnp.einsum('bqhgd,bhnd->bqhgn', q, kc).astype(f32) * scale + jnp.moveaxis(tbl[rel_bucket(dist_c)], 1, -1)
    pc = masked_softmax(lc, (dist_c >= 0)[:, None, None, :])
    o_c = jnp.einsum('bqhgn,bhnd->bqhgd', pc, vc)
    imp = jnp.einsum('bqhn,ns->bqhs', pc.sum(3), overlap)
    blk = jnp.arange(overlap.shape[1])
    cur = (qpos // SLC_BLOCK)[:, None]
    forced = (blk[None] == 0) | (blk[None] == cur) | (blk[None] == cur - 1)
    visible = blk[None] * SLC_BLOCK <= qpos[:, None]
    score = jnp.where(forced[:, None], -NEG_INF, imp)
    score = jnp.where(visible[:, None], score, NEG_INF)
    top_s, idx = lax.top_k(score, min(SLC_TOPK, overlap.shape[1]))
    ks, vs = gather_slc(idx)
    kpos = idx[..., None] * SLC_BLOCK + jnp.arange(SLC_BLOCK)
    dist_s = qpos[None, :, None, None, None] - kpos
    mask_s = (dist_s >= 0) & (top_s > 0.5 * NEG_INF)[..., None]
    ls = jnp.einsum('bqhgd,bqhnsd->bqhgns', q, ks).astype(f32) * scale + jnp.moveaxis(tbl[rel_bucket(dist_s), hi], -1, 3)
    ps = masked_softmax(ls.reshape(b, nq, kvh, grp, -1), mask_s.reshape(b, nq, kvh, 1, -1))
    o_s = jnp.einsum('bqhgm,bqhmd->bqhgd', ps, vs.reshape(b, nq, kvh, -1, hd))
    dist_w = qpos[:, None] - wpos[None, :]
    mask_w = (dist_w >= 0) & (dist_w < WINDOW) & (wpos >= 0)[None, :]
    lwin = jnp.einsum('bqhgd,bhwd->bqhgw', q, kw).astype(f32) * scale + jnp.moveaxis(tbl[rel_bucket(dist_w)], 1, -1)
    pw = masked_softmax(lwin, mask_w[:, None, None, :])
    o_w = jnp.einsum('bqhgw,bhwd->bqhgd', pw, vw)
    o = gates[..., 0:1] * o_c + gates[..., 1:2] * o_s + gates[..., 2:3] * o_w
    return o.astype(q.dtype)


def nsa_prompt(q, gates, kc_r, vc_r, ks_r, vs_r, kw_r, vw_r, lw, tbl):
    b, s = q.shape[:2]
    kc = compress(kc_r, lw['cmp_pe_k'], lw['cmp_w1_k'], lw['cmp_w2_k'])
    vc = compress(vc_r, lw['cmp_pe_v'], lw['cmp_w1_v'], lw['cmp_w2_v'])
    nsb = s // SLC_BLOCK
    ks_blk = ks_r.reshape(b, NSA_KV_HEADS, nsb, SLC_BLOCK, HEAD_DIM)
    vs_blk = vs_r.reshape(b, NSA_KV_HEADS, nsb, SLC_BLOCK, HEAD_DIM)
    bi = jnp.arange(b)[:, None, None, None]
    hi = jnp.arange(NSA_KV_HEADS)[None, None, :, None]

    def gather_slc(idx):
        return ks_blk[bi, hi, idx], vs_blk[bi, hi, idx]

    overlap = slc_overlap(kc.shape[2], nsb)
    kw_pad = jnp.pad(kw_r, ((0, 0), (0, 0), (WINDOW, 0), (0, 0)))
    vw_pad = jnp.pad(vw_r, ((0, 0), (0, 0), (WINDOW, 0), (0, 0)))
    n_chunks = s // NSA_QCHUNK
    span = WINDOW + NSA_QCHUNK

    def chunks(a):
        return a.reshape(b, n_chunks, NSA_QCHUNK, *a.shape[2:]).swapaxes(0, 1)

    def body(inp):
        qc, gc, c0 = inp
        return nsa_attend(qc, c0 + jnp.arange(NSA_QCHUNK), gc, kc, vc, overlap, gather_slc,
                          lax.dynamic_slice_in_dim(kw_pad, c0, span, axis=2),
                          lax.dynamic_slice_in_dim(vw_pad, c0, span, axis=2),
                          c0 - WINDOW + jnp.arange(span), tbl)

    o = lax.map(body, (chunks(q), chunks(gates), jnp.arange(n_chunks) * NSA_QCHUNK))
    o = o.swapaxes(0, 1).reshape(b, s, *q.shape[2:])
    wb = min(WINDOW, s)
    return o, kw_r[:, :, s - wb:], vw_r[:, :, s - wb:]


def nsa_sample(q, gates, kc_r, vc_r, ks_r, vs_r, kw_r, vw_r, lw, tbl, layer, pool_ck, pool_cv, pool_sk, pool_sv, win_k, win_v, page_table):
    b, t = q.shape[:2]
    past = page_table.shape[1] * PAGE_SIZE
    t_cmp = -(-t // CMP_STRIDE) * CMP_STRIDE
    nnb = -(-t // SLC_BLOCK)
    npb = past // SLC_BLOCK
    bpp = PAGE_SIZE // SLC_BLOCK

    def past_rows(pool):
        g = pool[layer, page_table]
        return g.transpose(0, 2, 1, 3, 4).reshape(b, NSA_KV_HEADS, past, HEAD_DIM)

    def pad_t(a, n):
        return jnp.pad(a, ((0, 0), (0, 0), (0, n - a.shape[2]), (0, 0)))

    kc = compress(jnp.concatenate([past_rows(pool_ck), pad_t(kc_r, t_cmp)], axis=2), lw['cmp_pe_k'], lw['cmp_w1_k'], lw['cmp_w2_k'])
    vc = compress(jnp.concatenate([past_rows(pool_cv), pad_t(vc_r, t_cmp)], axis=2), lw['cmp_pe_v'], lw['cmp_w1_v'], lw['cmp_w2_v'])
    new_ks = pad_t(ks_r, nnb * SLC_BLOCK).reshape(b, NSA_KV_HEADS, nnb, SLC_BLOCK, HEAD_DIM)
    new_vs = pad_t(vs_r, nnb * SLC_BLOCK).reshape(b, NSA_KV_HEADS, nnb, SLC_BLOCK, HEAD_DIM)
    bi = jnp.arange(b)[:, None, None, None]
    hi = jnp.arange(NSA_KV_HEADS)[None, None, :, None]
    offs = jnp.arange(SLC_BLOCK)

    def gather_slc(idx):
        in_past = (idx < npb)[..., None, None]
        pidx = jnp.minimum(idx, npb - 1)
        page = page_table[bi, pidx // bpp][..., None]
        rows = ((pidx % bpp) * SLC_BLOCK)[..., None] + offs
        nidx = jnp.clip(idx - npb, 0, nnb - 1)
        ks = jnp.where(in_past, pool_sk[layer, page, hi[..., None], rows], new_ks[bi, hi, nidx])
        vs = jnp.where(in_past, pool_sv[layer, page, hi[..., None], rows], new_vs[bi, hi, nidx])
        return ks, vs

    kw = jnp.concatenate([win_k.astype(kw_r.dtype), kw_r], axis=2)
    vw = jnp.concatenate([win_v.astype(vw_r.dtype), vw_r], axis=2)
    wb = win_k.shape[2]
    o = nsa_attend(q, past + jnp.arange(t), gates, kc, vc, slc_overlap(kc.shape[2], npb + nnb), gather_slc,
                   kw, vw, past - wb + jnp.arange(wb + t), tbl)
    return o, kw[:, :, -wb:], vw[:, :, -wb:]


def token_mixer(h, lw, gla_s0, conv_prev, nsa_fn):
    b, t, _ = h.shape
    f32 = jnp.float32
    (gq, gk, gv, gg, ga, cb, cc, cx, nq, nkc, nvc, nks, nvs, nkw, nvw, ngt, mgt) = jnp.split(
        in_proj(h, lw['w_in']), np.cumsum(IN_SIZES)[:-1].tolist(), axis=-1)
    q = gq.reshape(b, t, GLA_HEADS, GLA_DK).astype(f32) * GLA_DK ** -0.5
    k = gk.reshape(b, t, GLA_HEADS, GLA_DK).astype(f32)
    v = gv.reshape(b, t, GLA_HEADS, GLA_DV).astype(f32)
    log_a = (jax.nn.log_sigmoid((ga @ lw['gla_wa2'] + lw['gla_ba']).astype(f32)) / GLA_TAU).reshape(b, t, GLA_HEADS, GLA_DK)
    o, s_fin = gla_scan(q, k, v, log_a, gla_s0.astype(f32))
    o_gla = gla_output(o, gg, lw['gla_norm_g']).astype(h.dtype)
    o_conv, conv_state = short_conv(cb, cc, cx, conv_prev.astype(h.dtype), lw['conv_w'])

    def rows(a):
        return a.reshape(b, t, NSA_KV_HEADS, HEAD_DIM).transpose(0, 2, 1, 3)
    kc_r, vc_r, ks_r, vs_r, kw_r, vw_r = [rows(a) for a in (nkc, nvc, nks, nvs, nkw, nvw)]
    q_n = nq.reshape(b, t, NSA_KV_HEADS, NSA_GROUP, HEAD_DIM)
    g_n = jax.nn.sigmoid(ngt).reshape(b, t, NSA_KV_HEADS, NSA_GROUP, 3)
    o_n, win_k, win_v = nsa_fn(q_n, g_n, kc_r, vc_r, ks_r, vs_r, kw_r, vw_r)
    s_a, s_b, s_c = jnp.split(jax.nn.sigmoid(mgt), 3, axis=-1)
    merged = (s_a * (o_gla @ lw['w_br_gla']) + s_b * (o_conv @ lw['w_br_conv'])
              + s_c * (o_n.reshape(b, t, NSA_Q) @ lw['w_br_nsa']))
    return merged @ lw['w_o'], (kc_r, vc_r, ks_r, vs_r, win_k, win_v, s_fin.astype(h.dtype), conv_state)


def moe_dispatch(hf, idx, wts, wg, wu, wd):
    n, d = hf.shape
    a = n * TOP_K
    fe = idx.reshape(-1)
    ft = jnp.arange(a) // TOP_K
    fw = wts.reshape(-1)
    order = jnp.argsort(fe)
    se, st, sw = fe[order], ft[order], fw[order]
    counts = jnp.zeros((N_EXPERTS,), jnp.int32).at[fe].add(1)
    starts = jnp.cumsum(counts) - counts
    padded = (counts + MOE_BLOCK - 1) // MOE_BLOCK * MOE_BLOCK
    pend = jnp.cumsum(padded)
    pstart = pend - padded
    dest = pstart[se] + jnp.arange(a) - starts[se]
    nb = -(-a // MOE_BLOCK) + N_EXPERTS
    rows = nb * MOE_BLOCK
    row_tok = jnp.full((rows,), n, jnp.int32).at[dest].set(st)
    row_w = jnp.zeros((rows,), hf.dtype).at[dest].set(sw)
    blk_e = jnp.minimum(jnp.searchsorted(pend, jnp.arange(nb) * MOE_BLOCK, side='right'), N_EXPERTS - 1)
    h_pad = jnp.concatenate([hf, jnp.zeros((1, d), hf.dtype)], axis=0)

    def expert_block(inp):
        tok, e = inp
        xb = h_pad[tok]
        return (jax.nn.silu(xb @ wg[e]) * (xb @ wu[e])) @ wd[e]

    y = lax.map(expert_block, (row_tok.reshape(nb, MOE_BLOCK), blk_e))
    return jnp.zeros((n + 1, d), y.dtype).at[row_tok].add(y.reshape(rows, d) * row_w[:, None])[:n]


def moe(h, lw):
    b, t, d = h.shape
    n = b * t
    hf = h.reshape(n, d)
    s = jax.nn.sigmoid((hf @ lw['w_router']).astype(jnp.float32))
    sel = s + lw['router_bias'].astype(jnp.float32)
    per = N_EXPERTS // N_GROUPS
    gscore = lax.top_k(sel.reshape(n, N_GROUPS, per), 2)[0].sum(-1)
    gidx = lax.top_k(gscore, TOPK_GROUPS)[1]
    gmask = jax.nn.one_hot(gidx, N_GROUPS).sum(1) > 0
    sel = jnp.where(jnp.repeat(gmask, per, axis=1), sel, NEG_INF)
    idx = lax.top_k(sel, TOP_K)[1]
    w = jnp.take_along_axis(s, idx, axis=1)
    w = w / w.sum(-1, keepdims=True) * ROUTED_SCALE
    routed = moe_dispatch(hf, idx, w.astype(h.dtype), lw['w_exp_gate'], lw['w_exp_up'], lw['w_exp_down'])
    shared = (jax.nn.silu(hf @ lw['w_sh_gate']) * (hf @ lw['w_sh_up'])) @ lw['w_sh_down']
    return (routed + shared).reshape(b, t, d)


def block(x, c, lw, mixer):
    mod = (jax.nn.silu(c) @ lw['w_ada'] + lw['b_ada'])[:, None, :]
    sh1, sc1, g1, sh2, sc2, g2 = jnp.split(mod, 6, axis=-1)
    mix, state = mixer(x * (1 + sc1) + sh1)
    x = layer_norm(DN_ALPHA * x + g1 * mix, lw['ln1_g'], lw['ln1_b'])
    x = layer_norm(DN_ALPHA * x + g2 * moe(x * (1 + sc2) + sh2, lw), lw['ln2_g'], lw['ln2_b'])
    return x, state


def kernel(x_prompt, x_sample, cache_cmp_k, cache_cmp_v, cache_slc_k, cache_slc_v, cache_win_k, cache_win_v,
           state_gla, state_conv, page_table, c_prompt, c_sample, w_ada, b_ada, w_in, gla_wa2, gla_ba,
           gla_norm_g, conv_w, cmp_pe_k, cmp_pe_v, cmp_w1_k, cmp_w2_k, cmp_w1_v, cmp_w2_v, rel_bias,
           w_br_gla, w_br_conv, w_br_nsa, w_o, ln1_g, ln1_b, ln2_g, ln2_b, w_router, router_bias,
           w_exp_gate, w_exp_up, w_exp_down, w_sh_gate, w_sh_up, w_sh_down):
    tbl = rel_bias.reshape(REL_BUCKETS, NSA_KV_HEADS, NSA_GROUP)
    xp, xs = x_prompt, x_sample
    p_states, s_states = [], []
    for l in range(DEPTH):
        lw = {
            'w_ada': w_ada[l], 'b_ada': b_ada[l], 'w_in': w_in[l], 'gla_wa2': gla_wa2[l], 'gla_ba': gla_ba[l],
            'gla_norm_g': gla_norm_g[l], 'conv_w': conv_w[l], 'cmp_pe_k': cmp_pe_k[l], 'cmp_pe_v': cmp_pe_v[l],
            'cmp_w1_k': cmp_w1_k[l], 'cmp_w2_k': cmp_w2_k[l], 'cmp_w1_v': cmp_w1_v[l], 'cmp_w2_v': cmp_w2_v[l],
            'w_br_gla': w_br_gla[l], 'w_br_conv': w_br_conv[l], 'w_br_nsa': w_br_nsa[l], 'w_o': w_o[l],
            'ln1_g': ln1_g[l], 'ln1_b': ln1_b[l], 'ln2_g': ln2_g[l], 'ln2_b': ln2_b[l],
            'w_router': w_router[l], 'router_bias': router_bias[l], 'w_exp_gate': w_exp_gate[l],
            'w_exp_up': w_exp_up[l], 'w_exp_down': w_exp_down[l], 'w_sh_gate': w_sh_gate[l],
            'w_sh_up': w_sh_up[l], 'w_sh_down': w_sh_down[l],
        }
        prompt_mixer = functools.partial(
            token_mixer, lw=lw,
            gla_s0=jnp.zeros((xp.shape[0], GLA_HEADS, GLA_DK, GLA_DV), jnp.float32),
            conv_prev=jnp.zeros((xp.shape[0], CONV_W - 1, CONV_DIM), xp.dtype),
            nsa_fn=functools.partial(nsa_prompt, lw=lw, tbl=tbl))
        sample_mixer = functools.partial(
            token_mixer, lw=lw, gla_s0=state_gla[l], conv_prev=state_conv[l],
            nsa_fn=functools.partial(nsa_sample, lw=lw, tbl=tbl, layer=l, pool_ck=cache_cmp_k, pool_cv=cache_cmp_v,
                                     pool_sk=cache_slc_k, pool_sv=cache_slc_v, win_k=cache_win_k[l],
                                     win_v=cache_win_v[l], page_table=page_table))
        xp, st = block(xp, c_prompt, lw, prompt_mixer)
        p_states.append(st)
        xs, st = block(xs, c_sample, lw, sample_mixer)
        s_states.append(st)
    p_out = [jnp.stack(z) for z in zip(*p_states)]
    s_out = [jnp.stack(z) for z in zip(*s_states)]
    return (xp, xs, *p_out, *s_out)


MERGE_ROWS = 256
CONV_HALO = 8


def _layer_norm_rows(z, g, b):
    mu = jnp.mean(z, axis=-1, keepdims=True)
    var = jnp.mean(jnp.square(z - mu), axis=-1, keepdims=True)
    return (z - mu) * lax.rsqrt(var + LN_EPS) * g + b


def _merge_kernel(x_ref, cb_ref, cc_ref, cx_ref, ccp_ref, cxp_ref, prev_ref, ma_ref, mb_ref, mc_ref, og_ref, on_ref,
                  cw_ref, wg_ref, wc_ref, wn_ref, wo_ref, g1_ref, sc2_ref, sh2_ref, lng_ref, lnb_ref, wr_ref,
                  x1_ref, h2_ref, s_ref, cst_ref):
    j = pl.program_id(1)
    tm = x_ref.shape[1]
    u = cc_ref[0] * cx_ref[0]
    halo = jnp.where(j == 0, prev_ref[0], ccp_ref[0] * cxp_ref[0])
    p1 = halo[CONV_HALO - 1:CONV_HALO, :]
    p2 = halo[CONV_HALO - 2:CONV_HALO - 1, :]
    rid = lax.broadcasted_iota(jnp.int32, (tm, 1), 0)
    u1 = jnp.where(rid == 0, p1, pltpu.roll(u, 1, axis=0))
    u2 = jnp.where(rid == 0, p2, jnp.where(rid == 1, p1, pltpu.roll(u, 2, axis=0)))
    cw = cw_ref[...]
    o_conv = cb_ref[0] * (u2 * cw[0:1, :] + u1 * cw[1:2, :] + u * cw[2:3, :])
    cst_ref[0] = u[tm - CONV_HALO:tm, :]
    merged = (jax.nn.sigmoid(ma_ref[0]) * _dot(og_ref[0].astype(BF16), wg_ref[...])
              + jax.nn.sigmoid(mb_ref[0]) * _dot(o_conv.astype(BF16), wc_ref[...])
              + jax.nn.sigmoid(mc_ref[0]) * _dot(on_ref[0].astype(BF16), wn_ref[...]))
    mix = _dot(merged.astype(BF16), wo_ref[...])
    x1 = _layer_norm_rows(DN_ALPHA * x_ref[0] + g1_ref[0] * mix, lng_ref[...], lnb_ref[...])
    x1_ref[0] = x1
    h2 = x1 * (1.0 + sc2_ref[0]) + sh2_ref[0]
    h2_ref[0] = h2.astype(h2_ref.dtype)
    s_ref[0] = jax.nn.sigmoid(_dot_hi(h2, wr_ref[...]))


def merge_pallas(x, y, o_gla, o_n, conv_prev, g1, sc2, sh2, lw):
    b, t, d = x.shape
    tm = math.gcd(t, MERGE_ROWS)
    halo = CONV_HALO
    hb = tm // halo
    prev = jnp.concatenate([jnp.zeros((b, halo - (CONV_W - 1), d), F32), conv_prev.astype(F32)], axis=1)
    col = lambda name, k=0: PROJ_OFF[name] // d + k
    tile = lambda cb_: pl.BlockSpec((1, tm, d), lambda i, j: (i, j, cb_))
    halo_spec = lambda cb_: pl.BlockSpec((1, halo, d), lambda i, j: (i, jnp.maximum(j * hb - 1, 0), cb_))
    vec = pl.BlockSpec((1, 1, d), lambda i, j: (i, 0, 0))
    const = lambda a: pl.BlockSpec(a.shape, lambda i, j: (0,) * a.ndim)
    wts = [lw['w_br_gla'].astype(BF16), lw['w_br_conv'].astype(BF16), lw['w_br_nsa'].astype(BF16), lw['w_o'].astype(BF16)]
    lng, lnb = lw['ln1_g'].reshape(1, d), lw['ln1_b'].reshape(1, d)
    x1, h2, s, cst = pl.pallas_call(
        _merge_kernel,
        grid=(b, t // tm),
        in_specs=[tile(0), tile(col('cb')), tile(col('cc')), tile(col('cx')), halo_spec(col('cc')), halo_spec(col('cx')),
                  pl.BlockSpec((1, halo, d), lambda i, j: (i, 0, 0)),
                  tile(col('mgt', 0)), tile(col('mgt', 1)), tile(col('mgt', 2)), tile(0), tile(0),
                  const(lw['conv_w'])] + [const(w) for w in wts] + [vec, vec, vec, const(lng), const(lnb),
                                                                    const(lw['w_router'])],
        out_specs=[tile(0), tile(0), pl.BlockSpec((1, tm, N_EXPERTS), lambda i, j: (i, j, 0)),
                   pl.BlockSpec((1, halo, d), lambda i, j: (i, 0, 0))],
        out_shape=[jax.ShapeDtypeStruct((b, t, d), F32), jax.ShapeDtypeStruct((b, t, d), BF16 if tm % 16 == 0 else F32),
                   jax.ShapeDtypeStruct((b, t, N_EXPERTS), F32), jax.ShapeDtypeStruct((b, halo, d), F32)],
        compiler_params=pltpu.CompilerParams(vmem_limit_bytes=VMEM_LIMIT),
        name="merge_ln1",
    )(x, y, y, y, y, y, prev, y, y, y, o_gla, o_n, lw['conv_w'], *wts,
      g1.reshape(b, 1, d), sc2.reshape(b, 1, d), sh2.reshape(b, 1, d), lng, lnb, lw['w_router'])
    return x1, h2, s, cst[:, halo - (CONV_W - 1):]


def route(s, router_bias):
    n = s.shape[0]
    sel = s + router_bias.astype(F32)
    per = N_EXPERTS // N_GROUPS
    gscore = lax.top_k(sel.reshape(n, N_GROUPS, per), 2)[0].sum(-1)
    gidx = lax.top_k(gscore, TOPK_GROUPS)[1]
    gmask = jax.nn.one_hot(gidx, N_GROUPS).sum(1) > 0
    sel = jnp.where(jnp.repeat(gmask, per, axis=1), sel, NEG_INF)
    idx = lax.top_k(sel, TOP_K)[1]
    w = jnp.take_along_axis(s, idx, axis=1)
    return idx, w / w.sum(-1, keepdims=True) * ROUTED_SCALE


def _expert_act(xb, wg, wu):
    hg = _dot(xb, wg)
    return hg * jax.nn.sigmoid(hg) * _dot(xb, wu)


def _grouped_expert_kernel(be_ref, x_ref, wg_ref, wu_ref, wd_ref, o_ref):
    act = _expert_act(x_ref[...], wg_ref[0], wu_ref[0])
    o_ref[...] = _dot(act.astype(BF16), wd_ref[0]).astype(o_ref.dtype)


def moe_routed_sorted(h2, idx, w, lw):
    n, d = h2.shape
    a = n * TOP_K
    hit = jnp.zeros((n, N_EXPERTS), F32).at[jnp.arange(n)[:, None], idx].add(1.0)
    ct = math.gcd(n, 512)
    hit_t = hit.reshape(n // ct, ct, N_EXPERTS)
    within = jnp.einsum('ts,nse->nte', jnp.tril(jnp.ones((ct, ct), F32), -1), hit_t)
    tile_tot = hit_t.sum(1)
    before = jnp.cumsum(tile_tot, axis=0) - tile_tot
    pos = (within + before[:, None, :]).reshape(n, N_EXPERTS).astype(jnp.int32)
    counts = tile_tot.sum(0).astype(jnp.int32)
    padded = (counts + MOE_BLOCK - 1) // MOE_BLOCK * MOE_BLOCK
    pend = jnp.cumsum(padded)
    pstart = pend - padded
    slot = jnp.take_along_axis(pos + pstart[None, :], idx, axis=1)
    nb = -(-a // MOE_BLOCK) + N_EXPERTS
    rows = nb * MOE_BLOCK
    row_tok = jnp.zeros((rows,), jnp.int32).at[slot.reshape(-1)].set(jnp.arange(a, dtype=jnp.int32) // TOP_K)
    blk_e = jnp.minimum(jnp.searchsorted(pend, jnp.arange(nb) * MOE_BLOCK, side='right'), N_EXPERTS - 1).astype(jnp.int32)
    xs = h2.astype(BF16)[row_tok]
    wg, wu, wd = lw['w_exp_gate'].astype(BF16), lw['w_exp_up'].astype(BF16), lw['w_exp_down'].astype(BF16)
    hdim = wg.shape[2]
    y = pl.pallas_call(
        _grouped_expert_kernel,
        grid_spec=pltpu.PrefetchScalarGridSpec(
            num_scalar_prefetch=1,
            grid=(nb,),
            in_specs=[pl.BlockSpec((MOE_BLOCK, d), lambda i, be: (i, 0)),
                      pl.BlockSpec((1, d, hdim), lambda i, be: (be[i], 0, 0)),
                      pl.BlockSpec((1, d, hdim), lambda i, be: (be[i], 0, 0)),
                      pl.BlockSpec((1, hdim, d), lambda i, be: (be[i], 0, 0))],
            out_specs=pl.BlockSpec((MOE_BLOCK, d), lambda i, be: (i, 0))),
        out_shape=jax.ShapeDtypeStruct((rows, d), BF16),
        name="moe_grouped",
    )(blk_e, xs, wg, wu, wd)
    return jnp.einsum('nk,nkd->nd', w, y[slot].astype(F32))


def _dense_expert_kernel(x_ref, wt_ref, wg_ref, wu_ref, wd_ref, o_ref):
    @pl.when(pl.program_id(0) == 0)
    def _():
        o_ref[...] = jnp.zeros_like(o_ref)
    act = _expert_act(x_ref[...].astype(BF16), wg_ref[0].astype(BF16), wu_ref[0].astype(BF16))
    o_ref[...] += _dot((act * wt_ref[0]).astype(BF16), wd_ref[0].astype(BF16))


def moe_routed_dense(h2, idx, w, lw):
    n, d = h2.shape
    wdense = jnp.zeros((n, N_EXPERTS), F32).at[jnp.arange(n)[:, None], idx].add(w)
    wt = wdense.T.reshape(N_EXPERTS, n, 1)
    hdim = lw['w_exp_gate'].shape[2]
    return pl.pallas_call(
        _dense_expert_kernel,
        grid=(N_EXPERTS,),
        in_specs=[pl.BlockSpec((n, d), lambda e: (0, 0)),
                  pl.BlockSpec((1, n, 1), lambda e: (e, 0, 0)),
                  pl.BlockSpec((1, d, hdim), lambda e: (e, 0, 0)),
                  pl.BlockSpec((1, d, hdim), lambda e: (e, 0, 0)),
                  pl.BlockSpec((1, hdim, d), lambda e: (e, 0, 0))],
        out_specs=pl.BlockSpec((n, d), lambda e: (0, 0)),
        out_shape=jax.ShapeDtypeStruct((n, d), F32),
        name="moe_dense",
    )(h2, wt, lw['w_exp_gate'], lw['w_exp_up'], lw['w_exp_down'])


def _moe_tail_kernel(x1_ref, h2_ref, r_ref, g2_ref, wg_ref, wu_ref, wd_ref, lng_ref, lnb_ref, o_ref):
    act = _expert_act(h2_ref[0].astype(BF16), wg_ref[...], wu_ref[...])
    shared = _dot(act.astype(BF16), wd_ref[...])
    o_ref[0] = _layer_norm_rows(DN_ALPHA * x1_ref[0] + g2_ref[0] * (r_ref[0] + shared), lng_ref[...], lnb_ref[...])


def moe_tail_pallas(x1, h2, routed, g2, lw):
    b, t, d = x1.shape
    tm = math.gcd(t, MERGE_ROWS)
    tile = pl.BlockSpec((1, tm, d), lambda i, j: (i, j, 0))
    const = lambda a: pl.BlockSpec(a.shape, lambda i, j: (0,) * a.ndim)
    wg, wu, wd = lw['w_sh_gate'].astype(BF16), lw['w_sh_up'].astype(BF16), lw['w_sh_down'].astype(BF16)
    lng, lnb = lw['ln2_g'].reshape(1, d), lw['ln2_b'].reshape(1, d)
    return pl.pallas_call(
        _moe_tail_kernel,
        grid=(b, t // tm),
        in_specs=[tile, tile, tile, pl.BlockSpec((1, 1, d), lambda i, j: (i, 0, 0)),
                  const(wg), const(wu), const(wd), const(lng), const(lnb)],
        out_specs=tile,
        out_shape=jax.ShapeDtypeStruct((b, t, d), F32),
        name="moe_tail_ln2",
    )(x1, h2, routed.reshape(b, t, d), g2.reshape(b, 1, d), wg, wu, wd, lng, lnb)


def layer_block(x, c, lw, gla_s0, conv_prev, nsa_fn, sorted_moe):
    b, t, d = x.shape
    mod = jax.nn.silu(c) @ lw['w_ada'] + lw['b_ada']
    sh1, sc1, g1, sh2, sc2, g2 = jnp.split(mod, 6, axis=-1)
    if t % LANES == 0:
        y = in_proj(x, sc1, sh1, lw['w_in'])
    else:
        h1 = (x * (1.0 + sc1[:, None, :]) + sh1[:, None, :]).reshape(1, b * t, d)
        y = in_proj(h1, jnp.zeros((1, d), F32), jnp.zeros((1, d), F32), lw['w_in']).reshape(b, t, PROJ_COLS)
    o_gla, s_fin = gla_pallas(y, lw, gla_s0)

    def rows(name):
        return proj_cols(y, name).reshape(b, t, NSA_KV_HEADS, HEAD_DIM).transpose(0, 2, 1, 3)
    kv_rows = [rows(nm) for nm in ('nkc', 'nvc', 'nks', 'nvs', 'nkw', 'nvw')]
    o_n, win_k, win_v = nsa_fn(y, *kv_rows)
    x1, h2, s, conv_state = merge_pallas(x, y, o_gla, o_n, conv_prev, g1, sc2, sh2, lw)
    idx, w = route(s.reshape(b * t, N_EXPERTS), lw['router_bias'])
    routed = (moe_routed_sorted if sorted_moe else moe_routed_dense)(h2.reshape(b * t, d), idx, w, lw)
    x2 = moe_tail_pallas(x1, h2, routed, g2, lw)
    return x2, (*kv_rows[:4], win_k, win_v, s_fin, conv_state)


def kernel(x_prompt, x_sample, cache_cmp_k, cache_cmp_v, cache_slc_k, cache_slc_v, cache_win_k, cache_win_v,
           state_gla, state_conv, page_table, c_prompt, c_sample, w_ada, b_ada, w_in, gla_wa2, gla_ba,
           gla_norm_g, conv_w, cmp_pe_k, cmp_pe_v, cmp_w1_k, cmp_w2_k, cmp_w1_v, cmp_w2_v, rel_bias,
           w_br_gla, w_br_conv, w_br_nsa, w_o, ln1_g, ln1_b, ln2_g, ln2_b, w_router, router_bias,
           w_exp_gate, w_exp_up, w_exp_down, w_sh_gate, w_sh_up, w_sh_down):
    params = dict(w_ada=w_ada, b_ada=b_ada, w_in=w_in, gla_wa2=gla_wa2, gla_ba=gla_ba, gla_norm_g=gla_norm_g,
                  conv_w=conv_w, cmp_pe_k=cmp_pe_k, cmp_pe_v=cmp_pe_v, cmp_w1_k=cmp_w1_k, cmp_w2_k=cmp_w2_k,
                  cmp_w1_v=cmp_w1_v, cmp_w2_v=cmp_w2_v, w_br_gla=w_br_gla, w_br_conv=w_br_conv, w_br_nsa=w_br_nsa,
                  w_o=w_o, ln1_g=ln1_g, ln1_b=ln1_b, ln2_g=ln2_g, ln2_b=ln2_b, w_router=w_router,
                  router_bias=router_bias, w_exp_gate=w_exp_gate, w_exp_up=w_exp_up, w_exp_down=w_exp_down,
                  w_sh_gate=w_sh_gate, w_sh_up=w_sh_up, w_sh_down=w_sh_down)
    tbl = rel_bias.reshape(REL_BUCKETS, NSA_KV_HEADS, NSA_GROUP)
    bp, tp, _ = x_prompt.shape
    bs, ts, _ = x_sample.shape
    bias_tables = nsa_bias_tables(tbl, tp, tp // CMP_STRIDE)
    xp, xs = x_prompt, x_sample
    p_states, s_states = [], []
    for l in range(DEPTH):
        lw = {k: v[l] for k, v in params.items()}

        def prompt_nsa(y, kc_r, vc_r, ks_r, vs_r, kw_r, vw_r):
            o_n = nsa_prompt_pallas(y, PROJ_OFF['nq'], proj_cols(y, 'ngt'), kc_r, vc_r, ks_r, vs_r, kw_r, vw_r,
                                    lw, bias_tables)
            wb = min(WINDOW, tp)
            return o_n, kw_r[:, :, tp - wb:], vw_r[:, :, tp - wb:]

        def sample_nsa(y, kc_r, vc_r, ks_r, vs_r, kw_r, vw_r):
            q_n = proj_cols(y, 'nq').reshape(bs, ts, NSA_KV_HEADS, NSA_GROUP, HEAD_DIM)
            g_n = jax.nn.sigmoid(proj_cols(y, 'ngt')).reshape(bs, ts, NSA_KV_HEADS, NSA_GROUP, 3)
            o_n, win_k, win_v = nsa_sample(q_n, g_n, kc_r, vc_r, ks_r, vs_r, kw_r, vw_r, lw, tbl, l, cache_cmp_k,
                                           cache_cmp_v, cache_slc_k, cache_slc_v, cache_win_k[l], cache_win_v[l],
                                           page_table)
            return o_n.reshape(bs, ts, NSA_Q), win_k, win_v

        xp, st = layer_block(xp, c_prompt, lw, jnp.zeros((bp, GLA_HEADS, GLA_DK, GLA_DV), F32),
                             jnp.zeros((bp, CONV_W - 1, CONV_DIM), F32), prompt_nsa, True)
        p_states.append(st)
        xs, st = layer_block(xs, c_sample, lw, state_gla[l], state_conv[l], sample_nsa, False)
        s_states.append(st)
    p_out = [jnp.stack(z) for z in zip(*p_states)]
    s_out = [jnp.stack(z) for z in zip(*s_states)]
    return (xp, xs, *p_out, *s_out)
```

```python
import math, functools
import jax, jax.numpy as jnp
from jax import lax
import numpy as np
from jax.experimental import pallas as pl
from jax.experimental.pallas import tpu as pltpu

D_MODEL = 1024
DEPTH = 2
PAGE_SIZE = 128
GLA_HEADS = 4
GLA_DK = 128
GLA_DV = 256
GLA_RANK = 16
GLA_TAU = 16.0
GLA_CHUNK = 64
CONV_DIM = D_MODEL
CONV_W = 3
NSA_HEADS = 16
NSA_KV_HEADS = 4
NSA_GROUP = NSA_HEADS // NSA_KV_HEADS
HEAD_DIM = 64
CMP_BLOCK = 32
CMP_STRIDE = 16
CMP_HIDDEN = 128
SLC_BLOCK = 64
SLC_TOPK = 16
WINDOW = 512
NSA_QCHUNK = 16
REL_BUCKETS = 32
REL_MAX_DIST = 128
N_EXPERTS = 64
TOP_K = 8
N_GROUPS = 8
TOPK_GROUPS = 4
EXPERT_HIDDEN = 256
SHARED_HIDDEN = 256
ROUTED_SCALE = 2.5
MOE_BLOCK = 128
DN_ALPHA = (2 * DEPTH) ** 0.25
LN_EPS = 1e-5
NEG_INF = -1e30

GLA_QK = GLA_HEADS * GLA_DK
GLA_V = GLA_HEADS * GLA_DV
NSA_Q = NSA_HEADS * HEAD_DIM
NSA_KV = NSA_KV_HEADS * HEAD_DIM
IN_SIZES = (GLA_QK, GLA_QK, GLA_V, GLA_V, GLA_RANK, CONV_DIM, CONV_DIM, CONV_DIM, NSA_Q, NSA_KV, NSA_KV, NSA_KV, NSA_KV, NSA_KV, NSA_KV, 3 * NSA_HEADS, 3 * D_MODEL)
IN_TOTAL = sum(IN_SIZES)


BF16 = jnp.bfloat16
F32 = jnp.float32
LANES = 128
NSA_TILE = 128
GLA_ROWS = 256
VMEM_LIMIT = 48 * 1024 * 1024

PROJ_NAMES = ('gq', 'gk', 'gv', 'gg', 'ga', 'cb', 'cc', 'cx', 'nq', 'nkc', 'nvc', 'nks', 'nvs', 'nkw', 'nvw', 'ngt', 'mgt')
PROJ_ORDER = ('gv', 'gg', 'cb', 'cc', 'cx', 'nq', 'mgt', 'gq', 'gk', 'nkc', 'nvc', 'nks', 'nvs', 'nkw', 'nvw', 'ga', 'ngt')
PROJ_SIZE = dict(zip(PROJ_NAMES, IN_SIZES))
PROJ_SRC = dict(zip(PROJ_NAMES, np.cumsum((0,) + IN_SIZES[:-1]).tolist()))
PROJ_OFF = dict(zip(PROJ_ORDER, np.cumsum([0] + [PROJ_SIZE[n] for n in PROJ_ORDER[:-1]]).tolist()))
PROJ_TN = 512
PROJ_COLS = -(-IN_TOTAL // PROJ_TN) * PROJ_TN


def _dot_nt(a, b):
    return lax.dot_general(a, b, (((1,), (1,)), ((), ())), preferred_element_type=F32)


def _dot(a, b):
    return jnp.dot(a, b, preferred_element_type=F32)


def _dot_hi(a, b):
    return jnp.dot(a, b, precision=lax.Precision.HIGHEST, preferred_element_type=F32)


def _in_proj_kernel(x_ref, sc_ref, sh_ref, w_ref, o_ref, h_ref):
    @pl.when(pl.program_id(2) == 0)
    def _():
        h_ref[...] = (x_ref[0] * (1.0 + sc_ref[0]) + sh_ref[0]).astype(BF16)
    o_ref[0] = _dot(h_ref[...], w_ref[...])


def in_proj(x, sc, sh, w_in):
    b, t, d = x.shape
    w = jnp.concatenate([w_in[:, PROJ_SRC[n]:PROJ_SRC[n] + PROJ_SIZE[n]] for n in PROJ_ORDER]
                        + [jnp.zeros((d, PROJ_COLS - IN_TOTAL), w_in.dtype)], axis=1).astype(BF16)
    tm = math.gcd(t, 1024)
    return pl.pallas_call(
        _in_proj_kernel,
        grid=(b, t // tm, PROJ_COLS // PROJ_TN),
        in_specs=[pl.BlockSpec((1, tm, d), lambda i, j, k: (i, j, 0)),
                  pl.BlockSpec((1, 1, d), lambda i, j, k: (i, 0, 0)),
                  pl.BlockSpec((1, 1, d), lambda i, j, k: (i, 0, 0)),
                  pl.BlockSpec((d, PROJ_TN), lambda i, j, k: (0, k))],
        out_specs=pl.BlockSpec((1, tm, PROJ_TN), lambda i, j, k: (i, j, k)),
        out_shape=jax.ShapeDtypeStruct((b, t, PROJ_COLS), F32),
        scratch_shapes=[pltpu.VMEM((tm, d), BF16)],
        compiler_params=pltpu.CompilerParams(vmem_limit_bytes=VMEM_LIMIT),
        name="in_proj",
    )(x, sc.reshape(b, 1, d), sh.reshape(b, 1, d), w)


def proj_cols(y, name):
    return y[..., PROJ_OFF[name]:PROJ_OFF[name] + PROJ_SIZE[name]]


def _gla_kernel(q_ref, k_ref, v_ref, gg_ref, sm_ref, wa2_ref, ba_ref, ng_ref, s0_ref, o_ref, sfin_ref, st_ref,
                *, chunk, nchunks):
    tb = pl.program_id(2)
    c = chunk

    @pl.when(tb == 0)
    def _():
        st_ref[...] = s0_ref[0, 0].T

    row = lax.broadcasted_iota(jnp.int32, (c, c), 0)
    col = lax.broadcasted_iota(jnp.int32, (c, c), 1)
    causal = row >= col
    tri = jnp.where(causal, 1.0, 0.0)
    ga_off = PROJ_OFF['ga'] % LANES
    for ci in range(nchunks):
        sl = slice(ci * c, (ci + 1) * c)
        q = q_ref[0, sl, :] * (GLA_DK ** -0.5)
        k = k_ref[0, sl, :]
        v = v_ref[0, sl, :].astype(BF16)
        ga = sm_ref[0, sl, ga_off:ga_off + GLA_RANK]
        x = _dot_hi(ga, wa2_ref[...]) + ba_ref[...]
        log_a = (jnp.minimum(x, 0.0) - jnp.log1p(jnp.exp(-jnp.abs(x)))) * (1.0 / GLA_TAU)
        cum = _dot_hi(tri, log_a)
        last = cum[c - 1:c, :]
        mid = cum[c // 2 - 1:c // 2, :]
        st = st_ref[...]
        inter = _dot_nt((q * jnp.exp(cum)).astype(BF16), st.astype(BF16))
        att = _dot_nt((q * jnp.exp(cum - mid)).astype(BF16), (k * jnp.exp(mid - cum)).astype(BF16))
        att = jnp.where(causal, att, 0.0)
        o = inter + _dot(att.astype(BF16), v)
        kd = (k * jnp.exp(last - cum)).astype(BF16)
        st_ref[...] = st * jnp.exp(last) + lax.dot_general(v, kd, (((0,), (0,)), ((), ())),
                                                           preferred_element_type=F32)
        mu = jnp.mean(o, axis=-1, keepdims=True)
        var = jnp.mean(jnp.square(o - mu), axis=-1, keepdims=True)
        gate = gg_ref[0, sl, :]
        o_ref[0, sl, :] = (o - mu) * lax.rsqrt(var + LN_EPS) * ng_ref[...] * (gate * jax.nn.sigmoid(gate))

    @pl.when(tb == pl.num_programs(2) - 1)
    def _():
        sfin_ref[0, 0] = st_ref[...].T


def gla_pallas(y, lw, s0):
    b, t, _ = y.shape
    c = math.gcd(t, GLA_CHUNK)
    r = math.gcd(t, GLA_ROWS)
    qb, kb = PROJ_OFF['gq'] // GLA_DK, PROJ_OFF['gk'] // GLA_DK
    vb, gb = PROJ_OFF['gv'] // GLA_DV, PROJ_OFF['gg'] // GLA_DV
    sb = PROJ_OFF['ga'] // LANES
    return pl.pallas_call(
        functools.partial(_gla_kernel, chunk=c, nchunks=r // c),
        grid=(b, GLA_HEADS, t // r),
        in_specs=[pl.BlockSpec((1, r, GLA_DK), lambda i, h, j: (i, j, qb + h)),
                  pl.BlockSpec((1, r, GLA_DK), lambda i, h, j: (i, j, kb + h)),
                  pl.BlockSpec((1, r, GLA_DV), lambda i, h, j: (i, j, vb + h)),
                  pl.BlockSpec((1, r, GLA_DV), lambda i, h, j: (i, j, gb + h)),
                  pl.BlockSpec((1, r, LANES), lambda i, h, j: (i, j, sb)),
                  pl.BlockSpec((GLA_RANK, GLA_DK), lambda i, h, j: (0, h)),
                  pl.BlockSpec((1, GLA_DK), lambda i, h, j: (0, h)),
                  pl.BlockSpec((1, GLA_DV), lambda i, h, j: (0, h)),
                  pl.BlockSpec((1, 1, GLA_DK, GLA_DV), lambda i, h, j: (i, h, 0, 0))],
        out_specs=[pl.BlockSpec((1, r, GLA_DV), lambda i, h, j: (i, j, h)),
                   pl.BlockSpec((1, 1, GLA_DK, GLA_DV), lambda i, h, j: (i, h, 0, 0))],
        out_shape=[jax.ShapeDtypeStruct((b, t, GLA_V), F32),
                   jax.ShapeDtypeStruct((b, GLA_HEADS, GLA_DK, GLA_DV), F32)],
        scratch_shapes=[pltpu.VMEM((GLA_DV, GLA_DK), F32)],
        name="gla_scan",
    )(y, y, y, y, y, lw['gla_wa2'], lw['gla_ba'].reshape(1, -1), lw['gla_norm_g'].reshape(1, -1), s0.astype(F32))


def _split3(x):
    hi = x.astype(BF16)
    r = x - hi.astype(F32)
    mid = r.astype(BF16)
    lo = (r - mid.astype(F32)).astype(BF16)
    return hi, mid, lo


def _gelu_tanh(x):
    return 0.5 * x * (1.0 + jnp.tanh(math.sqrt(2.0 / math.pi) * (x + 0.044715 * (x * x * x))))


def _compress_kernel(chk_ref, chv_ref, pek_ref, pev_ref, w1k_ref, w1v_ref, w2k_ref, w2v_ref, ok_ref, ov_ref):
    rows = CMP_STRIDE * HEAD_DIM
    for ch_ref, pe_ref, w1_ref, w2_ref, o_ref in ((chk_ref, pek_ref, w1k_ref, w2k_ref, ok_ref),
                                                   (chv_ref, pev_ref, w1v_ref, w2v_ref, ov_ref)):
        ch = ch_ref[0, 0].astype(BF16)
        nch = ch.shape[0]
        w1 = w1_ref[...]
        pe = jnp.broadcast_to(pe_ref[...], (8, pe_ref.shape[1])).astype(BF16)
        hid = _dot(pe, w1)[0:1, :]
        a0 = _dot(ch, w1[0:rows, :])
        a1 = _dot(ch, w1[rows:2 * rows, :])
        hid = hid + a0 + pltpu.roll(a1, nch - 1, axis=0)
        o_ref[0, 0] = _dot(_gelu_tanh(hid).astype(BF16), w2_ref[...])


def compress_pair(kc_r, vc_r, lw):
    b, kvh, length, hd = kc_r.shape
    nch = length // CMP_STRIDE
    chk = kc_r.reshape(b, kvh, nch, CMP_STRIDE * hd)
    chv = vc_r.reshape(b, kvh, nch, CMP_STRIDE * hd)
    ch_spec = pl.BlockSpec((1, 1, nch, CMP_STRIDE * hd), lambda i, j: (i, j, 0, 0))
    full = lambda a: pl.BlockSpec(a.shape, lambda i, j: (0,) * a.ndim)
    pek = lw['cmp_pe_k'].reshape(1, -1)
    pev = lw['cmp_pe_v'].reshape(1, -1)
    w1k, w1v = lw['cmp_w1_k'].astype(BF16), lw['cmp_w1_v'].astype(BF16)
    w2k, w2v = lw['cmp_w2_k'].astype(BF16), lw['cmp_w2_v'].astype(BF16)
    o_spec = pl.BlockSpec((1, 1, nch, hd), lambda i, j: (i, j, 0, 0))
    o_shape = jax.ShapeDtypeStruct((b, kvh, nch, hd), F32)
    return pl.pallas_call(
        _compress_kernel,
        grid=(b, kvh),
        in_specs=[ch_spec, ch_spec, full(pek), full(pev), full(w1k), full(w1v), full(w2k), full(w2v)],
        out_specs=[o_spec, o_spec],
        out_shape=[o_shape, o_shape],
        name="nsa_compress",
    )(chk, chv, pek, pev, w1k, w1v, w2k, w2v)


def _dot_tn(a, b):
    return lax.dot_general(a, b, (((0,), (0,)), ((), ())), preferred_element_type=F32)


def _nsa_t_kernel(q_ref, g_ref, kc_ref, vc_ref, ks_ref, vs_ref, kw_ref, vw_ref, bc_ref, bt_ref, ov_ref, o_ref,
                  acc_ref, *, nsb):
    tq = NSA_TILE
    grp = NSA_GROUP
    hd = HEAD_DIM
    wq = grp * tq
    ncp = kc_ref.shape[2]
    qi = pl.program_id(2)
    q0 = qi * tq
    q = q_ref[0] * (hd ** -0.5)
    q4 = jnp.concatenate([q[:, g * hd:(g + 1) * hd] for g in range(grp)], axis=0).astype(BF16)
    lane_q = lax.broadcasted_iota(jnp.int32, (1, tq), 1)
    qpos1 = q0 + lane_q
    qpos = jnp.concatenate([qpos1] * grp, axis=1)

    kc = kc_ref[0, 0].astype(BF16)
    vc = vc_ref[0, 0].astype(BF16)
    cend = CMP_STRIDE * lax.broadcasted_iota(jnp.int32, (ncp, 1), 0) + (CMP_BLOCK - 1)
    mask_c = cend <= qpos
    lc = jnp.where(mask_c, _dot_nt(kc, q4) + bc_ref[0, 0], NEG_INF)
    mc = jnp.max(lc, axis=0, keepdims=True)
    pc = jnp.where(mask_c, jnp.exp(lc - mc), 0.0)
    lsum = jnp.sum(pc, axis=0, keepdims=True)
    pc = pc / jnp.where(lsum > 0.0, lsum, 1.0)
    o_c = _dot_tn(vc, pc.astype(BF16))
    psum = pc[:, 0:tq]
    for g in range(1, grp):
        psum = psum + pc[:, g * tq:(g + 1) * tq]

    ov = ov_ref[...]
    hi, mid, lo = _split3(psum)
    imp = (_dot(ov, hi) + _dot(ov, mid) + _dot(ov, lo))[0:nsb, :]
    blk = lax.broadcasted_iota(jnp.int32, (nsb, 1), 0)
    cur = lax.shift_right_logical(qpos1, 6)
    forced = (blk == 0) | (blk == cur) | (blk == cur - 1)
    visible = blk * SLC_BLOCK <= qpos1
    score = jnp.where(forced, -NEG_INF, imp)
    score = jnp.where(visible, score, NEG_INF)
    rank = jnp.zeros((nsb, tq), jnp.int32)
    for i in range(nsb):
        si = score[i:i + 1, :]
        beats = (si > score) | ((si == score) & (blk > i))
        rank = rank + beats.astype(jnp.int32)
    sel1 = jnp.where(visible & (rank < SLC_TOPK), 1.0, 0.0).astype(BF16)
    sel1 = jnp.concatenate([sel1, jnp.zeros((LANES - nsb, tq), BF16)], axis=0)
    sel = jnp.concatenate([sel1] * grp, axis=1)

    neg = jnp.full((1, wq), NEG_INF, F32)
    zero = jnp.zeros((1, wq), F32)
    krow = lax.broadcasted_iota(jnp.int32, (tq, 1), 0)
    lane4 = jnp.concatenate([lane_q] * grp, axis=1)

    def slc_logits(kt, causal, valid):
        k0 = pl.multiple_of(kt * tq, tq)
        k_t = ks_ref[0, 0, pl.ds(k0, tq), :].astype(BF16)
        v_t = vs_ref[0, 0, pl.ds(k0, tq), :].astype(BF16)
        kblk = lax.shift_right_logical(k0 + krow, 6)
        expand = jnp.where(kblk == lax.broadcasted_iota(jnp.int32, (1, LANES), 1), 1.0, 0.0).astype(BF16)
        mask = _dot(expand, sel) > 0.5
        if causal:
            mask = mask & (krow <= lane4)
        if valid is not None:
            mask = mask & valid
        return jnp.where(mask, _dot_nt(k_t, q4) + bt_ref[0, jnp.minimum(qi - kt, 2)], NEG_INF), v_t

    def slc_pair(a, b, m, l):
        (sa, va), (sb, vb) = a, b
        m_new = jnp.maximum(m, jnp.maximum(jnp.max(sa, axis=0, keepdims=True), jnp.max(sb, axis=0, keepdims=True)))
        alpha = jnp.exp(m - m_new)
        pa, pb = jnp.exp(sa - m_new), jnp.exp(sb - m_new)
        l = alpha * l + jnp.sum(pa, axis=0, keepdims=True) + jnp.sum(pb, axis=0, keepdims=True)
        acc_ref[...] = alpha * acc_ref[...] + _dot_tn(va, pa.astype(BF16)) + _dot_tn(vb, pb.astype(BF16))
        return m_new, l

    acc_ref[...] = jnp.zeros_like(acc_ref)
    m_s, l_s = lax.fori_loop(
        0, lax.shift_right_logical(qi, 1),
        lambda j, c: slc_pair(slc_logits(2 * j, False, None), slc_logits(2 * j + 1, False, None), *c), (neg, zero))
    m_s, l_s = slc_pair(slc_logits(jnp.maximum(qi - 1, 0), False, (qi & 1) == 1), slc_logits(qi, True, None), m_s, l_s)
    o_s = acc_ref[...] / l_s

    nwt = WINDOW // tq
    s_w, v_w = [], []
    for dt in range(nwt, -1, -1):
        kt = qi - dt
        k0 = pl.multiple_of(jnp.maximum(kt, 0) * tq, tq)
        k_t = kw_ref[0, 0, pl.ds(k0, tq), :].astype(BF16)
        v_w.append(vw_ref[0, 0, pl.ds(k0, tq), :].astype(BF16))
        s = _dot_nt(k_t, q4) + bt_ref[0, min(dt, 2)]
        if dt == nwt:
            s = jnp.where((krow > lane4) & (kt >= 0), s, NEG_INF)
        elif dt == 0:
            s = jnp.where(krow <= lane4, s, NEG_INF)
        else:
            s = jnp.where(kt >= 0, s, NEG_INF)
        s_w.append(s)
    m_w = functools.reduce(jnp.maximum, [jnp.max(s, axis=0, keepdims=True) for s in s_w])
    p_w = [jnp.exp(s - m_w) for s in s_w]
    l_w = functools.reduce(jnp.add, [jnp.sum(p, axis=0, keepdims=True) for p in p_w])
    o_w = functools.reduce(jnp.add, [_dot_tn(v, p.astype(BF16)) for v, p in zip(v_w, p_w)]) / l_w

    gates = jax.nn.sigmoid(g_ref[0, 0])
    outs = []
    for g in range(grp):
        sl = slice(g * tq, (g + 1) * tq)
        outs.append(gates[:, 3 * g:3 * g + 1] * o_c[:, sl].T + gates[:, 3 * g + 1:3 * g + 2] * o_s[:, sl].T
                    + gates[:, 3 * g + 2:3 * g + 3] * o_w[:, sl].T)
    o_ref[0] = jnp.concatenate(outs, axis=-1)


def nsa_bias_tables(tbl, t, ncp):
    tq = NSA_TILE
    kvh, grp = tbl.shape[1], tbl.shape[2]
    dist_c = jnp.arange(t)[None, :] - (CMP_STRIDE * jnp.arange(ncp) + CMP_BLOCK - 1)[:, None]
    bc = tbl[rel_bucket(dist_c)]
    bc = bc.reshape(ncp, t // tq, tq, kvh, grp).transpose(3, 1, 0, 4, 2).reshape(kvh, t // tq, ncp, grp * tq)
    cr = jnp.arange(tq)[None, :] - jnp.arange(tq)[:, None]
    tiles = jnp.stack([cr, cr + tq, jnp.full_like(cr, REL_MAX_DIST)])
    bt = tbl[rel_bucket(tiles)]
    bt = bt.transpose(3, 0, 1, 4, 2).reshape(kvh, 3, tq, grp * tq)
    return bc, bt


def nsa_prompt_pallas(y, q_off, ngt, kc_r, vc_r, ks_r, vs_r, kw_r, vw_r, lw, bias_tables):
    b, t, _ = y.shape
    kvh, grp, hd, tq = NSA_KV_HEADS, NSA_GROUP, HEAD_DIM, NSA_TILE
    qb = q_off // (grp * hd)
    kc, vc = compress_pair(kc_r, vc_r, lw)
    ncp = kc.shape[2]
    nsb = t // SLC_BLOCK
    bc, bt = bias_tables
    gates = ngt.reshape(b, t, kvh, grp * 3).transpose(0, 2, 1, 3)
    ci = jnp.arange(ncp)[None, :] * CMP_STRIDE
    sj = jnp.arange(LANES)[:, None] * SLC_BLOCK
    ov = ((ci < sj + SLC_BLOCK) & (ci + CMP_BLOCK > sj) & (jnp.arange(ncp)[None, :] < ncp - 1)
          & (jnp.arange(LANES)[:, None] < nsb)).astype(BF16)
    row_spec = pl.BlockSpec((1, 1, t, hd), lambda i, j, k: (i, j, 0, 0))
    cmp_spec = pl.BlockSpec((1, 1, ncp, hd), lambda i, j, k: (i, j, 0, 0))
    return pl.pallas_call(
        functools.partial(_nsa_t_kernel, nsb=nsb),
        grid=(b, kvh, t // tq),
        in_specs=[pl.BlockSpec((1, tq, grp * hd), lambda i, j, k: (i, k, qb + j)),
                  pl.BlockSpec((1, 1, tq, grp * 3), lambda i, j, k: (i, j, k, 0)),
                  cmp_spec, cmp_spec, row_spec, row_spec, row_spec, row_spec,
                  pl.BlockSpec((1, 1, ncp, grp * tq), lambda i, j, k: (j, k, 0, 0)),
                  pl.BlockSpec((1, 3, tq, grp * tq), lambda i, j, k: (j, 0, 0, 0)),
                  pl.BlockSpec(ov.shape, lambda i, j, k: (0, 0))],
        out_specs=pl.BlockSpec((1, tq, grp * hd), lambda i, j, k: (i, k, j)),
        out_shape=jax.ShapeDtypeStruct((b, t, kvh * grp * hd), F32),
        scratch_shapes=[pltpu.VMEM((hd, grp * tq), F32)],
        compiler_params=pltpu.CompilerParams(vmem_limit_bytes=VMEM_LIMIT),
        name="nsa_prompt",
    )(y, gates, kc, vc, ks_r, vs_r, kw_r, vw_r, bc, bt, ov)


def layer_norm(x, g, b):
    xf = x.astype(jnp.float32)
    mu = xf.mean(-1, keepdims=True)
    var = jnp.square(xf - mu).mean(-1, keepdims=True)
    return ((xf - mu) * lax.rsqrt(var + LN_EPS) * g + b).astype(x.dtype)


def masked_softmax(logits, mask):
    p = jax.nn.softmax(jnp.where(mask, logits.astype(jnp.float32), NEG_INF), axis=-1)
    return jnp.where(mask, p, 0.0)


def rel_bucket(dist):
    n = jnp.maximum(dist, 0)
    exact = REL_BUCKETS // 2
    big = exact + (jnp.log(jnp.maximum(n, 1).astype(jnp.float32) / exact)
                   / math.log(REL_MAX_DIST / exact) * (REL_BUCKETS - exact)).astype(jnp.int32)
    return jnp.where(n < exact, n, jnp.minimum(big, REL_BUCKETS - 1))


def gla_scan(q, k, v, log_a, s0):
    b, t = q.shape[:2]
    c = math.gcd(t, GLA_CHUNK)
    nc = t // c
    causal = jnp.tril(jnp.ones((c, c), dtype=bool))

    def to_chunks(a):
        return a.reshape(b, nc, c, *a.shape[2:]).swapaxes(0, 1)

    def step(s, inp):
        qc, kc, vc, ac = inp
        cum = jnp.cumsum(ac, axis=1)
        inter = jnp.einsum('bthk,bhkv->bthv', qc * jnp.exp(cum), s)
        diff = cum[:, :, None] - cum[:, None, :]
        decay = jnp.exp(jnp.where(causal[None, :, :, None, None], diff, -jnp.inf))
        att = jnp.einsum('bthk,bshk,btshk->bhts', qc, kc, decay)
        intra = jnp.einsum('bhts,bshv->bthv', att, vc)
        last = cum[:, -1]
        s = jnp.exp(last)[..., None] * s + jnp.einsum('bshk,bshv->bhkv', kc * jnp.exp(last[:, None] - cum), vc)
        return s, inter + intra

    s_fin, o = lax.scan(step, s0, (to_chunks(q), to_chunks(k), to_chunks(v), to_chunks(log_a)))
    return o.swapaxes(0, 1).reshape(b, t, *v.shape[2:]), s_fin


def gla_output(o, gate, norm_g):
    mu = o.mean(-1, keepdims=True)
    var = jnp.square(o - mu).mean(-1, keepdims=True)
    on = ((o - mu) * lax.rsqrt(var + LN_EPS)).reshape(*o.shape[:2], -1)
    return on * norm_g * jax.nn.silu(gate.astype(jnp.float32))


def short_conv(b_gate, c_gate, x_in, prev, w):
    u = c_gate * x_in
    up = jnp.concatenate([prev, u], axis=1)
    t = u.shape[1]
    y = up[:, 0:t] * w[0]
    for j in range(1, CONV_W):
        y = y + up[:, j:j + t] * w[j]
    return b_gate * y, up[:, t:]


def compress(k, pe, w1, w2):
    b, kvh, length, hd = k.shape
    n_chunks = length // CMP_STRIDE
    pieces = CMP_BLOCK // CMP_STRIDE
    n_blocks = n_chunks - pieces + 1
    ch = k.reshape(b, kvh, n_chunks, CMP_STRIDE * hd)
    rows = CMP_STRIDE * hd
    hid = pe.reshape(-1) @ w1
    for r in range(pieces):
        hid = hid + (ch @ w1[r * rows:(r + 1) * rows])[:, :, r:r + n_blocks]
    return jax.nn.gelu(hid) @ w2


def slc_overlap(n_cmp, n_slc):
    start = jnp.arange(n_cmp) * CMP_STRIDE
    blk = jnp.arange(n_slc) * SLC_BLOCK
    return ((start[:, None] < blk[None, :] + SLC_BLOCK) & (start[:, None] + CMP_BLOCK > blk[None, :])).astype(jnp.float32)


def nsa_attend(q, qpos, gates, kc, vc, overlap, gather_slc, kw, vw, wpos, tbl):
    b, nq, kvh, grp, hd = q.shape
    scale = hd ** -0.5
    f32 = jnp.float32
    hi = jnp.arange(kvh)[:, None, None]
    cmp_end = CMP_STRIDE * jnp.arange(kc.shape[2]) + CMP_BLOCK - 1
    dist_c = qpos[:, None] - cmp_end[None, :]
    lc = jnp.einsum('bqhgd,bhnd->bqhgn', q, kc).astype(f32) * scale + jnp.moveaxis(tbl[rel_bucket(dist_c)], 1, -1)
    pc = masked_softmax(lc, (dist_c >= 0)[:, None, None, :])
    o_c = jnp.einsum('bqhgn,bhnd->bqhgd', pc, vc)
    imp = jnp.einsum('bqhn,ns->bqhs', pc.sum(3), overlap)
    blk = jnp.arange(overlap.shape[1])
    cur = (qpos // SLC_BLOCK)[:, None]
    forced = (blk[None] == 0) | (blk[None] == cur) | (blk[None] == cur - 1)
    visible = blk[None] * SLC_BLOCK <= qpos[:, None]
    score = jnp.where(forced[:, None], -NEG_INF, imp)
    score = jnp.where(visible[:, None], score, NEG_INF)
    top_s, idx = lax.top_k(score, min(SLC_TOPK, overlap.shape[1]))
    ks, vs = gather_slc(idx)
    kpos = idx[..., None] * SLC_BLOCK + jnp.arange(SLC_BLOCK)
    dist_s = qpos[None, :, None, None, None] - kpos
    mask_s = (dist_s >= 0) & (top_s > 0.5 * NEG_INF)[..., None]
    ls = jnp.einsum('bqhgd,bqhnsd->bqhgns', q, ks).astype(f32) * scale + jnp.moveaxis(tbl[rel_bucket(dist_s), hi], -1, 3)
    ps = masked_softmax(ls.reshape(b, nq, kvh, grp, -1), mask_s.reshape(b, nq, kvh, 1, -1))
    o_s = jnp.einsum('bqhgm,bqhmd->bqhgd', ps, vs.reshape(b, nq, kvh, -1, hd))
    dist_w = qpos[:, None] - wpos[None, :]
    mask_w = (dist_w >= 0) & (dist_w < WINDOW) & (wpos >= 0)[None, :]
    lwin = jnp.einsum('bqhgd,bhwd->bqhgw', q, kw).astype(f32) * scale + jnp.moveaxis(tbl[rel_bucket(dist_w)], 1, -1)
    pw = masked_softmax(lwin, mask_w[:, None, None, :])
    o_w = jnp.einsum('bqhgw,bhwd->bqhgd', pw, vw)
    o = gates[..., 0:1] * o_c + gates[..., 1:2] * o_s + gates[..., 2:3] * o_w
    return o.astype(q.dtype)


def nsa_prompt(q, gates, kc_r, vc_r, ks_r, vs_r, kw_r, vw_r, lw, tbl):
    b, s = q.shape[:2]
    kc = compress(kc_r, lw['cmp_pe_k'], lw['cmp_w1_k'], lw['cmp_w2_k'])
    vc = compress(vc_r, lw['cmp_pe_v'], lw['cmp_w1_v'], lw['cmp_w2_v'])
    nsb = s // SLC_BLOCK
    ks_blk = ks_r.reshape(b, NSA_KV_HEADS, nsb, SLC_BLOCK, HEAD_DIM)
    vs_blk = vs_r.reshape(b, NSA_KV_HEADS, nsb, SLC_BLOCK, HEAD_DIM)
    bi = jnp.arange(b)[:, None, None, None]
    hi = jnp.arange(NSA_KV_HEADS)[None, None, :, None]

    def gather_slc(idx):
        return ks_blk[bi, hi, idx], vs_blk[bi, hi, idx]

    overlap = slc_overlap(kc.shape[2], nsb)
    kw_pad = jnp.pad(kw_r, ((0, 0), (0, 0), (WINDOW, 0), (0, 0)))
    vw_pad = jnp.pad(vw_r, ((0, 0), (0, 0), (WINDOW, 0), (0, 0)))
    n_chunks = s // NSA_QCHUNK
    span = WINDOW + NSA_QCHUNK

    def chunks(a):
        return a.reshape(b, n_chunks, NSA_QCHUNK, *a.shape[2:]).swapaxes(0, 1)

    def body(inp):
        qc, gc, c0 = inp
        return nsa_attend(qc, c0 + jnp.arange(NSA_QCHUNK), gc, kc, vc, overlap, gather_slc,
                          lax.dynamic_slice_in_dim(kw_pad, c0, span, axis=2),
                          lax.dynamic_slice_in_dim(vw_pad, c0, span, axis=2),
                          c0 - WINDOW + jnp.arange(span), tbl)

    o = lax.map(body, (chunks(q), chunks(gates), jnp.arange(n_chunks) * NSA_QCHUNK))
    o = o.swapaxes(0, 1).reshape(b, s, *q.shape[2:])
    wb = min(WINDOW, s)
    return o, kw_r[:, :, s - wb:], vw_r[:, :, s - wb:]


def nsa_sample(q, gates, kc_r, vc_r, ks_r, vs_r, kw_r, vw_r, lw, tbl, layer, pool_ck, pool_cv, pool_sk, pool_sv, win_k, win_v, page_table):
    b, t = q.shape[:2]
    past = page_table.shape[1] * PAGE_SIZE
    t_cmp = -(-t // CMP_STRIDE) * CMP_STRIDE
    nnb = -(-t // SLC_BLOCK)
    npb = past // SLC_BLOCK
    bpp = PAGE_SIZE // SLC_BLOCK

    def past_rows(pool):
        g = pool[layer, page_table]
        return g.transpose(0, 2, 1, 3, 4).reshape(b, NSA_KV_HEADS, past, HEAD_DIM)

    def pad_t(a, n):
        return jnp.pad(a, ((0, 0), (0, 0), (0, n - a.shape[2]), (0, 0)))

    kc = compress(jnp.concatenate([past_rows(pool_ck), pad_t(kc_r, t_cmp)], axis=2), lw['cmp_pe_k'], lw['cmp_w1_k'], lw['cmp_w2_k'])
    vc = compress(jnp.concatenate([past_rows(pool_cv), pad_t(vc_r, t_cmp)], axis=2), lw['cmp_pe_v'], lw['cmp_w1_v'], lw['cmp_w2_v'])
    new_ks = pad_t(ks_r, nnb * SLC_BLOCK).reshape(b, NSA_KV_HEADS, nnb, SLC_BLOCK, HEAD_DIM)
    new_vs = pad_t(vs_r, nnb * SLC_BLOCK).reshape(b, NSA_KV_HEADS, nnb, SLC_BLOCK, HEAD_DIM)
    bi = jnp.arange(b)[:, None, None, None]
    hi = jnp.arange(NSA_KV_HEADS)[None, None, :, None]
    offs = jnp.arange(SLC_BLOCK)

    def gather_slc(idx):
        in_past = (idx < npb)[..., None, None]
        pidx = jnp.minimum(idx, npb - 1)
        page = page_table[bi, pidx // bpp][..., None]
        rows = ((pidx % bpp) * SLC_BLOCK)[..., None] + offs
        nidx = jnp.clip(idx - npb, 0, nnb - 1)
        ks = jnp.where(in_past, pool_sk[layer, page, hi[..., None], rows], new_ks[bi, hi, nidx])
        vs = jnp.where(in_past, pool_sv[layer, page, hi[..., None], rows], new_vs[bi, hi, nidx])
        return ks, vs

    kw = jnp.concatenate([win_k.astype(kw_r.dtype), kw_r], axis=2)
    vw = jnp.concatenate([win_v.astype(vw_r.dtype), vw_r], axis=2)
    wb = win_k.shape[2]
    o = nsa_attend(q, past + jnp.arange(t), gates, kc, vc, slc_overlap(kc.shape[2], npb + nnb), gather_slc,
                   kw, vw, past - wb + jnp.arange(wb + t), tbl)
    return o, kw[:, :, -wb:], vw[:, :, -wb:]


def token_mixer(h, lw, gla_s0, conv_prev, nsa_fn):
    b, t, _ = h.shape
    f32 = jnp.float32
    (gq, gk, gv, gg, ga, cb, cc, cx, nq, nkc, nvc, nks, nvs, nkw, nvw, ngt, mgt) = jnp.split(
        in_proj(h, lw['w_in']), np.cumsum(IN_SIZES)[:-1].tolist(), axis=-1)
    q = gq.reshape(b, t, GLA_HEADS, GLA_DK).astype(f32) * GLA_DK ** -0.5
    k = gk.reshape(b, t, GLA_HEADS, GLA_DK).astype(f32)
    v = gv.reshape(b, t, GLA_HEADS, GLA_DV).astype(f32)
    log_a = (jax.nn.log_sigmoid((ga @ lw['gla_wa2'] + lw['gla_ba']).astype(f32)) / GLA_TAU).reshape(b, t, GLA_HEADS, GLA_DK)
    o, s_fin = gla_scan(q, k, v, log_a, gla_s0.astype(f32))
    o_gla = gla_output(o, gg, lw['gla_norm_g']).astype(h.dtype)
    o_conv, conv_state = short_conv(cb, cc, cx, conv_prev.astype(h.dtype), lw['conv_w'])

    def rows(a):
        return a.reshape(b, t, NSA_KV_HEADS, HEAD_DIM).transpose(0, 2, 1, 3)
    kc_r, vc_r, ks_r, vs_r, kw_r, vw_r = [rows(a) for a in (nkc, nvc, nks, nvs, nkw, nvw)]
    q_n = nq.reshape(b, t, NSA_KV_HEADS, NSA_GROUP, HEAD_DIM)
    g_n = jax.nn.sigmoid(ngt).reshape(b, t, NSA_KV_HEADS, NSA_GROUP, 3)
    o_n, win_k, win_v = nsa_fn(q_n, g_n, kc_r, vc_r, ks_r, vs_r, kw_r, vw_r)
    s_a, s_b, s_c = jnp.split(jax.nn.sigmoid(mgt), 3, axis=-1)
    merged = (s_a * (o_gla @ lw['w_br_gla']) + s_b * (o_conv @ lw['w_br_conv'])
              + s_c * (o_n.reshape(b, t, NSA_Q) @ lw['w_br_nsa']))
    return merged @ lw['w_o'], (kc_r, vc_r, ks_r, vs_r, win_k, win_v, s_fin.astype(h.dtype), conv_state)


def moe_dispatch(hf, idx, wts, wg, wu, wd):
    n, d = hf.shape
    a = n * TOP_K
    fe = idx.reshape(-1)
    ft = jnp.arange(a) // TOP_K
    fw = wts.reshape(-1)
    order = jnp.argsort(fe)
    se, st, sw = fe[order], ft[order], fw[order]
    counts = jnp.zeros((N_EXPERTS,), jnp.int32).at[fe].add(1)
    starts = jnp.cumsum(counts) - counts
    padded = (counts + MOE_BLOCK - 1) // MOE_BLOCK * MOE_BLOCK
    pend = jnp.cumsum(padded)
    pstart = pend - padded
    dest = pstart[se] + jnp.arange(a) - starts[se]
    nb = -(-a // MOE_BLOCK) + N_EXPERTS
    rows = nb * MOE_BLOCK
    row_tok = jnp.full((rows,), n, jnp.int32).at[dest].set(st)
    row_w = jnp.zeros((rows,), hf.dtype).at[dest].set(sw)
    blk_e = jnp.minimum(jnp.searchsorted(pend, jnp.arange(nb) * MOE_BLOCK, side='right'), N_EXPERTS - 1)
    h_pad = jnp.concatenate([hf, jnp.zeros((1, d), hf.dtype)], axis=0)

    def expert_block(inp):
        tok, e = inp
        xb = h_pad[tok]
        return (jax.nn.silu(xb @ wg[e]) * (xb @ wu[e])) @ wd[e]

    y = lax.map(expert_block, (row_tok.reshape(nb, MOE_BLOCK), blk_e))
    return jnp.zeros((n + 1, d), y.dtype).at[row_tok].add(y.reshape(rows, d) * row_w[:, None])[:n]


def moe(h, lw):
    b, t, d = h.shape
    n = b * t
    hf = h.reshape(n, d)
    s = jax.nn.sigmoid((hf @ lw['w_router']).astype(jnp.float32))
    sel = s + lw['router_bias'].astype(jnp.float32)
    per = N_EXPERTS // N_GROUPS
    gscore = lax.top_k(sel.reshape(n, N_GROUPS, per), 2)[0].sum(-1)
    gidx = lax.top_k(gscore, TOPK_GROUPS)[1]
    gmask = jax.nn.one_hot(gidx, N_GROUPS).sum(1) > 0
    sel = jnp.where(jnp.repeat(gmask, per, axis=1), sel, NEG_INF)
    idx = lax.top_k(sel, TOP_K)[1]
    w = jnp.take_along_axis(s, idx, axis=1)
    w = w / w.sum(-1, keepdims=True) * ROUTED_SCALE
    routed = moe_dispatch(hf, idx, w.astype(h.dtype), lw['w_exp_gate'], lw['w_exp_up'], lw['w_exp_down'])
    shared = (jax.nn.silu(hf @ lw['w_sh_gate']) * (hf @ lw['w_sh_up'])) @ lw['w_sh_down']
    return (routed + shared).reshape(b, t, d)


def block(x, c, lw, mixer):
    mod = (jax.nn.silu(c) @ lw['w_ada'] + lw['b_ada'])[:, None, :]
    sh1, sc1, g1, sh2, sc2, g2 = jnp.split(mod, 6, axis=-1)
    mix, state = mixer(x * (1 + sc1) + sh1)
    x = layer_norm(DN_ALPHA * x + g1 * mix, lw['ln1_g'], lw['ln1_b'])
    x = layer_norm(DN_ALPHA * x + g2 * moe(x * (1 + sc2) + sh2, lw), lw['ln2_g'], lw['ln2_b'])
    return x, state


def kernel(x_prompt, x_sample, cache_cmp_k, cache_cmp_v, cache_slc_k, cache_slc_v, cache_win_k, cache_win_v,
           state_gla, state_conv, page_table, c_prompt, c_sample, w_ada, b_ada, w_in, gla_wa2, gla_ba,
           gla_norm_g, conv_w, cmp_pe_k, cmp_pe_v, cmp_w1_k, cmp_w2_k, cmp_w1_v, cmp_w2_v, rel_bias,
           w_br_gla, w_br_conv, w_br_nsa, w_o, ln1_g, ln1_b, ln2_g, ln2_b, w_router, router_bias,
           w_exp_gate, w_exp_up, w_exp_down, w_sh_gate, w_sh_up, w_sh_down):
    tbl = rel_bias.reshape(REL_BUCKETS, NSA_KV_HEADS, NSA_GROUP)
    xp, xs = x_prompt, x_sample
    p_states, s_states = [], []
    for l in range(DEPTH):
        lw = {
            'w_ada': w_ada[l], 'b_ada': b_ada[l], 'w_in': w_in[l], 'gla_wa2': gla_wa2[l], 'gla_ba': gla_ba[l],
            'gla_norm_g': gla_norm_g[l], 'conv_w': conv_w[l], 'cmp_pe_k': cmp_pe_k[l], 'cmp_pe_v': cmp_pe_v[l],
            'cmp_w1_k': cmp_w1_k[l], 'cmp_w2_k': cmp_w2_k[l], 'cmp_w1_v': cmp_w1_v[l], 'cmp_w2_v': cmp_w2_v[l],
            'w_br_gla': w_br_gla[l], 'w_br_conv': w_br_conv[l], 'w_br_nsa': w_br_nsa[l], 'w_o': w_o[l],
            'ln1_g': ln1_g[l], 'ln1_b': ln1_b[l], 'ln2_g': ln2_g[l], 'ln2_b': ln2_b[l],
            'w_router': w_router[l], 'router_bias': router_bias[l], 'w_exp_gate': w_exp_gate[l],
            'w_exp_up': w_exp_up[l], 'w_exp_down': w_exp_down[l], 'w_sh_gate': w_sh_gate[l],
            'w_sh_up': w_sh_up[l], 'w_sh_down': w_sh_down[l],
        }
        prompt_mixer = functools.partial(
            token_mixer, lw=lw,
            gla_s0=jnp.zeros((xp.shape[0], GLA_HEADS, GLA_DK, GLA_DV), jnp.float32),
            conv_prev=jnp.zeros((xp.shape[0], CONV_W - 1, CONV_DIM), xp.dtype),
            nsa_fn=functools.partial(nsa_prompt, lw=lw, tbl=tbl))
        sample_mixer = functools.partial(
            token_mixer, lw=lw, gla_s0=state_gla[l], conv_prev=state_conv[l],
            nsa_fn=functools.partial(nsa_sample, lw=lw, tbl=tbl, layer=l, pool_ck=cache_cmp_k, pool_cv=cache_cmp_v,
                                     pool_sk=cache_slc_k, pool_sv=cache_slc_v, win_k=cache_win_k[l],
                                     win_v=cache_win_v[l], page_table=page_table))
        xp, st = block(xp, c_prompt, lw, prompt_mixer)
        p_states.append(st)
        xs, st = block(xs, c_sample, lw, sample_mixer)
        s_states.append(st)
    p_out = [jnp.stack(z) for z in zip(*p_states)]
    s_out = [jnp.stack(z) for z in zip(*s_states)]
    return (xp, xs, *p_out, *s_out)


MERGE_ROWS = 256
CONV_HALO = 8


def _layer_norm_rows(z, g, b):
    mu = jnp.mean(z, axis=-1, keepdims=True)
    var = jnp.mean(jnp.square(z - mu), axis=-1, keepdims=True)
    return (z - mu) * lax.rsqrt(var + LN_EPS) * g + b


def _rank_rows(v):
    r = v.shape[0]
    rid = lax.broadcasted_iota(jnp.int32, (r, 1), 0)
    rank = jnp.zeros(v.shape, jnp.int32)
    for j in range(r):
        vj = v[j:j + 1, :]
        rank = rank + ((vj > v) | ((vj == v) & (rid > j))).astype(jnp.int32)
    return rank


def _route_cols(s, bias):
    n = s.shape[1]
    per = N_EXPERTS // N_GROUPS
    sel = s + bias
    gs = []
    for g in range(N_GROUPS):
        sg = sel[g * per:(g + 1) * per, :]
        gs.append(jnp.sum(jnp.where(_rank_rows(sg) < 2, sg, 0.0), axis=0, keepdims=True))
    gkeep = _rank_rows(jnp.concatenate(gs, axis=0)) < TOPK_GROUPS
    keep = jnp.concatenate([jnp.broadcast_to(gkeep[g:g + 1, :], (per, n)) for g in range(N_GROUPS)], axis=0)
    rank = _rank_rows(jnp.where(keep, sel, NEG_INF))
    eid = lax.broadcasted_iota(jnp.int32, (N_EXPERTS, 1), 0)
    ids, ws = [], []
    for k in range(TOP_K):
        hit = rank == k
        ids.append(jnp.sum(jnp.where(hit, eid, 0), axis=0, keepdims=True))
        ws.append(jnp.sum(jnp.where(hit, s, 0.0), axis=0, keepdims=True))
    w = jnp.concatenate(ws, axis=0)
    return jnp.concatenate(ids, axis=0), w / jnp.sum(w, axis=0, keepdims=True) * ROUTED_SCALE


def _merge_kernel(x_ref, cb_ref, cc_ref, cx_ref, ccp_ref, cxp_ref, prev_ref, ma_ref, mb_ref, mc_ref, og_ref, on_ref,
                  cw_ref, wg_ref, wc_ref, wn_ref, wo_ref, g1_ref, sc2_ref, sh2_ref, lng_ref, lnb_ref, wr_ref, rb_ref,
                  x1_ref, h2_ref, idx_ref, rw_ref, cst_ref):
    j = pl.program_id(1)
    tm = x_ref.shape[1]
    u = cc_ref[0] * cx_ref[0]
    halo = jnp.where(j == 0, prev_ref[0], ccp_ref[0] * cxp_ref[0])
    p1 = halo[CONV_HALO - 1:CONV_HALO, :]
    p2 = halo[CONV_HALO - 2:CONV_HALO - 1, :]
    rid = lax.broadcasted_iota(jnp.int32, (tm, 1), 0)
    u1 = jnp.where(rid == 0, p1, pltpu.roll(u, 1, axis=0))
    u2 = jnp.where(rid == 0, p2, jnp.where(rid == 1, p1, pltpu.roll(u, 2, axis=0)))
    cw = cw_ref[...]
    o_conv = cb_ref[0] * (u2 * cw[0:1, :] + u1 * cw[1:2, :] + u * cw[2:3, :])
    cst_ref[0] = u[tm - CONV_HALO:tm, :]
    merged = (jax.nn.sigmoid(ma_ref[0]) * _dot(og_ref[0].astype(BF16), wg_ref[...])
              + jax.nn.sigmoid(mb_ref[0]) * _dot(o_conv.astype(BF16), wc_ref[...])
              + jax.nn.sigmoid(mc_ref[0]) * _dot(on_ref[0].astype(BF16), wn_ref[...]))
    mix = _dot(merged.astype(BF16), wo_ref[...])
    x1 = _layer_norm_rows(DN_ALPHA * x_ref[0] + g1_ref[0] * mix, lng_ref[...], lnb_ref[...])
    x1_ref[0] = x1
    h2 = x1 * (1.0 + sc2_ref[0]) + sh2_ref[0]
    h2_ref[0] = h2.astype(h2_ref.dtype)
    s = jax.nn.sigmoid(lax.dot_general(wr_ref[...], h2, (((1,), (1,)), ((), ())),
                                       precision=lax.Precision.HIGHEST, preferred_element_type=F32))
    idx_ref[0], rw_ref[0] = _route_cols(s, rb_ref[...])


def merge_pallas(x, y, o_gla, o_n, conv_prev, g1, sc2, sh2, lw):
    b, t, d = x.shape
    tm = math.gcd(t, MERGE_ROWS)
    halo = CONV_HALO
    hb = tm // halo
    prev = jnp.concatenate([jnp.zeros((b, halo - (CONV_W - 1), d), F32), conv_prev.astype(F32)], axis=1)
    col = lambda name, k=0: PROJ_OFF[name] // d + k
    tile = lambda cb_: pl.BlockSpec((1, tm, d), lambda i, j: (i, j, cb_))
    halo_spec = lambda cb_: pl.BlockSpec((1, halo, d), lambda i, j: (i, jnp.maximum(j * hb - 1, 0), cb_))
    vec = pl.BlockSpec((1, 1, d), lambda i, j: (i, 0, 0))
    const = lambda a: pl.BlockSpec(a.shape, lambda i, j: (0,) * a.ndim)
    wts = [lw['w_br_gla'].astype(BF16), lw['w_br_conv'].astype(BF16), lw['w_br_nsa'].astype(BF16), lw['w_o'].astype(BF16)]
    lng, lnb = lw['ln1_g'].reshape(1, d), lw['ln1_b'].reshape(1, d)
    wr, rb = lw['w_router'].T, lw['router_bias'].reshape(N_EXPERTS, 1).astype(F32)
    topk = pl.BlockSpec((1, TOP_K, tm), lambda i, j: (i, 0, j))
    x1, h2, idx, rw, cst = pl.pallas_call(
        _merge_kernel,
        grid=(b, t // tm),
        in_specs=[tile(0), tile(col('cb')), tile(col('cc')), tile(col('cx')), halo_spec(col('cc')), halo_spec(col('cx')),
                  pl.BlockSpec((1, halo, d), lambda i, j: (i, 0, 0)),
                  tile(col('mgt', 0)), tile(col('mgt', 1)), tile(col('mgt', 2)), tile(0), tile(0),
                  const(lw['conv_w'])] + [const(w) for w in wts] + [vec, vec, vec, const(lng), const(lnb),
                                                                    const(wr), const(rb)],
        out_specs=[tile(0), tile(0), topk, topk, pl.BlockSpec((1, halo, d), lambda i, j: (i, 0, 0))],
        out_shape=[jax.ShapeDtypeStruct((b, t, d), F32), jax.ShapeDtypeStruct((b, t, d), BF16 if tm % 16 == 0 else F32),
                   jax.ShapeDtypeStruct((b, TOP_K, t), jnp.int32), jax.ShapeDtypeStruct((b, TOP_K, t), F32),
                   jax.ShapeDtypeStruct((b, halo, d), F32)],
        compiler_params=pltpu.CompilerParams(vmem_limit_bytes=VMEM_LIMIT),
        name="merge_ln1",
    )(x, y, y, y, y, y, prev, y, y, y, o_gla, o_n, lw['conv_w'], *wts,
      g1.reshape(b, 1, d), sc2.reshape(b, 1, d), sh2.reshape(b, 1, d), lng, lnb, wr, rb)
    to_rows = lambda a: a.transpose(0, 2, 1).reshape(b * t, TOP_K)
    return x1, h2, to_rows(idx), to_rows(rw), cst[:, halo - (CONV_W - 1):]


def _expert_act(xb, wg, wu):
    hg = _dot(xb, wg)
    return hg * jax.nn.sigmoid(hg) * _dot(xb, wu)


def _grouped_expert_kernel(be_ref, x_ref, wg_ref, wu_ref, wd_ref, o_ref):
    act = _expert_act(x_ref[...], wg_ref[0], wu_ref[0])
    o_ref[...] = _dot(act.astype(BF16), wd_ref[0]).astype(o_ref.dtype)


def moe_routed_sorted(h2, idx, w, lw):
    n, d = h2.shape
    a = n * TOP_K
    onehot = idx[:, :, None] == jnp.arange(N_EXPERTS, dtype=jnp.int32)[None, None, :]
    hit = onehot.any(axis=1).astype(F32)
    ct = math.gcd(n, 512)
    hit_t = hit.reshape(n // ct, ct, N_EXPERTS)
    within = jnp.einsum('ts,nse->nte', jnp.tril(jnp.ones((ct, ct), F32), -1), hit_t)
    tile_tot = hit_t.sum(1)
    before = jnp.cumsum(tile_tot, axis=0) - tile_tot
    pos = (within + before[:, None, :]).reshape(n, N_EXPERTS).astype(jnp.int32)
    counts = tile_tot.sum(0).astype(jnp.int32)
    padded = (counts + MOE_BLOCK - 1) // MOE_BLOCK * MOE_BLOCK
    pend = jnp.cumsum(padded)
    pstart = pend - padded
    slot = jnp.sum(jnp.where(onehot, (pos + pstart[None, :])[:, None, :], 0), axis=-1)
    nb = -(-a // MOE_BLOCK) + N_EXPERTS
    rows = nb * MOE_BLOCK
    row_tok = jnp.zeros((rows,), jnp.int32).at[slot.reshape(-1)].set(jnp.arange(a, dtype=jnp.int32) // TOP_K)
    blk_e = jnp.minimum(jnp.searchsorted(pend, jnp.arange(nb) * MOE_BLOCK, side='right'), N_EXPERTS - 1).astype(jnp.int32)
    xs = h2.astype(BF16)[row_tok]
    wg, wu, wd = lw['w_exp_gate'].astype(BF16), lw['w_exp_up'].astype(BF16), lw['w_exp_down'].astype(BF16)
    hdim = wg.shape[2]
    y = pl.pallas_call(
        _grouped_expert_kernel,
        grid_spec=pltpu.PrefetchScalarGridSpec(
            num_scalar_prefetch=1,
            grid=(nb,),
            in_specs=[pl.BlockSpec((MOE_BLOCK, d), lambda i, be: (i, 0)),
                      pl.BlockSpec((1, d, hdim), lambda i, be: (be[i], 0, 0)),
                      pl.BlockSpec((1, d, hdim), lambda i, be: (be[i], 0, 0)),
                      pl.BlockSpec((1, hdim, d), lambda i, be: (be[i], 0, 0))],
            out_specs=pl.BlockSpec((MOE_BLOCK, d), lambda i, be: (i, 0))),
        out_shape=jax.ShapeDtypeStruct((rows, d), BF16),
        name="moe_grouped",
    )(blk_e, xs, wg, wu, wd)
    return y[slot.T.reshape(-1)].reshape(TOP_K, n, d), w


def _dense_expert_kernel(x_ref, wt_ref, wg_ref, wu_ref, wd_ref, o_ref):
    @pl.when(pl.program_id(0) == 0)
    def _():
        o_ref[...] = jnp.zeros_like(o_ref)
    act = _expert_act(x_ref[...].astype(BF16), wg_ref[0].astype(BF16), wu_ref[0].astype(BF16))
    o_ref[...] += _dot((act * wt_ref[0]).astype(BF16), wd_ref[0].astype(BF16))


def moe_routed_dense(h2, idx, w, lw):
    n, d = h2.shape
    wdense = jnp.zeros((n, N_EXPERTS), F32).at[jnp.arange(n)[:, None], idx].add(w)
    wt = wdense.T.reshape(N_EXPERTS, n, 1)
    hdim = lw['w_exp_gate'].shape[2]
    routed = pl.pallas_call(
        _dense_expert_kernel,
        grid=(N_EXPERTS,),
        in_specs=[pl.BlockSpec((n, d), lambda e: (0, 0)),
                  pl.BlockSpec((1, n, 1), lambda e: (e, 0, 0)),
                  pl.BlockSpec((1, d, hdim), lambda e: (e, 0, 0)),
                  pl.BlockSpec((1, d, hdim), lambda e: (e, 0, 0)),
                  pl.BlockSpec((1, hdim, d), lambda e: (e, 0, 0))],
        out_specs=pl.BlockSpec((n, d), lambda e: (0, 0)),
        out_shape=jax.ShapeDtypeStruct((n, d), F32),
        name="moe_dense",
    )(h2, wt, lw['w_exp_gate'], lw['w_exp_up'], lw['w_exp_down'])
    return routed.reshape(1, n, d), jnp.ones((n, 1), F32)


def _moe_tail_kernel(x1_ref, h2_ref, rw_ref, g2_ref, wg_ref, wu_ref, wd_ref, lng_ref, lnb_ref, *rest):
    r_refs, o_ref = rest[:-1], rest[-1]
    rw = rw_ref[0]
    routed = rw[:, 0:1] * r_refs[0][0, 0].astype(F32)
    for k in range(1, len(r_refs)):
        routed = routed + rw[:, k:k + 1] * r_refs[k][0, 0].astype(F32)
    act = _expert_act(h2_ref[0].astype(BF16), wg_ref[...], wu_ref[...])
    shared = _dot(act.astype(BF16), wd_ref[...])
    o_ref[0] = _layer_norm_rows(DN_ALPHA * x1_ref[0] + g2_ref[0] * (routed + shared), lng_ref[...], lnb_ref[...])


def moe_tail_pallas(x1, h2, rows, rw, g2, lw):
    b, t, d = x1.shape
    nk = rw.shape[1]
    tm = math.gcd(t, MERGE_ROWS)
    tile = pl.BlockSpec((1, tm, d), lambda i, j: (i, j, 0))
    const = lambda a: pl.BlockSpec(a.shape, lambda i, j: (0,) * a.ndim)
    wg, wu, wd = lw['w_sh_gate'].astype(BF16), lw['w_sh_up'].astype(BF16), lw['w_sh_down'].astype(BF16)
    lng, lnb = lw['ln2_g'].reshape(1, d), lw['ln2_b'].reshape(1, d)
    rows = rows.reshape(nk, b, t, d)
    row_specs = [pl.BlockSpec((1, 1, tm, d), functools.partial(lambda i, j, k: (k, i, j, 0), k=k)) for k in range(nk)]
    return pl.pallas_call(
        _moe_tail_kernel,
        grid=(b, t // tm),
        in_specs=[tile, tile, pl.BlockSpec((1, tm, nk), lambda i, j: (i, j, 0)),
                  pl.BlockSpec((1, 1, d), lambda i, j: (i, 0, 0)),
                  const(wg), const(wu), const(wd), const(lng), const(lnb)] + row_specs,
        out_specs=tile,
        out_shape=jax.ShapeDtypeStruct((b, t, d), F32),
        compiler_params=pltpu.CompilerParams(vmem_limit_bytes=VMEM_LIMIT),
        name="moe_tail_ln2",
    )(x1, h2, rw.reshape(b, t, nk), g2.reshape(b, 1, d), wg, wu, wd, lng, lnb, *([rows] * nk))


def layer_block(x, c, lw, gla_s0, conv_prev, nsa_fn, sorted_moe):
    b, t, d = x.shape
    mod = jax.nn.silu(c) @ lw['w_ada'] + lw['b_ada']
    sh1, sc1, g1, sh2, sc2, g2 = jnp.split(mod, 6, axis=-1)
    if t % LANES == 0:
        y = in_proj(x, sc1, sh1, lw['w_in'])
    else:
        h1 = (x * (1.0 + sc1[:, None, :]) + sh1[:, None, :]).reshape(1, b * t, d)
        y = in_proj(h1, jnp.zeros((1, d), F32), jnp.zeros((1, d), F32), lw['w_in']).reshape(b, t, PROJ_COLS)
    o_gla, s_fin = gla_pallas(y, lw, gla_s0)

    def rows(name):
        return proj_cols(y, name).reshape(b, t, NSA_KV_HEADS, HEAD_DIM).transpose(0, 2, 1, 3)
    kv_rows = [rows(nm) for nm in ('nkc', 'nvc', 'nks', 'nvs', 'nkw', 'nvw')]
    o_n, win_k, win_v = nsa_fn(y, *kv_rows)
    x1, h2, idx, w, conv_state = merge_pallas(x, y, o_gla, o_n, conv_prev, g1, sc2, sh2, lw)
    rows_out, rw = (moe_routed_sorted if sorted_moe else moe_routed_dense)(h2.reshape(b * t, d), idx, w, lw)
    x2 = moe_tail_pallas(x1, h2, rows_out, rw, g2, lw)
    return x2, (*kv_rows[:4], win_k, win_v, s_fin, conv_state)


def kernel(x_prompt, x_sample, cache_cmp_k, cache_cmp_v, cache_slc_k, cache_slc_v, cache_win_k, cache_win_v,
           state_gla, state_conv, page_table, c_prompt, c_sample, w_ada, b_ada, w_in, gla_wa2, gla_ba,
           gla_norm_g, conv_w, cmp_pe_k, cmp_pe_v, cmp_w1_k, cmp_w2_k, cmp_w1_v, cmp_w2_v, rel_bias,
           w_br_gla, w_br_conv, w_br_nsa, w_o, ln1_g, ln1_b, ln2_g, ln2_b, w_router, router_bias,
           w_exp_gate, w_exp_up, w_exp_down, w_sh_gate, w_sh_up, w_sh_down):
    params = dict(w_ada=w_ada, b_ada=b_ada, w_in=w_in, gla_wa2=gla_wa2, gla_ba=gla_ba, gla_norm_g=gla_norm_g,
                  conv_w=conv_w, cmp_pe_k=cmp_pe_k, cmp_pe_v=cmp_pe_v, cmp_w1_k=cmp_w1_k, cmp_w2_k=cmp_w2_k,
                  cmp_w1_v=cmp_w1_v, cmp_w2_v=cmp_w2_v, w_br_gla=w_br_gla, w_br_conv=w_br_conv, w_br_nsa=w_br_nsa,
                  w_o=w_o, ln1_g=ln1_g, ln1_b=ln1_b, ln2_g=ln2_g, ln2_b=ln2_b, w_router=w_router,
                  router_bias=router_bias, w_exp_gate=w_exp_gate, w_exp_up=w_exp_up, w_exp_down=w_exp_down,
                  w_sh_gate=w_sh_gate, w_sh_up=w_sh_up, w_sh_down=w_sh_down)
    tbl = rel_bias.reshape(REL_BUCKETS, NSA_KV_HEADS, NSA_GROUP)
    bp, tp, _ = x_prompt.shape
    bs, ts, _ = x_sample.shape
    bias_tables = nsa_bias_tables(tbl, tp, tp // CMP_STRIDE)
    xp, xs = x_prompt, x_sample
    p_states, s_states = [], []
    for l in range(DEPTH):
        lw = {k: v[l] for k, v in params.items()}

        def prompt_nsa(y, kc_r, vc_r, ks_r, vs_r, kw_r, vw_r):
            o_n = nsa_prompt_pallas(y, PROJ_OFF['nq'], proj_cols(y, 'ngt'), kc_r, vc_r, ks_r, vs_r, kw_r, vw_r,
                                    lw, bias_tables)
            wb = min(WINDOW, tp)
            return o_n, kw_r[:, :, tp - wb:], vw_r[:, :, tp - wb:]

        def sample_nsa(y, kc_r, vc_r, ks_r, vs_r, kw_r, vw_r):
            q_n = proj_cols(y, 'nq').reshape(bs, ts, NSA_KV_HEADS, NSA_GROUP, HEAD_DIM)
            g_n = jax.nn.sigmoid(proj_cols(y, 'ngt')).reshape(bs, ts, NSA_KV_HEADS, NSA_GROUP, 3)
            o_n, win_k, win_v = nsa_sample(q_n, g_n, kc_r, vc_r, ks_r, vs_r, kw_r, vw_r, lw, tbl, l, cache_cmp_k,
                                           cache_cmp_v, cache_slc_k, cache_slc_v, cache_win_k[l], cache_win_v[l],
                                           page_table)
            return o_n.reshape(bs, ts, NSA_Q), win_k, win_v

        xp, st = layer_block(xp, c_prompt, lw, jnp.zeros((bp, GLA_HEADS, GLA_DK, GLA_DV), F32),
                             jnp.zeros((bp, CONV_W - 1, CONV_DIM), F32), prompt_nsa, True)
        p_states.append(st)
        xs, st = layer_block(xs, c_sample, lw, state_gla[l], state_conv[l], sample_nsa, False)
        s_states.append(st)
    p_out = [jnp.stack(z) for z in zip(*p_states)]
    s_out = [jnp.stack(z) for z in zip(*s_states)]
    return (xp, xs, *p_out, *s_out)
```

```python
import math, functools
import jax, jax.numpy as jnp
from jax import lax
import numpy as np
from jax.experimental import pallas as pl
from jax.experimental.pallas import tpu as pltpu

D_MODEL = 1024
DEPTH = 2
PAGE_SIZE = 128
GLA_HEADS = 4
GLA_DK = 128
GLA_DV = 256
GLA_RANK = 16
GLA_TAU = 16.0
GLA_CHUNK = 64
CONV_DIM = D_MODEL
CONV_W = 3
NSA_HEADS = 16
NSA_KV_HEADS = 4
NSA_GROUP = NSA_HEADS // NSA_KV_HEADS
HEAD_DIM = 64
CMP_BLOCK = 32
CMP_STRIDE = 16
CMP_HIDDEN = 128
SLC_BLOCK = 64
SLC_TOPK = 16
WINDOW = 512
NSA_QCHUNK = 16
REL_BUCKETS = 32
REL_MAX_DIST = 128
N_EXPERTS = 64
TOP_K = 8
N_GROUPS = 8
TOPK_GROUPS = 4
EXPERT_HIDDEN = 256
SHARED_HIDDEN = 256
ROUTED_SCALE = 2.5
MOE_BLOCK = 128
DN_ALPHA = (2 * DEPTH) ** 0.25
LN_EPS = 1e-5
NEG_INF = -1e30

GLA_QK = GLA_HEADS * GLA_DK
GLA_V = GLA_HEADS * GLA_DV
NSA_Q = NSA_HEADS * HEAD_DIM
NSA_KV = NSA_KV_HEADS * HEAD_DIM
IN_SIZES = (GLA_QK, GLA_QK, GLA_V, GLA_V, GLA_RANK, CONV_DIM, CONV_DIM, CONV_DIM, NSA_Q, NSA_KV, NSA_KV, NSA_KV, NSA_KV, NSA_KV, NSA_KV, 3 * NSA_HEADS, 3 * D_MODEL)
IN_TOTAL = sum(IN_SIZES)


BF16 = jnp.bfloat16
F32 = jnp.float32
LANES = 128
NSA_TILE = 128
GLA_ROWS = 256
VMEM_LIMIT = 48 * 1024 * 1024

PROJ_NAMES = ('gq', 'gk', 'gv', 'gg', 'ga', 'cb', 'cc', 'cx', 'nq', 'nkc', 'nvc', 'nks', 'nvs', 'nkw', 'nvw', 'ngt', 'mgt')
PROJ_ORDER = ('gv', 'gg', 'cb', 'cc', 'cx', 'nq', 'mgt', 'gq', 'gk', 'nkc', 'nvc', 'nks', 'nvs', 'nkw', 'nvw', 'ga', 'ngt')
PROJ_SIZE = dict(zip(PROJ_NAMES, IN_SIZES))
PROJ_SRC = dict(zip(PROJ_NAMES, np.cumsum((0,) + IN_SIZES[:-1]).tolist()))
PROJ_OFF = dict(zip(PROJ_ORDER, np.cumsum([0] + [PROJ_SIZE[n] for n in PROJ_ORDER[:-1]]).tolist()))
PROJ_TN = 512
PROJ_COLS = -(-IN_TOTAL // PROJ_TN) * PROJ_TN


def _dot_nt(a, b):
    return lax.dot_general(a, b, (((1,), (1,)), ((), ())), preferred_element_type=F32)


def _dot(a, b):
    return jnp.dot(a, b, preferred_element_type=F32)


def _dot_hi(a, b):
    return jnp.dot(a, b, precision=lax.Precision.HIGHEST, preferred_element_type=F32)


def _in_proj_kernel(x_ref, sc_ref, sh_ref, w_ref, o_ref, h_ref):
    @pl.when(pl.program_id(2) == 0)
    def _():
        h_ref[...] = (x_ref[0] * (1.0 + sc_ref[0]) + sh_ref[0]).astype(BF16)
    o_ref[0] = _dot(h_ref[...], w_ref[...])


def in_proj(x, sc, sh, w_in):
    b, t, d = x.shape
    w = jnp.concatenate([w_in[:, PROJ_SRC[n]:PROJ_SRC[n] + PROJ_SIZE[n]] for n in PROJ_ORDER]
                        + [jnp.zeros((d, PROJ_COLS - IN_TOTAL), w_in.dtype)], axis=1).astype(BF16)
    tm = math.gcd(t, 1024)
    return pl.pallas_call(
        _in_proj_kernel,
        grid=(b, t // tm, PROJ_COLS // PROJ_TN),
        in_specs=[pl.BlockSpec((1, tm, d), lambda i, j, k: (i, j, 0)),
                  pl.BlockSpec((1, 1, d), lambda i, j, k: (i, 0, 0)),
                  pl.BlockSpec((1, 1, d), lambda i, j, k: (i, 0, 0)),
                  pl.BlockSpec((d, PROJ_TN), lambda i, j, k: (0, k))],
        out_specs=pl.BlockSpec((1, tm, PROJ_TN), lambda i, j, k: (i, j, k)),
        out_shape=jax.ShapeDtypeStruct((b, t, PROJ_COLS), F32),
        scratch_shapes=[pltpu.VMEM((tm, d), BF16)],
        compiler_params=pltpu.CompilerParams(vmem_limit_bytes=VMEM_LIMIT),
        name="in_proj",
    )(x, sc.reshape(b, 1, d), sh.reshape(b, 1, d), w)


def proj_cols(y, name):
    return y[..., PROJ_OFF[name]:PROJ_OFF[name] + PROJ_SIZE[name]]


def _gla_kernel(q_ref, k_ref, v_ref, gg_ref, sm_ref, wa2_ref, ba_ref, ng_ref, s0_ref, o_ref, sfin_ref, st_ref,
                *, chunk, nchunks):
    tb = pl.program_id(2)
    c = chunk

    @pl.when(tb == 0)
    def _():
        st_ref[...] = s0_ref[0, 0].T

    row = lax.broadcasted_iota(jnp.int32, (c, c), 0)
    col = lax.broadcasted_iota(jnp.int32, (c, c), 1)
    causal = row >= col
    tri = jnp.where(causal, 1.0, 0.0)
    ga_off = PROJ_OFF['ga'] % LANES
    for ci in range(nchunks):
        sl = slice(ci * c, (ci + 1) * c)
        q = q_ref[0, sl, :] * (GLA_DK ** -0.5)
        k = k_ref[0, sl, :]
        v = v_ref[0, sl, :].astype(BF16)
        ga = sm_ref[0, sl, ga_off:ga_off + GLA_RANK]
        x = _dot_hi(ga, wa2_ref[...]) + ba_ref[...]
        log_a = (jnp.minimum(x, 0.0) - jnp.log1p(jnp.exp(-jnp.abs(x)))) * (1.0 / GLA_TAU)
        cum = _dot_hi(tri, log_a)
        last = cum[c - 1:c, :]
        mid = cum[c // 2 - 1:c // 2, :]
        st = st_ref[...]
        inter = _dot_nt((q * jnp.exp(cum)).astype(BF16), st.astype(BF16))
        att = _dot_nt((q * jnp.exp(cum - mid)).astype(BF16), (k * jnp.exp(mid - cum)).astype(BF16))
        att = jnp.where(causal, att, 0.0)
        o = inter + _dot(att.astype(BF16), v)
        kd = (k * jnp.exp(last - cum)).astype(BF16)
        st_ref[...] = st * jnp.exp(last) + lax.dot_general(v, kd, (((0,), (0,)), ((), ())),
                                                           preferred_element_type=F32)
        mu = jnp.mean(o, axis=-1, keepdims=True)
        var = jnp.mean(jnp.square(o - mu), axis=-1, keepdims=True)
        gate = gg_ref[0, sl, :]
        o_ref[0, sl, :] = (o - mu) * lax.rsqrt(var + LN_EPS) * ng_ref[...] * (gate * jax.nn.sigmoid(gate))

    @pl.when(tb == pl.num_programs(2) - 1)
    def _():
        sfin_ref[0, 0] = st_ref[...].T


def gla_pallas(y, lw, s0):
    b, t, _ = y.shape
    c = math.gcd(t, GLA_CHUNK)
    r = math.gcd(t, GLA_ROWS)
    qb, kb = PROJ_OFF['gq'] // GLA_DK, PROJ_OFF['gk'] // GLA_DK
    vb, gb = PROJ_OFF['gv'] // GLA_DV, PROJ_OFF['gg'] // GLA_DV
    sb = PROJ_OFF['ga'] // LANES
    return pl.pallas_call(
        functools.partial(_gla_kernel, chunk=c, nchunks=r // c),
        grid=(b, GLA_HEADS, t // r),
        in_specs=[pl.BlockSpec((1, r, GLA_DK), lambda i, h, j: (i, j, qb + h)),
                  pl.BlockSpec((1, r, GLA_DK), lambda i, h, j: (i, j, kb + h)),
                  pl.BlockSpec((1, r, GLA_DV), lambda i, h, j: (i, j, vb + h)),
                  pl.BlockSpec((1, r, GLA_DV), lambda i, h, j: (i, j, gb + h)),
                  pl.BlockSpec((1, r, LANES), lambda i, h, j: (i, j, sb)),
                  pl.BlockSpec((GLA_RANK, GLA_DK), lambda i, h, j: (0, h)),
                  pl.BlockSpec((1, GLA_DK), lambda i, h, j: (0, h)),
                  pl.BlockSpec((1, GLA_DV), lambda i, h, j: (0, h)),
                  pl.BlockSpec((1, 1, GLA_DK, GLA_DV), lambda i, h, j: (i, h, 0, 0))],
        out_specs=[pl.BlockSpec((1, r, GLA_DV), lambda i, h, j: (i, j, h)),
                   pl.BlockSpec((1, 1, GLA_DK, GLA_DV), lambda i, h, j: (i, h, 0, 0))],
        out_shape=[jax.ShapeDtypeStruct((b, t, GLA_V), F32),
                   jax.ShapeDtypeStruct((b, GLA_HEADS, GLA_DK, GLA_DV), F32)],
        scratch_shapes=[pltpu.VMEM((GLA_DV, GLA_DK), F32)],
        name="gla_scan",
    )(y, y, y, y, y, lw['gla_wa2'], lw['gla_ba'].reshape(1, -1), lw['gla_norm_g'].reshape(1, -1), s0.astype(F32))


def _split3(x):
    hi = x.astype(BF16)
    r = x - hi.astype(F32)
    mid = r.astype(BF16)
    lo = (r - mid.astype(F32)).astype(BF16)
    return hi, mid, lo


def _gelu_tanh(x):
    return 0.5 * x * (1.0 + jnp.tanh(math.sqrt(2.0 / math.pi) * (x + 0.044715 * (x * x * x))))


def _compress_kernel(chk_ref, chv_ref, pek_ref, pev_ref, w1k_ref, w1v_ref, w2k_ref, w2v_ref, ok_ref, ov_ref):
    rows = CMP_STRIDE * HEAD_DIM
    for ch_ref, pe_ref, w1_ref, w2_ref, o_ref in ((chk_ref, pek_ref, w1k_ref, w2k_ref, ok_ref),
                                                   (chv_ref, pev_ref, w1v_ref, w2v_ref, ov_ref)):
        ch = ch_ref[0, 0].astype(BF16)
        nch = ch.shape[0]
        w1 = w1_ref[...]
        pe = jnp.broadcast_to(pe_ref[...], (8, pe_ref.shape[1])).astype(BF16)
        hid = _dot(pe, w1)[0:1, :]
        a0 = _dot(ch, w1[0:rows, :])
        a1 = _dot(ch, w1[rows:2 * rows, :])
        hid = hid + a0 + pltpu.roll(a1, nch - 1, axis=0)
        o_ref[0, 0] = _dot(_gelu_tanh(hid).astype(BF16), w2_ref[...])


def compress_pair(kc_r, vc_r, lw):
    b, kvh, length, hd = kc_r.shape
    nch = length // CMP_STRIDE
    chk = kc_r.reshape(b, kvh, nch, CMP_STRIDE * hd)
    chv = vc_r.reshape(b, kvh, nch, CMP_STRIDE * hd)
    ch_spec = pl.BlockSpec((1, 1, nch, CMP_STRIDE * hd), lambda i, j: (i, j, 0, 0))
    full = lambda a: pl.BlockSpec(a.shape, lambda i, j: (0,) * a.ndim)
    pek = lw['cmp_pe_k'].reshape(1, -1)
    pev = lw['cmp_pe_v'].reshape(1, -1)
    w1k, w1v = lw['cmp_w1_k'].astype(BF16), lw['cmp_w1_v'].astype(BF16)
    w2k, w2v = lw['cmp_w2_k'].astype(BF16), lw['cmp_w2_v'].astype(BF16)
    o_spec = pl.BlockSpec((1, 1, nch, hd), lambda i, j: (i, j, 0, 0))
    o_shape = jax.ShapeDtypeStruct((b, kvh, nch, hd), F32)
    return pl.pallas_call(
        _compress_kernel,
        grid=(b, kvh),
        in_specs=[ch_spec, ch_spec, full(pek), full(pev), full(w1k), full(w1v), full(w2k), full(w2v)],
        out_specs=[o_spec, o_spec],
        out_shape=[o_shape, o_shape],
        name="nsa_compress",
    )(chk, chv, pek, pev, w1k, w1v, w2k, w2v)


def _dot_tn(a, b):
    return lax.dot_general(a, b, (((0,), (0,)), ((), ())), preferred_element_type=F32)


def _nsa_t_kernel(q_ref, g_ref, kc_ref, vc_ref, ks_ref, vs_ref, kw_ref, vw_ref, bc_ref, bt_ref, ov_ref, o_ref,
                  acc_ref, *, nsb):
    tq = NSA_TILE
    grp = NSA_GROUP
    hd = HEAD_DIM
    wq = grp * tq
    ncp = kc_ref.shape[2]
    qi = pl.program_id(2)
    q0 = qi * tq
    q = q_ref[0] * (hd ** -0.5)
    q4 = jnp.concatenate([q[:, g * hd:(g + 1) * hd] for g in range(grp)], axis=0).astype(BF16)
    lane_q = lax.broadcasted_iota(jnp.int32, (1, tq), 1)
    qpos1 = q0 + lane_q
    qpos = jnp.concatenate([qpos1] * grp, axis=1)

    kc = kc_ref[0, 0].astype(BF16)
    vc = vc_ref[0, 0].astype(BF16)
    cend = CMP_STRIDE * lax.broadcasted_iota(jnp.int32, (ncp, 1), 0) + (CMP_BLOCK - 1)
    mask_c = cend <= qpos
    lc = jnp.where(mask_c, _dot_nt(kc, q4) + bc_ref[0, 0], NEG_INF)
    mc = jnp.max(lc, axis=0, keepdims=True)
    pc = jnp.where(mask_c, jnp.exp(lc - mc), 0.0)
    lsum = jnp.sum(pc, axis=0, keepdims=True)
    pc = pc / jnp.where(lsum > 0.0, lsum, 1.0)
    o_c = _dot_tn(vc, pc.astype(BF16))
    psum = pc[:, 0:tq]
    for g in range(1, grp):
        psum = psum + pc[:, g * tq:(g + 1) * tq]

    ov = ov_ref[...]
    hi, mid, lo = _split3(psum)
    imp = (_dot(ov, hi) + _dot(ov, mid) + _dot(ov, lo))[0:nsb, :]
    blk = lax.broadcasted_iota(jnp.int32, (nsb, 1), 0)
    cur = lax.shift_right_logical(qpos1, 6)
    forced = (blk == 0) | (blk == cur) | (blk == cur - 1)
    visible = blk * SLC_BLOCK <= qpos1
    score = jnp.where(forced, -NEG_INF, imp)
    score = jnp.where(visible, score, NEG_INF)
    rank = jnp.zeros((nsb, tq), jnp.int32)
    for i in range(nsb):
        si = score[i:i + 1, :]
        beats = (si > score) | ((si == score) & (blk > i))
        rank = rank + beats.astype(jnp.int32)
    sel1 = jnp.where(visible & (rank < SLC_TOPK), 1.0, 0.0).astype(BF16)
    sel1 = jnp.concatenate([sel1, jnp.zeros((LANES - nsb, tq), BF16)], axis=0)
    sel = jnp.concatenate([sel1] * grp, axis=1)

    neg = jnp.full((1, wq), NEG_INF, F32)
    zero = jnp.zeros((1, wq), F32)
    krow = lax.broadcasted_iota(jnp.int32, (tq, 1), 0)
    lane4 = jnp.concatenate([lane_q] * grp, axis=1)

    def slc_logits(kt, causal, valid):
        k0 = pl.multiple_of(kt * tq, tq)
        k_t = ks_ref[0, 0, pl.ds(k0, tq), :].astype(BF16)
        v_t = vs_ref[0, 0, pl.ds(k0, tq), :].astype(BF16)
        kblk = lax.shift_right_logical(k0 + krow, 6)
        expand = jnp.where(kblk == lax.broadcasted_iota(jnp.int32, (1, LANES), 1), 1.0, 0.0).astype(BF16)
        mask = _dot(expand, sel) > 0.5
        if causal:
            mask = mask & (krow <= lane4)
        if valid is not None:
            mask = mask & valid
        return jnp.where(mask, _dot_nt(k_t, q4) + bt_ref[0, jnp.minimum(qi - kt, 2)], NEG_INF), v_t

    def slc_pair(a, b, m, l):
        (sa, va), (sb, vb) = a, b
        m_new = jnp.maximum(m, jnp.maximum(jnp.max(sa, axis=0, keepdims=True), jnp.max(sb, axis=0, keepdims=True)))
        alpha = jnp.exp(m - m_new)
        pa, pb = jnp.exp(sa - m_new), jnp.exp(sb - m_new)
        l = alpha * l + jnp.sum(pa, axis=0, keepdims=True) + jnp.sum(pb, axis=0, keepdims=True)
        acc_ref[...] = alpha * acc_ref[...] + _dot_tn(va, pa.astype(BF16)) + _dot_tn(vb, pb.astype(BF16))
        return m_new, l

    acc_ref[...] = jnp.zeros_like(acc_ref)
    m_s, l_s = lax.fori_loop(
        0, lax.shift_right_logical(qi, 1),
        lambda j, c: slc_pair(slc_logits(2 * j, False, None), slc_logits(2 * j + 1, False, None), *c), (neg, zero))
    m_s, l_s = slc_pair(slc_logits(jnp.maximum(qi - 1, 0), False, (qi & 1) == 1), slc_logits(qi, True, None), m_s, l_s)
    o_s = acc_ref[...] / l_s

    nwt = WINDOW // tq
    s_w, v_w = [], []
    for dt in range(nwt, -1, -1):
        kt = qi - dt
        k0 = pl.multiple_of(jnp.maximum(kt, 0) * tq, tq)
        k_t = kw_ref[0, 0, pl.ds(k0, tq), :].astype(BF16)
        v_w.append(vw_ref[0, 0, pl.ds(k0, tq), :].astype(BF16))
        s = _dot_nt(k_t, q4) + bt_ref[0, min(dt, 2)]
        if dt == nwt:
            s = jnp.where((krow > lane4) & (kt >= 0), s, NEG_INF)
        elif dt == 0:
            s = jnp.where(krow <= lane4, s, NEG_INF)
        else:
            s = jnp.where(kt >= 0, s, NEG_INF)
        s_w.append(s)
    m_w = functools.reduce(jnp.maximum, [jnp.max(s, axis=0, keepdims=True) for s in s_w])
    p_w = [jnp.exp(s - m_w) for s in s_w]
    l_w = functools.reduce(jnp.add, [jnp.sum(p, axis=0, keepdims=True) for p in p_w])
    o_w = functools.reduce(jnp.add, [_dot_tn(v, p.astype(BF16)) for v, p in zip(v_w, p_w)]) / l_w

    gates = jax.nn.sigmoid(g_ref[0, 0])
    outs = []
    for g in range(grp):
        sl = slice(g * tq, (g + 1) * tq)
        outs.append(gates[:, 3 * g:3 * g + 1] * o_c[:, sl].T + gates[:, 3 * g + 1:3 * g + 2] * o_s[:, sl].T
                    + gates[:, 3 * g + 2:3 * g + 3] * o_w[:, sl].T)
    o_ref[0] = jnp.concatenate(outs, axis=-1)


def nsa_bias_tables(tbl, t, ncp):
    tq = NSA_TILE
    kvh, grp = tbl.shape[1], tbl.shape[2]
    dist_c = jnp.arange(t)[None, :] - (CMP_STRIDE * jnp.arange(ncp) + CMP_BLOCK - 1)[:, None]
    bc = tbl[rel_bucket(dist_c)]
    bc = bc.reshape(ncp, t // tq, tq, kvh, grp).transpose(3, 1, 0, 4, 2).reshape(kvh, t // tq, ncp, grp * tq)
    cr = jnp.arange(tq)[None, :] - jnp.arange(tq)[:, None]
    tiles = jnp.stack([cr, cr + tq, jnp.full_like(cr, REL_MAX_DIST)])
    bt = tbl[rel_bucket(tiles)]
    bt = bt.transpose(3, 0, 1, 4, 2).reshape(kvh, 3, tq, grp * tq)
    return bc, bt


def nsa_prompt_pallas(y, q_off, ngt, kc_r, vc_r, ks_r, vs_r, kw_r, vw_r, lw, bias_tables):
    b, t, _ = y.shape
    kvh, grp, hd, tq = NSA_KV_HEADS, NSA_GROUP, HEAD_DIM, NSA_TILE
    qb = q_off // (grp * hd)
    kc, vc = compress_pair(kc_r, vc_r, lw)
    ncp = kc.shape[2]
    nsb = t // SLC_BLOCK
    bc, bt = bias_tables
    gates = ngt.reshape(b, t, kvh, grp * 3).transpose(0, 2, 1, 3)
    ci = jnp.arange(ncp)[None, :] * CMP_STRIDE
    sj = jnp.arange(LANES)[:, None] * SLC_BLOCK
    ov = ((ci < sj + SLC_BLOCK) & (ci + CMP_BLOCK > sj) & (jnp.arange(ncp)[None, :] < ncp - 1)
          & (jnp.arange(LANES)[:, None] < nsb)).astype(BF16)
    row_spec = pl.BlockSpec((1, 1, t, hd), lambda i, j, k: (i, j, 0, 0))
    cmp_spec = pl.BlockSpec((1, 1, ncp, hd), lambda i, j, k: (i, j, 0, 0))
    return pl.pallas_call(
        functools.partial(_nsa_t_kernel, nsb=nsb),
        grid=(b, kvh, t // tq),
        in_specs=[pl.BlockSpec((1, tq, grp * hd), lambda i, j, k: (i, k, qb + j)),
                  pl.BlockSpec((1, 1, tq, grp * 3), lambda i, j, k: (i, j, k, 0)),
                  cmp_spec, cmp_spec, row_spec, row_spec, row_spec, row_spec,
                  pl.BlockSpec((1, 1, ncp, grp * tq), lambda i, j, k: (j, k, 0, 0)),
                  pl.BlockSpec((1, 3, tq, grp * tq), lambda i, j, k: (j, 0, 0, 0)),
                  pl.BlockSpec(ov.shape, lambda i, j, k: (0, 0))],
        out_specs=pl.BlockSpec((1, tq, grp * hd), lambda i, j, k: (i, k, j)),
        out_shape=jax.ShapeDtypeStruct((b, t, kvh * grp * hd), F32),
        scratch_shapes=[pltpu.VMEM((hd, grp * tq), F32)],
        compiler_params=pltpu.CompilerParams(vmem_limit_bytes=VMEM_LIMIT),
        name="nsa_prompt",
    )(y, gates, kc, vc, ks_r, vs_r, kw_r, vw_r, bc, bt, ov)


def layer_norm(x, g, b):
    xf = x.astype(jnp.float32)
    mu = xf.mean(-1, keepdims=True)
    var = jnp.square(xf - mu).mean(-1, keepdims=True)
    return ((xf - mu) * lax.rsqrt(var + LN_EPS) * g + b).astype(x.dtype)


def masked_softmax(logits, mask):
    p = jax.nn.softmax(jnp.where(mask, logits.astype(jnp.float32), NEG_INF), axis=-1)
    return jnp.where(mask, p, 0.0)


def rel_bucket(dist):
    n = jnp.maximum(dist, 0)
    exact = REL_BUCKETS // 2
    big = exact + (jnp.log(jnp.maximum(n, 1).astype(jnp.float32) / exact)
                   / math.log(REL_MAX_DIST / exact) * (REL_BUCKETS - exact)).astype(jnp.int32)
    return jnp.where(n < exact, n, jnp.minimum(big, REL_BUCKETS - 1))


def gla_scan(q, k, v, log_a, s0):
    b, t = q.shape[:2]
    c = math.gcd(t, GLA_CHUNK)
    nc = t // c
    causal = jnp.tril(jnp.ones((c, c), dtype=bool))

    def to_chunks(a):
        return a.reshape(b, nc, c, *a.shape[2:]).swapaxes(0, 1)

    def step(s, inp):
        qc, kc, vc, ac = inp
        cum = jnp.cumsum(ac, axis=1)
        inter = jnp.einsum('bthk,bhkv->bthv', qc * jnp.exp(cum), s)
        diff = cum[:, :, None] - cum[:, None, :]
        decay = jnp.exp(jnp.where(causal[None, :, :, None, None], diff, -jnp.inf))
        att = jnp.einsum('bthk,bshk,btshk->bhts', qc, kc, decay)
        intra = jnp.einsum('bhts,bshv->bthv', att, vc)
        last = cum[:, -1]
        s = jnp.exp(last)[..., None] * s + jnp.einsum('bshk,bshv->bhkv', kc * jnp.exp(last[:, None] - cum), vc)
        return s, inter + intra

    s_fin, o = lax.scan(step, s0, (to_chunks(q), to_chunks(k), to_chunks(v), to_chunks(log_a)))
    return o.swapaxes(0, 1).reshape(b, t, *v.shape[2:]), s_fin


def gla_output(o, gate, norm_g):
    mu = o.mean(-1, keepdims=True)
    var = jnp.square(o - mu).mean(-1, keepdims=True)
    on = ((o - mu) * lax.rsqrt(var + LN_EPS)).reshape(*o.shape[:2], -1)
    return on * norm_g * jax.nn.silu(gate.astype(jnp.float32))


def short_conv(b_gate, c_gate, x_in, prev, w):
    u = c_gate * x_in
    up = jnp.concatenate([prev, u], axis=1)
    t = u.shape[1]
    y = up[:, 0:t] * w[0]
    for j in range(1, CONV_W):
        y = y + up[:, j:j + t] * w[j]
    return b_gate * y, up[:, t:]


def compress(k, pe, w1, w2):
    b, kvh, length, hd = k.shape
    n_chunks = length // CMP_STRIDE
    pieces = CMP_BLOCK // CMP_STRIDE
    n_blocks = n_chunks - pieces + 1
    ch = k.reshape(b, kvh, n_chunks, CMP_STRIDE * hd)
    rows = CMP_STRIDE * hd
    hid = pe.reshape(-1) @ w1
    for r in range(pieces):
        hid = hid + (ch @ w1[r * rows:(r + 1) * rows])[:, :, r:r + n_blocks]
    return jax.nn.gelu(hid) @ w2


def slc_overlap(n_cmp, n_slc):
    start = jnp.arange(n_cmp) * CMP_STRIDE
    blk = jnp.arange(n_slc) * SLC_BLOCK
    return ((start[:, None] < blk[None, :] + SLC_BLOCK) & (start[:, None] + CMP_BLOCK > blk[None, :])).astype(jnp.float32)


def nsa_attend(q, qpos, gates, kc, vc, overlap, gather_slc, kw, vw, wpos, tbl):
    b, nq, kvh, grp, hd = q.shape
    scale = hd ** -0.5
    f32 = jnp.float32
    hi = jnp.arange(kvh)[:, None, None]
    cmp_end = CMP_STRIDE * jnp.arange(kc.shape[2]) + CMP_BLOCK - 1
    dist_c = qpos[:, None] - cmp_end[None, :]
    def bias(dist, spec):
        onehot = jax.nn.one_hot(rel_bucket(dist), REL_BUCKETS, dtype=f32)
        return jnp.einsum(spec, onehot, tbl, precision=lax.Precision.HIGHEST)

    lc = jnp.einsum('bqhgd,bhnd->bqhgn', q, kc).astype(f32) * scale + bias(dist_c, 'qnk,khg->qhgn')
    pc = masked_softmax(lc, (dist_c >= 0)[:, None, None, :])
    o_c = jnp.einsum('bqhgn,bhnd->bqhgd', pc, vc)
    imp = jnp.einsum('bqhn,ns->bqhs', pc.sum(3), overlap)
    blk = jnp.arange(overlap.shape[1])
    cur = (qpos // SLC_BLOCK)[:, None]
    forced = (blk[None] == 0) | (blk[None] == cur) | (blk[None] == cur - 1)
    visible = blk[None] * SLC_BLOCK <= qpos[:, None]
    score = jnp.where(forced[:, None], -NEG_INF, imp)
    score = jnp.where(visible[:, None], score, NEG_INF)
    top_s, idx = lax.top_k(score, min(SLC_TOPK, overlap.shape[1]))
    ks, vs = gather_slc(idx)
    kpos = idx[..., None] * SLC_BLOCK + jnp.arange(SLC_BLOCK)
    dist_s = qpos[None, :, None, None, None] - kpos
    mask_s = (dist_s >= 0) & (top_s > 0.5 * NEG_INF)[..., None]
    ls = jnp.einsum('bqhgd,bqhnsd->bqhgns', q, ks).astype(f32) * scale + bias(dist_s, 'bqhnsk,khg->bqhgns')
    ps = masked_softmax(ls.reshape(b, nq, kvh, grp, -1), mask_s.reshape(b, nq, kvh, 1, -1))
    o_s = jnp.einsum('bqhgm,bqhmd->bqhgd', ps, vs.reshape(b, nq, kvh, -1, hd))
    dist_w = qpos[:, None] - wpos[None, :]
    mask_w = (dist_w >= 0) & (dist_w < WINDOW) & (wpos >= 0)[None, :]
    lwin = jnp.einsum('bqhgd,bhwd->bqhgw', q, kw).astype(f32) * scale + bias(dist_w, 'qwk,khg->qhgw')
    pw = masked_softmax(lwin, mask_w[:, None, None, :])
    o_w = jnp.einsum('bqhgw,bhwd->bqhgd', pw, vw)
    o = gates[..., 0:1] * o_c + gates[..., 1:2] * o_s + gates[..., 2:3] * o_w
    return o.astype(q.dtype)


def nsa_prompt(q, gates, kc_r, vc_r, ks_r, vs_r, kw_r, vw_r, lw, tbl):
    b, s = q.shape[:2]
    kc = compress(kc_r, lw['cmp_pe_k'], lw['cmp_w1_k'], lw['cmp_w2_k'])
    vc = compress(vc_r, lw['cmp_pe_v'], lw['cmp_w1_v'], lw['cmp_w2_v'])
    nsb = s // SLC_BLOCK
    ks_blk = ks_r.reshape(b, NSA_KV_HEADS, nsb, SLC_BLOCK, HEAD_DIM)
    vs_blk = vs_r.reshape(b, NSA_KV_HEADS, nsb, SLC_BLOCK, HEAD_DIM)
    bi = jnp.arange(b)[:, None, None, None]
    hi = jnp.arange(NSA_KV_HEADS)[None, None, :, None]

    def gather_slc(idx):
        return ks_blk[bi, hi, idx], vs_blk[bi, hi, idx]

    overlap = slc_overlap(kc.shape[2], nsb)
    kw_pad = jnp.pad(kw_r, ((0, 0), (0, 0), (WINDOW, 0), (0, 0)))
    vw_pad = jnp.pad(vw_r, ((0, 0), (0, 0), (WINDOW, 0), (0, 0)))
    n_chunks = s // NSA_QCHUNK
    span = WINDOW + NSA_QCHUNK

    def chunks(a):
        return a.reshape(b, n_chunks, NSA_QCHUNK, *a.shape[2:]).swapaxes(0, 1)

    def body(inp):
        qc, gc, c0 = inp
        return nsa_attend(qc, c0 + jnp.arange(NSA_QCHUNK), gc, kc, vc, overlap, gather_slc,
                          lax.dynamic_slice_in_dim(kw_pad, c0, span, axis=2),
                          lax.dynamic_slice_in_dim(vw_pad, c0, span, axis=2),
                          c0 - WINDOW + jnp.arange(span), tbl)

    o = lax.map(body, (chunks(q), chunks(gates), jnp.arange(n_chunks) * NSA_QCHUNK))
    o = o.swapaxes(0, 1).reshape(b, s, *q.shape[2:])
    wb = min(WINDOW, s)
    return o, kw_r[:, :, s - wb:], vw_r[:, :, s - wb:]


def nsa_sample(q, gates, kc_r, vc_r, ks_r, vs_r, kw_r, vw_r, lw, tbl, layer, pool_ck, pool_cv, pool_sk, pool_sv, win_k, win_v, page_table):
    b, t = q.shape[:2]
    past = page_table.shape[1] * PAGE_SIZE
    t_cmp = -(-t // CMP_STRIDE) * CMP_STRIDE
    nnb = -(-t // SLC_BLOCK)
    npb = past // SLC_BLOCK
    bpp = PAGE_SIZE // SLC_BLOCK

    def past_rows(pool):
        g = pool[layer, page_table]
        return g.transpose(0, 2, 1, 3, 4).reshape(b, NSA_KV_HEADS, past, HEAD_DIM)

    def pad_t(a, n):
        return jnp.pad(a, ((0, 0), (0, 0), (0, n - a.shape[2]), (0, 0)))

    kc = compress(jnp.concatenate([past_rows(pool_ck), pad_t(kc_r, t_cmp)], axis=2), lw['cmp_pe_k'], lw['cmp_w1_k'], lw['cmp_w2_k'])
    vc = compress(jnp.concatenate([past_rows(pool_cv), pad_t(vc_r, t_cmp)], axis=2), lw['cmp_pe_v'], lw['cmp_w1_v'], lw['cmp_w2_v'])
    new_ks = pad_t(ks_r, nnb * SLC_BLOCK).reshape(b, NSA_KV_HEADS, nnb, SLC_BLOCK, HEAD_DIM)
    new_vs = pad_t(vs_r, nnb * SLC_BLOCK).reshape(b, NSA_KV_HEADS, nnb, SLC_BLOCK, HEAD_DIM)
    bi = jnp.arange(b)[:, None, None, None]
    hi = jnp.arange(NSA_KV_HEADS)[None, None, :, None]
    offs = jnp.arange(SLC_BLOCK)

    def gather_slc(idx):
        in_past = (idx < npb)[..., None, None]
        pidx = jnp.minimum(idx, npb - 1)
        page = page_table[bi, pidx // bpp][..., None]
        rows = ((pidx % bpp) * SLC_BLOCK)[..., None] + offs
        nidx = jnp.clip(idx - npb, 0, nnb - 1)
        ks = jnp.where(in_past, pool_sk[layer, page, hi[..., None], rows], new_ks[bi, hi, nidx])
        vs = jnp.where(in_past, pool_sv[layer, page, hi[..., None], rows], new_vs[bi, hi, nidx])
        return ks, vs

    kw = jnp.concatenate([win_k.astype(kw_r.dtype), kw_r], axis=2)
    vw = jnp.concatenate([win_v.astype(vw_r.dtype), vw_r], axis=2)
    wb = win_k.shape[2]
    o = nsa_attend(q, past + jnp.arange(t), gates, kc, vc, slc_overlap(kc.shape[2], npb + nnb), gather_slc,
                   kw, vw, past - wb + jnp.arange(wb + t), tbl)
    return o, kw[:, :, -wb:], vw[:, :, -wb:]


def token_mixer(h, lw, gla_s0, conv_prev, nsa_fn):
    b, t, _ = h.shape
    f32 = jnp.float32
    (gq, gk, gv, gg, ga, cb, cc, cx, nq, nkc, nvc, nks, nvs, nkw, nvw, ngt, mgt) = jnp.split(
        in_proj(h, lw['w_in']), np.cumsum(IN_SIZES)[:-1].tolist(), axis=-1)
    q = gq.reshape(b, t, GLA_HEADS, GLA_DK).astype(f32) * GLA_DK ** -0.5
    k = gk.reshape(b, t, GLA_HEADS, GLA_DK).astype(f32)
    v = gv.reshape(b, t, GLA_HEADS, GLA_DV).astype(f32)
    log_a = (jax.nn.log_sigmoid((ga @ lw['gla_wa2'] + lw['gla_ba']).astype(f32)) / GLA_TAU).reshape(b, t, GLA_HEADS, GLA_DK)
    o, s_fin = gla_scan(q, k, v, log_a, gla_s0.astype(f32))
    o_gla = gla_output(o, gg, lw['gla_norm_g']).astype(h.dtype)
    o_conv, conv_state = short_conv(cb, cc, cx, conv_prev.astype(h.dtype), lw['conv_w'])

    def rows(a):
        return a.reshape(b, t, NSA_KV_HEADS, HEAD_DIM).transpose(0, 2, 1, 3)
    kc_r, vc_r, ks_r, vs_r, kw_r, vw_r = [rows(a) for a in (nkc, nvc, nks, nvs, nkw, nvw)]
    q_n = nq.reshape(b, t, NSA_KV_HEADS, NSA_GROUP, HEAD_DIM)
    g_n = jax.nn.sigmoid(ngt).reshape(b, t, NSA_KV_HEADS, NSA_GROUP, 3)
    o_n, win_k, win_v = nsa_fn(q_n, g_n, kc_r, vc_r, ks_r, vs_r, kw_r, vw_r)
    s_a, s_b, s_c = jnp.split(jax.nn.sigmoid(mgt), 3, axis=-1)
    merged = (s_a * (o_gla @ lw['w_br_gla']) + s_b * (o_conv @ lw['w_br_conv'])
              + s_c * (o_n.reshape(b, t, NSA_Q) @ lw['w_br_nsa']))
    return merged @ lw['w_o'], (kc_r, vc_r, ks_r, vs_r, win_k, win_v, s_fin.astype(h.dtype), conv_state)


def moe_dispatch(hf, idx, wts, wg, wu, wd):
    n, d = hf.shape
    a = n * TOP_K
    fe = idx.reshape(-1)
    ft = jnp.arange(a) // TOP_K
    fw = wts.reshape(-1)
    order = jnp.argsort(fe)
    se, st, sw = fe[order], ft[order], fw[order]
    counts = jnp.zeros((N_EXPERTS,), jnp.int32).at[fe].add(1)
    starts = jnp.cumsum(counts) - counts
    padded = (counts + MOE_BLOCK - 1) // MOE_BLOCK * MOE_BLOCK
    pend = jnp.cumsum(padded)
    pstart = pend - padded
    dest = pstart[se] + jnp.arange(a) - starts[se]
    nb = -(-a // MOE_BLOCK) + N_EXPERTS
    rows = nb * MOE_BLOCK
    row_tok = jnp.full((rows,), n, jnp.int32).at[dest].set(st)
    row_w = jnp.zeros((rows,), hf.dtype).at[dest].set(sw)
    blk_e = jnp.minimum(jnp.searchsorted(pend, jnp.arange(nb) * MOE_BLOCK, side='right'), N_EXPERTS - 1)
    h_pad = jnp.concatenate([hf, jnp.zeros((1, d), hf.dtype)], axis=0)

    def expert_block(inp):
        tok, e = inp
        xb = h_pad[tok]
        return (jax.nn.silu(xb @ wg[e]) * (xb @ wu[e])) @ wd[e]

    y = lax.map(expert_block, (row_tok.reshape(nb, MOE_BLOCK), blk_e))
    return jnp.zeros((n + 1, d), y.dtype).at[row_tok].add(y.reshape(rows, d) * row_w[:, None])[:n]


def moe(h, lw):
    b, t, d = h.shape
    n = b * t
    hf = h.reshape(n, d)
    s = jax.nn.sigmoid((hf @ lw['w_router']).astype(jnp.float32))
    sel = s + lw['router_bias'].astype(jnp.float32)
    per = N_EXPERTS // N_GROUPS
    gscore = lax.top_k(sel.reshape(n, N_GROUPS, per), 2)[0].sum(-1)
    gidx = lax.top_k(gscore, TOPK_GROUPS)[1]
    gmask = jax.nn.one_hot(gidx, N_GROUPS).sum(1) > 0
    sel = jnp.where(jnp.repeat(gmask, per, axis=1), sel, NEG_INF)
    idx = lax.top_k(sel, TOP_K)[1]
    w = jnp.take_along_axis(s, idx, axis=1)
    w = w / w.sum(-1, keepdims=True) * ROUTED_SCALE
    routed = moe_dispatch(hf, idx, w.astype(h.dtype), lw['w_exp_gate'], lw['w_exp_up'], lw['w_exp_down'])
    shared = (jax.nn.silu(hf @ lw['w_sh_gate']) * (hf @ lw['w_sh_up'])) @ lw['w_sh_down']
    return (routed + shared).reshape(b, t, d)


def block(x, c, lw, mixer):
    mod = (jax.nn.silu(c) @ lw['w_ada'] + lw['b_ada'])[:, None, :]
    sh1, sc1, g1, sh2, sc2, g2 = jnp.split(mod, 6, axis=-1)
    mix, state = mixer(x * (1 + sc1) + sh1)
    x = layer_norm(DN_ALPHA * x + g1 * mix, lw['ln1_g'], lw['ln1_b'])
    x = layer_norm(DN_ALPHA * x + g2 * moe(x * (1 + sc2) + sh2, lw), lw['ln2_g'], lw['ln2_b'])
    return x, state


def kernel(x_prompt, x_sample, cache_cmp_k, cache_cmp_v, cache_slc_k, cache_slc_v, cache_win_k, cache_win_v,
           state_gla, state_conv, page_table, c_prompt, c_sample, w_ada, b_ada, w_in, gla_wa2, gla_ba,
           gla_norm_g, conv_w, cmp_pe_k, cmp_pe_v, cmp_w1_k, cmp_w2_k, cmp_w1_v, cmp_w2_v, rel_bias,
           w_br_gla, w_br_conv, w_br_nsa, w_o, ln1_g, ln1_b, ln2_g, ln2_b, w_router, router_bias,
           w_exp_gate, w_exp_up, w_exp_down, w_sh_gate, w_sh_up, w_sh_down):
    tbl = rel_bias.reshape(REL_BUCKETS, NSA_KV_HEADS, NSA_GROUP)
    xp, xs = x_prompt, x_sample
    p_states, s_states = [], []
    for l in range(DEPTH):
        lw = {
            'w_ada': w_ada[l], 'b_ada': b_ada[l], 'w_in': w_in[l], 'gla_wa2': gla_wa2[l], 'gla_ba': gla_ba[l],
            'gla_norm_g': gla_norm_g[l], 'conv_w': conv_w[l], 'cmp_pe_k': cmp_pe_k[l], 'cmp_pe_v': cmp_pe_v[l],
            'cmp_w1_k': cmp_w1_k[l], 'cmp_w2_k': cmp_w2_k[l], 'cmp_w1_v': cmp_w1_v[l], 'cmp_w2_v': cmp_w2_v[l],
            'w_br_gla': w_br_gla[l], 'w_br_conv': w_br_conv[l], 'w_br_nsa': w_br_nsa[l], 'w_o': w_o[l],
            'ln1_g': ln1_g[l], 'ln1_b': ln1_b[l], 'ln2_g': ln2_g[l], 'ln2_b': ln2_b[l],
            'w_router': w_router[l], 'router_bias': router_bias[l], 'w_exp_gate': w_exp_gate[l],
            'w_exp_up': w_exp_up[l], 'w_exp_down': w_exp_down[l], 'w_sh_gate': w_sh_gate[l],
            'w_sh_up': w_sh_up[l], 'w_sh_down': w_sh_down[l],
        }
        prompt_mixer = functools.partial(
            token_mixer, lw=lw,
            gla_s0=jnp.zeros((xp.shape[0], GLA_HEADS, GLA_DK, GLA_DV), jnp.float32),
            conv_prev=jnp.zeros((xp.shape[0], CONV_W - 1, CONV_DIM), xp.dtype),
            nsa_fn=functools.partial(nsa_prompt, lw=lw, tbl=tbl))
        sample_mixer = functools.partial(
            token_mixer, lw=lw, gla_s0=state_gla[l], conv_prev=state_conv[l],
            nsa_fn=functools.partial(nsa_sample, lw=lw, tbl=tbl, layer=l, pool_ck=cache_cmp_k, pool_cv=cache_cmp_v,
                                     pool_sk=cache_slc_k, pool_sv=cache_slc_v, win_k=cache_win_k[l],
                                     win_v=cache_win_v[l], page_table=page_table))
        xp, st = block(xp, c_prompt, lw, prompt_mixer)
        p_states.append(st)
        xs, st = block(xs, c_sample, lw, sample_mixer)
        s_states.append(st)
    p_out = [jnp.stack(z) for z in zip(*p_states)]
    s_out = [jnp.stack(z) for z in zip(*s_states)]
    return (xp, xs, *p_out, *s_out)


MERGE_ROWS = 256
CONV_HALO = 8


def _layer_norm_rows(z, g, b):
    mu = jnp.mean(z, axis=-1, keepdims=True)
    var = jnp.mean(jnp.square(z - mu), axis=-1, keepdims=True)
    return (z - mu) * lax.rsqrt(var + LN_EPS) * g + b


def _rank_rows(v):
    r = v.shape[0]
    rid = lax.broadcasted_iota(jnp.int32, (r, 1), 0)
    rank = jnp.zeros(v.shape, jnp.int32)
    for j in range(r):
        vj = v[j:j + 1, :]
        rank = rank + ((vj > v) | ((vj == v) & (rid > j))).astype(jnp.int32)
    return rank


def _route_cols(s, bias):
    n = s.shape[1]
    per = N_EXPERTS // N_GROUPS
    sel = s + bias
    gs = []
    for g in range(N_GROUPS):
        sg = sel[g * per:(g + 1) * per, :]
        gs.append(jnp.sum(jnp.where(_rank_rows(sg) < 2, sg, 0.0), axis=0, keepdims=True))
    gkeep = _rank_rows(jnp.concatenate(gs, axis=0)) < TOPK_GROUPS
    keep = jnp.concatenate([jnp.broadcast_to(gkeep[g:g + 1, :], (per, n)) for g in range(N_GROUPS)], axis=0)
    rank = _rank_rows(jnp.where(keep, sel, NEG_INF))
    eid = lax.broadcasted_iota(jnp.int32, (N_EXPERTS, 1), 0)
    ids, ws = [], []
    for k in range(TOP_K):
        hit = rank == k
        ids.append(jnp.sum(jnp.where(hit, eid, 0), axis=0, keepdims=True))
        ws.append(jnp.sum(jnp.where(hit, s, 0.0), axis=0, keepdims=True))
    w = jnp.concatenate(ws, axis=0)
    return jnp.concatenate(ids, axis=0), w / jnp.sum(w, axis=0, keepdims=True) * ROUTED_SCALE


def _merge_kernel(x_ref, cb_ref, cc_ref, cx_ref, ccp_ref, cxp_ref, prev_ref, ma_ref, mb_ref, mc_ref, og_ref, on_ref,
                  cw_ref, wg_ref, wc_ref, wn_ref, wo_ref, g1_ref, sc2_ref, sh2_ref, lng_ref, lnb_ref, wr_ref, rb_ref,
                  x1_ref, h2_ref, idx_ref, rw_ref, cst_ref):
    j = pl.program_id(1)
    tm = x_ref.shape[1]
    u = cc_ref[0] * cx_ref[0]
    halo = jnp.where(j == 0, prev_ref[0], ccp_ref[0] * cxp_ref[0])
    p1 = halo[CONV_HALO - 1:CONV_HALO, :]
    p2 = halo[CONV_HALO - 2:CONV_HALO - 1, :]
    rid = lax.broadcasted_iota(jnp.int32, (tm, 1), 0)
    u1 = jnp.where(rid == 0, p1, pltpu.roll(u, 1, axis=0))
    u2 = jnp.where(rid == 0, p2, jnp.where(rid == 1, p1, pltpu.roll(u, 2, axis=0)))
    cw = cw_ref[...]
    o_conv = cb_ref[0] * (u2 * cw[0:1, :] + u1 * cw[1:2, :] + u * cw[2:3, :])
    cst_ref[0] = u[tm - CONV_HALO:tm, :]
    merged = (jax.nn.sigmoid(ma_ref[0]) * _dot(og_ref[0].astype(BF16), wg_ref[...])
              + jax.nn.sigmoid(mb_ref[0]) * _dot(o_conv.astype(BF16), wc_ref[...])
              + jax.nn.sigmoid(mc_ref[0]) * _dot(on_ref[0].astype(BF16), wn_ref[...]))
    mix = _dot(merged.astype(BF16), wo_ref[...])
    x1 = _layer_norm_rows(DN_ALPHA * x_ref[0] + g1_ref[0] * mix, lng_ref[...], lnb_ref[...])
    x1_ref[0] = x1
    h2 = x1 * (1.0 + sc2_ref[0]) + sh2_ref[0]
    h2_ref[0] = h2.astype(h2_ref.dtype)
    s = jax.nn.sigmoid(lax.dot_general(wr_ref[...], h2, (((1,), (1,)), ((), ())),
                                       precision=lax.Precision.HIGHEST, preferred_element_type=F32))
    idx_ref[0], rw_ref[0] = _route_cols(s, rb_ref[...])


def merge_pallas(x, y, o_gla, o_n, conv_prev, g1, sc2, sh2, lw):
    b, t, d = x.shape
    tm = math.gcd(t, MERGE_ROWS)
    halo = CONV_HALO
    hb = tm // halo
    prev = jnp.concatenate([jnp.zeros((b, halo - (CONV_W - 1), d), F32), conv_prev.astype(F32)], axis=1)
    col = lambda name, k=0: PROJ_OFF[name] // d + k
    tile = lambda cb_: pl.BlockSpec((1, tm, d), lambda i, j: (i, j, cb_))
    halo_spec = lambda cb_: pl.BlockSpec((1, halo, d), lambda i, j: (i, jnp.maximum(j * hb - 1, 0), cb_))
    vec = pl.BlockSpec((1, 1, d), lambda i, j: (i, 0, 0))
    const = lambda a: pl.BlockSpec(a.shape, lambda i, j: (0,) * a.ndim)
    wts = [lw['w_br_gla'].astype(BF16), lw['w_br_conv'].astype(BF16), lw['w_br_nsa'].astype(BF16), lw['w_o'].astype(BF16)]
    lng, lnb = lw['ln1_g'].reshape(1, d), lw['ln1_b'].reshape(1, d)
    wr, rb = lw['w_router'].T, lw['router_bias'].reshape(N_EXPERTS, 1).astype(F32)
    topk = pl.BlockSpec((1, TOP_K, tm), lambda i, j: (i, 0, j))
    x1, h2, idx, rw, cst = pl.pallas_call(
        _merge_kernel,
        grid=(b, t // tm),
        in_specs=[tile(0), tile(col('cb')), tile(col('cc')), tile(col('cx')), halo_spec(col('cc')), halo_spec(col('cx')),
                  pl.BlockSpec((1, halo, d), lambda i, j: (i, 0, 0)),
                  tile(col('mgt', 0)), tile(col('mgt', 1)), tile(col('mgt', 2)), tile(0), tile(0),
                  const(lw['conv_w'])] + [const(w) for w in wts] + [vec, vec, vec, const(lng), const(lnb),
                                                                    const(wr), const(rb)],
        out_specs=[tile(0), tile(0), topk, topk, pl.BlockSpec((1, halo, d), lambda i, j: (i, 0, 0))],
        out_shape=[jax.ShapeDtypeStruct((b, t, d), F32), jax.ShapeDtypeStruct((b, t, d), BF16 if tm % 16 == 0 else F32),
                   jax.ShapeDtypeStruct((b, TOP_K, t), jnp.int32), jax.ShapeDtypeStruct((b, TOP_K, t), F32),
                   jax.ShapeDtypeStruct((b, halo, d), F32)],
        compiler_params=pltpu.CompilerParams(vmem_limit_bytes=VMEM_LIMIT),
        name="merge_ln1",
    )(x, y, y, y, y, y, prev, y, y, y, o_gla, o_n, lw['conv_w'], *wts,
      g1.reshape(b, 1, d), sc2.reshape(b, 1, d), sh2.reshape(b, 1, d), lng, lnb, wr, rb)
    to_rows = lambda a: a.transpose(0, 2, 1).reshape(b * t, TOP_K)
    return x1, h2, to_rows(idx), to_rows(rw), cst[:, halo - (CONV_W - 1):]


def _expert_act(xb, wg, wu):
    hg = _dot(xb, wg)
    return hg * jax.nn.sigmoid(hg) * _dot(xb, wu)


def _grouped_expert_kernel(be_ref, x_ref, wg_ref, wu_ref, wd_ref, o_ref):
    act = _expert_act(x_ref[...], wg_ref[0], wu_ref[0])
    o_ref[...] = _dot(act.astype(BF16), wd_ref[0]).astype(o_ref.dtype)


def moe_routed_sorted(h2, idx, w, lw):
    n, d = h2.shape
    a = n * TOP_K
    onehot = idx[:, :, None] == jnp.arange(N_EXPERTS, dtype=jnp.int32)[None, None, :]
    hit = onehot.any(axis=1).astype(F32)
    ct = math.gcd(n, 512)
    hit_t = hit.reshape(n // ct, ct, N_EXPERTS)
    within = jnp.einsum('ts,nse->nte', jnp.tril(jnp.ones((ct, ct), F32), -1), hit_t)
    tile_tot = hit_t.sum(1)
    before = jnp.cumsum(tile_tot, axis=0) - tile_tot
    pos = (within + before[:, None, :]).reshape(n, N_EXPERTS).astype(jnp.int32)
    counts = tile_tot.sum(0).astype(jnp.int32)
    padded = (counts + MOE_BLOCK - 1) // MOE_BLOCK * MOE_BLOCK
    pend = jnp.cumsum(padded)
    pstart = pend - padded
    slot = jnp.sum(jnp.where(onehot, (pos + pstart[None, :])[:, None, :], 0), axis=-1)
    nb = -(-a // MOE_BLOCK) + N_EXPERTS
    rows = nb * MOE_BLOCK
    row_tok = jnp.zeros((rows,), jnp.int32).at[slot.reshape(-1)].set(jnp.arange(a, dtype=jnp.int32) // TOP_K)
    blk_e = jnp.minimum(jnp.searchsorted(pend, jnp.arange(nb) * MOE_BLOCK, side='right'), N_EXPERTS - 1).astype(jnp.int32)
    xs = h2.astype(BF16)[row_tok]
    wg, wu, wd = lw['w_exp_gate'].astype(BF16), lw['w_exp_up'].astype(BF16), lw['w_exp_down'].astype(BF16)
    hdim = wg.shape[2]
    y = pl.pallas_call(
        _grouped_expert_kernel,
        grid_spec=pltpu.PrefetchScalarGridSpec(
            num_scalar_prefetch=1,
            grid=(nb,),
            in_specs=[pl.BlockSpec((MOE_BLOCK, d), lambda i, be: (i, 0)),
                      pl.BlockSpec((1, d, hdim), lambda i, be: (be[i], 0, 0)),
                      pl.BlockSpec((1, d, hdim), lambda i, be: (be[i], 0, 0)),
                      pl.BlockSpec((1, hdim, d), lambda i, be: (be[i], 0, 0))],
            out_specs=pl.BlockSpec((MOE_BLOCK, d), lambda i, be: (i, 0))),
        out_shape=jax.ShapeDtypeStruct((rows, d), BF16),
        name="moe_grouped",
    )(blk_e, xs, wg, wu, wd)
    return y[slot.T.reshape(-1)].reshape(TOP_K, n, d), w


def _dense_expert_kernel(x_ref, wt_ref, wg_ref, wu_ref, wd_ref, o_ref):
    @pl.when(pl.program_id(0) == 0)
    def _():
        o_ref[...] = jnp.zeros_like(o_ref)
    act = _expert_act(x_ref[...].astype(BF16), wg_ref[0].astype(BF16), wu_ref[0].astype(BF16))
    o_ref[...] += _dot((act * wt_ref[0]).astype(BF16), wd_ref[0].astype(BF16))


def moe_routed_dense(h2, idx, w, lw):
    n, d = h2.shape
    wdense = jnp.zeros((n, N_EXPERTS), F32).at[jnp.arange(n)[:, None], idx].add(w)
    wt = wdense.T.reshape(N_EXPERTS, n, 1)
    hdim = lw['w_exp_gate'].shape[2]
    routed = pl.pallas_call(
        _dense_expert_kernel,
        grid=(N_EXPERTS,),
        in_specs=[pl.BlockSpec((n, d), lambda e: (0, 0)),
                  pl.BlockSpec((1, n, 1), lambda e: (e, 0, 0)),
                  pl.BlockSpec((1, d, hdim), lambda e: (e, 0, 0)),
                  pl.BlockSpec((1, d, hdim), lambda e: (e, 0, 0)),
                  pl.BlockSpec((1, hdim, d), lambda e: (e, 0, 0))],
        out_specs=pl.BlockSpec((n, d), lambda e: (0, 0)),
        out_shape=jax.ShapeDtypeStruct((n, d), F32),
        name="moe_dense",
    )(h2, wt, lw['w_exp_gate'], lw['w_exp_up'], lw['w_exp_down'])
    return routed.reshape(1, n, d), jnp.ones((n, 1), F32)


def _moe_tail_kernel(x1_ref, h2_ref, rw_ref, g2_ref, wg_ref, wu_ref, wd_ref, lng_ref, lnb_ref, *rest):
    r_refs, o_ref = rest[:-1], rest[-1]
    rw = rw_ref[0]
    routed = rw[:, 0:1] * r_refs[0][0, 0].astype(F32)
    for k in range(1, len(r_refs)):
        routed = routed + rw[:, k:k + 1] * r_refs[k][0, 0].astype(F32)
    act = _expert_act(h2_ref[0].astype(BF16), wg_ref[...], wu_ref[...])
    shared = _dot(act.astype(BF16), wd_ref[...])
    o_ref[0] = _layer_norm_rows(DN_ALPHA * x1_ref[0] + g2_ref[0] * (routed + shared), lng_ref[...], lnb_ref[...])


def moe_tail_pallas(x1, h2, rows, rw, g2, lw):
    b, t, d = x1.shape
    nk = rw.shape[1]
    tm = math.gcd(t, MERGE_ROWS)
    tile = pl.BlockSpec((1, tm, d), lambda i, j: (i, j, 0))
    const = lambda a: pl.BlockSpec(a.shape, lambda i, j: (0,) * a.ndim)
    wg, wu, wd = lw['w_sh_gate'].astype(BF16), lw['w_sh_up'].astype(BF16), lw['w_sh_down'].astype(BF16)
    lng, lnb = lw['ln2_g'].reshape(1, d), lw['ln2_b'].reshape(1, d)
    rows = rows.reshape(nk, b, t, d)
    row_specs = [pl.BlockSpec((1, 1, tm, d), functools.partial(lambda i, j, k: (k, i, j, 0), k=k)) for k in range(nk)]
    return pl.pallas_call(
        _moe_tail_kernel,
        grid=(b, t // tm),
        in_specs=[tile, tile, pl.BlockSpec((1, tm, nk), lambda i, j: (i, j, 0)),
                  pl.BlockSpec((1, 1, d), lambda i, j: (i, 0, 0)),
                  const(wg), const(wu), const(wd), const(lng), const(lnb)] + row_specs,
        out_specs=tile,
        out_shape=jax.ShapeDtypeStruct((b, t, d), F32),
        compiler_params=pltpu.CompilerParams(vmem_limit_bytes=VMEM_LIMIT),
        name="moe_tail_ln2",
    )(x1, h2, rw.reshape(b, t, nk), g2.reshape(b, 1, d), wg, wu, wd, lng, lnb, *([rows] * nk))


def layer_block(x, c, lw, gla_s0, conv_prev, nsa_fn, sorted_moe):
    b, t, d = x.shape
    mod = jax.nn.silu(c) @ lw['w_ada'] + lw['b_ada']
    sh1, sc1, g1, sh2, sc2, g2 = jnp.split(mod, 6, axis=-1)
    if t % LANES == 0:
        y = in_proj(x, sc1, sh1, lw['w_in'])
    else:
        h1 = (x * (1.0 + sc1[:, None, :]) + sh1[:, None, :]).reshape(1, b * t, d)
        y = in_proj(h1, jnp.zeros((1, d), F32), jnp.zeros((1, d), F32), lw['w_in']).reshape(b, t, PROJ_COLS)
    o_gla, s_fin = gla_pallas(y, lw, gla_s0)

    def rows(name):
        return proj_cols(y, name).reshape(b, t, NSA_KV_HEADS, HEAD_DIM).transpose(0, 2, 1, 3)
    kv_rows = [rows(nm) for nm in ('nkc', 'nvc', 'nks', 'nvs', 'nkw', 'nvw')]
    o_n, win_k, win_v = nsa_fn(y, *kv_rows)
    x1, h2, idx, w, conv_state = merge_pallas(x, y, o_gla, o_n, conv_prev, g1, sc2, sh2, lw)
    rows_out, rw = (moe_routed_sorted if sorted_moe else moe_routed_dense)(h2.reshape(b * t, d), idx, w, lw)
    x2 = moe_tail_pallas(x1, h2, rows_out, rw, g2, lw)
    return x2, (*kv_rows[:4], win_k, win_v, s_fin, conv_state)


def kernel(x_prompt, x_sample, cache_cmp_k, cache_cmp_v, cache_slc_k, cache_slc_v, cache_win_k, cache_win_v,
           state_gla, state_conv, page_table, c_prompt, c_sample, w_ada, b_ada, w_in, gla_wa2, gla_ba,
           gla_norm_g, conv_w, cmp_pe_k, cmp_pe_v, cmp_w1_k, cmp_w2_k, cmp_w1_v, cmp_w2_v, rel_bias,
           w_br_gla, w_br_conv, w_br_nsa, w_o, ln1_g, ln1_b, ln2_g, ln2_b, w_router, router_bias,
           w_exp_gate, w_exp_up, w_exp_down, w_sh_gate, w_sh_up, w_sh_down):
    params = dict(w_ada=w_ada, b_ada=b_ada, w_in=w_in, gla_wa2=gla_wa2, gla_ba=gla_ba, gla_norm_g=gla_norm_g,
                  conv_w=conv_w, cmp_pe_k=cmp_pe_k, cmp_pe_v=cmp_pe_v, cmp_w1_k=cmp_w1_k, cmp_w2_k=cmp_w2_k,
                  cmp_w1_v=cmp_w1_v, cmp_w2_v=cmp_w2_v, w_br_gla=w_br_gla, w_br_conv=w_br_conv, w_br_nsa=w_br_nsa,
                  w_o=w_o, ln1_g=ln1_g, ln1_b=ln1_b, ln2_g=ln2_g, ln2_b=ln2_b, w_router=w_router,
                  router_bias=router_bias, w_exp_gate=w_exp_gate, w_exp_up=w_exp_up, w_exp_down=w_exp_down,
                  w_sh_gate=w_sh_gate, w_sh_up=w_sh_up, w_sh_down=w_sh_down)
    tbl = rel_bias.reshape(REL_BUCKETS, NSA_KV_HEADS, NSA_GROUP)
    bp, tp, _ = x_prompt.shape
    bs, ts, _ = x_sample.shape
    bias_tables = nsa_bias_tables(tbl, tp, tp // CMP_STRIDE)
    xp, xs = x_prompt, x_sample
    p_states, s_states = [], []
    for l in range(DEPTH):
        lw = {k: v[l] for k, v in params.items()}

        def prompt_nsa(y, kc_r, vc_r, ks_r, vs_r, kw_r, vw_r):
            o_n = nsa_prompt_pallas(y, PROJ_OFF['nq'], proj_cols(y, 'ngt'), kc_r, vc_r, ks_r, vs_r, kw_r, vw_r,
                                    lw, bias_tables)
            wb = min(WINDOW, tp)
            return o_n, kw_r[:, :, tp - wb:], vw_r[:, :, tp - wb:]

        def sample_nsa(y, kc_r, vc_r, ks_r, vs_r, kw_r, vw_r):
            q_n = proj_cols(y, 'nq').reshape(bs, ts, NSA_KV_HEADS, NSA_GROUP, HEAD_DIM)
            g_n = jax.nn.sigmoid(proj_cols(y, 'ngt')).reshape(bs, ts, NSA_KV_HEADS, NSA_GROUP, 3)
            o_n, win_k, win_v = nsa_sample(q_n, g_n, kc_r, vc_r, ks_r, vs_r, kw_r, vw_r, lw, tbl, l, cache_cmp_k,
                                           cache_cmp_v, cache_slc_k, cache_slc_v, cache_win_k[l], cache_win_v[l],
                                           page_table)
            return o_n.reshape(bs, ts, NSA_Q), win_k, win_v

        xp, st = layer_block(xp, c_prompt, lw, jnp.zeros((bp, GLA_HEADS, GLA_DK, GLA_DV), F32),
                             jnp.zeros((bp, CONV_W - 1, CONV_DIM), F32), prompt_nsa, True)
        p_states.append(st)
        xs, st = layer_block(xs, c_sample, lw, state_gla[l], state_conv[l], sample_nsa, False)
        s_states.append(st)
    p_out = [jnp.stack(z) for z in zip(*p_states)]
    s_out = [jnp.stack(z) for z in zip(*s_states)]
    return (xp, xs, *p_out, *s_out)
```

```python
import math, functools
import jax, jax.numpy as jnp
from jax import lax
import numpy as np
from jax.experimental import pallas as pl
from jax.experimental.pallas import tpu as pltpu

D_MODEL = 1024
DEPTH = 2
PAGE_SIZE = 128
GLA_HEADS = 4
GLA_DK = 128
GLA_DV = 256
GLA_RANK = 16
GLA_TAU = 16.0
GLA_CHUNK = 64
CONV_DIM = D_MODEL
CONV_W = 3
NSA_HEADS = 16
NSA_KV_HEADS = 4
NSA_GROUP = NSA_HEADS // NSA_KV_HEADS
HEAD_DIM = 64
CMP_BLOCK = 32
CMP_STRIDE = 16
CMP_HIDDEN = 128
SLC_BLOCK = 64
SLC_TOPK = 16
WINDOW = 512
NSA_QCHUNK = 16
REL_BUCKETS = 32
REL_MAX_DIST = 128
N_EXPERTS = 64
TOP_K = 8
N_GROUPS = 8
TOPK_GROUPS = 4
EXPERT_HIDDEN = 256
SHARED_HIDDEN = 256
ROUTED_SCALE = 2.5
MOE_BLOCK = 128
DN_ALPHA = (2 * DEPTH) ** 0.25
LN_EPS = 1e-5
NEG_INF = -1e30

GLA_QK = GLA_HEADS * GLA_DK
GLA_V = GLA_HEADS * GLA_DV
NSA_Q = NSA_HEADS * HEAD_DIM
NSA_KV = NSA_KV_HEADS * HEAD_DIM
IN_SIZES = (GLA_QK, GLA_QK, GLA_V, GLA_V, GLA_RANK, CONV_DIM, CONV_DIM, CONV_DIM, NSA_Q, NSA_KV, NSA_KV, NSA_KV, NSA_KV, NSA_KV, NSA_KV, 3 * NSA_HEADS, 3 * D_MODEL)
IN_TOTAL = sum(IN_SIZES)


BF16 = jnp.bfloat16
F32 = jnp.float32
LANES = 128
NSA_TILE = 128
GLA_ROWS = 256
VMEM_LIMIT = 48 * 1024 * 1024

PROJ_NAMES = ('gq', 'gk', 'gv', 'gg', 'ga', 'cb', 'cc', 'cx', 'nq', 'nkc', 'nvc', 'nks', 'nvs', 'nkw', 'nvw', 'ngt', 'mgt')
PROJ_ORDER = ('gv', 'gg', 'cb', 'cc', 'cx', 'nq', 'mgt', 'gq', 'gk', 'nkc', 'nvc', 'nks', 'nvs', 'nkw', 'nvw', 'ga', 'ngt')
PROJ_SIZE = dict(zip(PROJ_NAMES, IN_SIZES))
PROJ_SRC = dict(zip(PROJ_NAMES, np.cumsum((0,) + IN_SIZES[:-1]).tolist()))
PROJ_OFF = dict(zip(PROJ_ORDER, np.cumsum([0] + [PROJ_SIZE[n] for n in PROJ_ORDER[:-1]]).tolist()))
PROJ_TN = 512
PROJ_COLS = -(-IN_TOTAL // PROJ_TN) * PROJ_TN


def _dot_nt(a, b):
    return lax.dot_general(a, b, (((1,), (1,)), ((), ())), preferred_element_type=F32)


def _dot(a, b):
    return jnp.dot(a, b, preferred_element_type=F32)


def _dot_hi(a, b):
    return jnp.dot(a, b, precision=lax.Precision.HIGHEST, preferred_element_type=F32)


def _in_proj_kernel(x_ref, sc_ref, sh_ref, w_ref, o_ref, h_ref):
    @pl.when(pl.program_id(2) == 0)
    def _():
        h_ref[...] = (x_ref[0] * (1.0 + sc_ref[0]) + sh_ref[0]).astype(BF16)
    o_ref[0] = _dot(h_ref[...], w_ref[...])


def in_proj(x, sc, sh, w_in):
    b, t, d = x.shape
    w = jnp.concatenate([w_in[:, PROJ_SRC[n]:PROJ_SRC[n] + PROJ_SIZE[n]] for n in PROJ_ORDER]
                        + [jnp.zeros((d, PROJ_COLS - IN_TOTAL), w_in.dtype)], axis=1).astype(BF16)
    tm = math.gcd(t, 1024)
    return pl.pallas_call(
        _in_proj_kernel,
        grid=(b, t // tm, PROJ_COLS // PROJ_TN),
        in_specs=[pl.BlockSpec((1, tm, d), lambda i, j, k: (i, j, 0)),
                  pl.BlockSpec((1, 1, d), lambda i, j, k: (i, 0, 0)),
                  pl.BlockSpec((1, 1, d), lambda i, j, k: (i, 0, 0)),
                  pl.BlockSpec((d, PROJ_TN), lambda i, j, k: (0, k))],
        out_specs=pl.BlockSpec((1, tm, PROJ_TN), lambda i, j, k: (i, j, k)),
        out_shape=jax.ShapeDtypeStruct((b, t, PROJ_COLS), F32),
        scratch_shapes=[pltpu.VMEM((tm, d), BF16)],
        compiler_params=pltpu.CompilerParams(vmem_limit_bytes=VMEM_LIMIT),
        name="in_proj",
    )(x, sc.reshape(b, 1, d), sh.reshape(b, 1, d), w)


def proj_cols(y, name):
    return y[..., PROJ_OFF[name]:PROJ_OFF[name] + PROJ_SIZE[name]]


def _gla_kernel(q_ref, k_ref, v_ref, gg_ref, sm_ref, wa2_ref, ba_ref, ng_ref, s0_ref, o_ref, sfin_ref, st_ref,
                *, chunk, nchunks):
    tb = pl.program_id(2)
    c = chunk

    @pl.when(tb == 0)
    def _():
        st_ref[...] = s0_ref[0, 0].T

    row = lax.broadcasted_iota(jnp.int32, (c, c), 0)
    col = lax.broadcasted_iota(jnp.int32, (c, c), 1)
    causal = row >= col
    tri = jnp.where(causal, 1.0, 0.0)
    ga_off = PROJ_OFF['ga'] % LANES
    for ci in range(nchunks):
        sl = slice(ci * c, (ci + 1) * c)
        q = q_ref[0, sl, :] * (GLA_DK ** -0.5)
        k = k_ref[0, sl, :]
        v = v_ref[0, sl, :].astype(BF16)
        ga = sm_ref[0, sl, ga_off:ga_off + GLA_RANK]
        x = _dot_hi(ga, wa2_ref[...]) + ba_ref[...]
        log_a = (jnp.minimum(x, 0.0) - jnp.log1p(jnp.exp(-jnp.abs(x)))) * (1.0 / GLA_TAU)
        cum = _dot_hi(tri, log_a)
        last = cum[c - 1:c, :]
        mid = cum[c // 2 - 1:c // 2, :]
        st = st_ref[...]
        inter = _dot_nt((q * jnp.exp(cum)).astype(BF16), st.astype(BF16))
        att = _dot_nt((q * jnp.exp(cum - mid)).astype(BF16), (k * jnp.exp(mid - cum)).astype(BF16))
        att = jnp.where(causal, att, 0.0)
        o = inter + _dot(att.astype(BF16), v)
        kd = (k * jnp.exp(last - cum)).astype(BF16)
        st_ref[...] = st * jnp.exp(last) + lax.dot_general(v, kd, (((0,), (0,)), ((), ())),
                                                           preferred_element_type=F32)
        mu = jnp.mean(o, axis=-1, keepdims=True)
        var = jnp.mean(jnp.square(o - mu), axis=-1, keepdims=True)
        gate = gg_ref[0, sl, :]
        o_ref[0, sl, :] = (o - mu) * lax.rsqrt(var + LN_EPS) * ng_ref[...] * (gate * jax.nn.sigmoid(gate))

    @pl.when(tb == pl.num_programs(2) - 1)
    def _():
        sfin_ref[0, 0] = st_ref[...].T


def gla_pallas(y, lw, s0):
    b, t, _ = y.shape
    c = math.gcd(t, GLA_CHUNK)
    r = math.gcd(t, GLA_ROWS)
    qb, kb = PROJ_OFF['gq'] // GLA_DK, PROJ_OFF['gk'] // GLA_DK
    vb, gb = PROJ_OFF['gv'] // GLA_DV, PROJ_OFF['gg'] // GLA_DV
    sb = PROJ_OFF['ga'] // LANES
    return pl.pallas_call(
        functools.partial(_gla_kernel, chunk=c, nchunks=r // c),
        grid=(b, GLA_HEADS, t // r),
        in_specs=[pl.BlockSpec((1, r, GLA_DK), lambda i, h, j: (i, j, qb + h)),
                  pl.BlockSpec((1, r, GLA_DK), lambda i, h, j: (i, j, kb + h)),
                  pl.BlockSpec((1, r, GLA_DV), lambda i, h, j: (i, j, vb + h)),
                  pl.BlockSpec((1, r, GLA_DV), lambda i, h, j: (i, j, gb + h)),
                  pl.BlockSpec((1, r, LANES), lambda i, h, j: (i, j, sb)),
                  pl.BlockSpec((GLA_RANK, GLA_DK), lambda i, h, j: (0, h)),
                  pl.BlockSpec((1, GLA_DK), lambda i, h, j: (0, h)),
                  pl.BlockSpec((1, GLA_DV), lambda i, h, j: (0, h)),
                  pl.BlockSpec((1, 1, GLA_DK, GLA_DV), lambda i, h, j: (i, h, 0, 0))],
        out_specs=[pl.BlockSpec((1, r, GLA_DV), lambda i, h, j: (i, j, h)),
                   pl.BlockSpec((1, 1, GLA_DK, GLA_DV), lambda i, h, j: (i, h, 0, 0))],
        out_shape=[jax.ShapeDtypeStruct((b, t, GLA_V), F32),
                   jax.ShapeDtypeStruct((b, GLA_HEADS, GLA_DK, GLA_DV), F32)],
        scratch_shapes=[pltpu.VMEM((GLA_DV, GLA_DK), F32)],
        name="gla_scan",
    )(y, y, y, y, y, lw['gla_wa2'], lw['gla_ba'].reshape(1, -1), lw['gla_norm_g'].reshape(1, -1), s0.astype(F32))


def _split3(x):
    hi = x.astype(BF16)
    r = x - hi.astype(F32)
    mid = r.astype(BF16)
    lo = (r - mid.astype(F32)).astype(BF16)
    return hi, mid, lo


def _gelu_tanh(x):
    return 0.5 * x * (1.0 + jnp.tanh(math.sqrt(2.0 / math.pi) * (x + 0.044715 * (x * x * x))))


def _compress_kernel(chk_ref, chv_ref, pek_ref, pev_ref, w1k_ref, w1v_ref, w2k_ref, w2v_ref, ok_ref, ov_ref):
    rows = CMP_STRIDE * HEAD_DIM
    for ch_ref, pe_ref, w1_ref, w2_ref, o_ref in ((chk_ref, pek_ref, w1k_ref, w2k_ref, ok_ref),
                                                   (chv_ref, pev_ref, w1v_ref, w2v_ref, ov_ref)):
        ch = ch_ref[0, 0].astype(BF16)
        nch = ch.shape[0]
        w1 = w1_ref[...]
        pe = jnp.broadcast_to(pe_ref[...], (8, pe_ref.shape[1])).astype(BF16)
        hid = _dot(pe, w1)[0:1, :]
        a0 = _dot(ch, w1[0:rows, :])
        a1 = _dot(ch, w1[rows:2 * rows, :])
        hid = hid + a0 + pltpu.roll(a1, nch - 1, axis=0)
        o_ref[0, 0] = _dot(_gelu_tanh(hid).astype(BF16), w2_ref[...])


def compress_pair(kc_r, vc_r, lw):
    b, kvh, length, hd = kc_r.shape
    nch = length // CMP_STRIDE
    chk = kc_r.reshape(b, kvh, nch, CMP_STRIDE * hd)
    chv = vc_r.reshape(b, kvh, nch, CMP_STRIDE * hd)
    ch_spec = pl.BlockSpec((1, 1, nch, CMP_STRIDE * hd), lambda i, j: (i, j, 0, 0))
    full = lambda a: pl.BlockSpec(a.shape, lambda i, j: (0,) * a.ndim)
    pek = lw['cmp_pe_k'].reshape(1, -1)
    pev = lw['cmp_pe_v'].reshape(1, -1)
    w1k, w1v = lw['cmp_w1_k'].astype(BF16), lw['cmp_w1_v'].astype(BF16)
    w2k, w2v = lw['cmp_w2_k'].astype(BF16), lw['cmp_w2_v'].astype(BF16)
    o_spec = pl.BlockSpec((1, 1, nch, hd), lambda i, j: (i, j, 0, 0))
    o_shape = jax.ShapeDtypeStruct((b, kvh, nch, hd), F32)
    return pl.pallas_call(
        _compress_kernel,
        grid=(b, kvh),
        in_specs=[ch_spec, ch_spec, full(pek), full(pev), full(w1k), full(w1v), full(w2k), full(w2v)],
        out_specs=[o_spec, o_spec],
        out_shape=[o_shape, o_shape],
        name="nsa_compress",
    )(chk, chv, pek, pev, w1k, w1v, w2k, w2v)


def _dot_tn(a, b):
    return lax.dot_general(a, b, (((0,), (0,)), ((), ())), preferred_element_type=F32)


def _nsa_t_kernel(q_ref, g_ref, kc_ref, vc_ref, ks_ref, vs_ref, kw_ref, vw_ref, bc_ref, bt_ref, ov_ref, o_ref,
                  acc_ref, *, nsb):
    tq = NSA_TILE
    grp = NSA_GROUP
    hd = HEAD_DIM
    wq = grp * tq
    ncp = kc_ref.shape[2]
    qi = pl.program_id(2)
    q0 = qi * tq
    q = q_ref[0] * (hd ** -0.5)
    q4 = jnp.concatenate([q[:, g * hd:(g + 1) * hd] for g in range(grp)], axis=0).astype(BF16)
    lane_q = lax.broadcasted_iota(jnp.int32, (1, tq), 1)
    qpos1 = q0 + lane_q
    qpos = jnp.concatenate([qpos1] * grp, axis=1)

    kc = kc_ref[0, 0].astype(BF16)
    vc = vc_ref[0, 0].astype(BF16)
    cend = CMP_STRIDE * lax.broadcasted_iota(jnp.int32, (ncp, 1), 0) + (CMP_BLOCK - 1)
    mask_c = cend <= qpos
    lc = jnp.where(mask_c, _dot_nt(kc, q4) + bc_ref[0, 0], NEG_INF)
    mc = jnp.max(lc, axis=0, keepdims=True)
    pc = jnp.where(mask_c, jnp.exp(lc - mc), 0.0)
    lsum = jnp.sum(pc, axis=0, keepdims=True)
    pc = pc / jnp.where(lsum > 0.0, lsum, 1.0)
    o_c = _dot_tn(vc, pc.astype(BF16))
    psum = pc[:, 0:tq]
    for g in range(1, grp):
        psum = psum + pc[:, g * tq:(g + 1) * tq]

    ov = ov_ref[...]
    hi, mid, lo = _split3(psum)
    imp = (_dot(ov, hi) + _dot(ov, mid) + _dot(ov, lo))[0:nsb, :]
    blk = lax.broadcasted_iota(jnp.int32, (nsb, 1), 0)
    cur = lax.shift_right_logical(qpos1, 6)
    forced = (blk == 0) | (blk == cur) | (blk == cur - 1)
    visible = blk * SLC_BLOCK <= qpos1
    score = jnp.where(forced, -NEG_INF, imp)
    score = jnp.where(visible, score, NEG_INF)
    rank = jnp.zeros((nsb, tq), jnp.int32)
    for i in range(nsb):
        si = score[i:i + 1, :]
        beats = (si > score) | ((si == score) & (blk > i))
        rank = rank + beats.astype(jnp.int32)
    sel1 = jnp.where(visible & (rank < SLC_TOPK), 1.0, 0.0).astype(BF16)
    sel1 = jnp.concatenate([sel1, jnp.zeros((LANES - nsb, tq), BF16)], axis=0)
    sel = jnp.concatenate([sel1] * grp, axis=1)

    neg = jnp.full((1, wq), NEG_INF, F32)
    zero = jnp.zeros((1, wq), F32)
    krow = lax.broadcasted_iota(jnp.int32, (tq, 1), 0)
    lane4 = jnp.concatenate([lane_q] * grp, axis=1)

    def slc_logits(kt, causal, valid):
        k0 = pl.multiple_of(kt * tq, tq)
        k_t = ks_ref[0, 0, pl.ds(k0, tq), :].astype(BF16)
        v_t = vs_ref[0, 0, pl.ds(k0, tq), :].astype(BF16)
        kblk = lax.shift_right_logical(k0 + krow, 6)
        expand = jnp.where(kblk == lax.broadcasted_iota(jnp.int32, (1, LANES), 1), 1.0, 0.0).astype(BF16)
        mask = _dot(expand, sel) > 0.5
        if causal:
            mask = mask & (krow <= lane4)
        if valid is not None:
            mask = mask & valid
        return jnp.where(mask, _dot_nt(k_t, q4) + bt_ref[0, jnp.minimum(qi - kt, 2)], NEG_INF), v_t

    def slc_pair(a, b, m, l):
        (sa, va), (sb, vb) = a, b
        m_new = jnp.maximum(m, jnp.maximum(jnp.max(sa, axis=0, keepdims=True), jnp.max(sb, axis=0, keepdims=True)))
        alpha = jnp.exp(m - m_new)
        pa, pb = jnp.exp(sa - m_new), jnp.exp(sb - m_new)
        l = alpha * l + jnp.sum(pa, axis=0, keepdims=True) + jnp.sum(pb, axis=0, keepdims=True)
        acc_ref[...] = alpha * acc_ref[...] + _dot_tn(va, pa.astype(BF16)) + _dot_tn(vb, pb.astype(BF16))
        return m_new, l

    acc_ref[...] = jnp.zeros_like(acc_ref)
    m_s, l_s = lax.fori_loop(
        0, lax.shift_right_logical(qi, 1),
        lambda j, c: slc_pair(slc_logits(2 * j, False, None), slc_logits(2 * j + 1, False, None), *c), (neg, zero))
    m_s, l_s = slc_pair(slc_logits(jnp.maximum(qi - 1, 0), False, (qi & 1) == 1), slc_logits(qi, True, None), m_s, l_s)
    o_s = acc_ref[...] / l_s

    nwt = WINDOW // tq
    s_w, v_w = [], []
    for dt in range(nwt, -1, -1):
        kt = qi - dt
        k0 = pl.multiple_of(jnp.maximum(kt, 0) * tq, tq)
        k_t = kw_ref[0, 0, pl.ds(k0, tq), :].astype(BF16)
        v_w.append(vw_ref[0, 0, pl.ds(k0, tq), :].astype(BF16))
        s = _dot_nt(k_t, q4) + bt_ref[0, min(dt, 2)]
        if dt == nwt:
            s = jnp.where((krow > lane4) & (kt >= 0), s, NEG_INF)
        elif dt == 0:
            s = jnp.where(krow <= lane4, s, NEG_INF)
        else:
            s = jnp.where(kt >= 0, s, NEG_INF)
        s_w.append(s)
    m_w = functools.reduce(jnp.maximum, [jnp.max(s, axis=0, keepdims=True) for s in s_w])
    p_w = [jnp.exp(s - m_w) for s in s_w]
    l_w = functools.reduce(jnp.add, [jnp.sum(p, axis=0, keepdims=True) for p in p_w])
    o_w = functools.reduce(jnp.add, [_dot_tn(v, p.astype(BF16)) for v, p in zip(v_w, p_w)]) / l_w

    gates = jax.nn.sigmoid(g_ref[0, 0])
    outs = []
    for g in range(grp):
        sl = slice(g * tq, (g + 1) * tq)
        outs.append(gates[:, 3 * g:3 * g + 1] * o_c[:, sl].T + gates[:, 3 * g + 1:3 * g + 2] * o_s[:, sl].T
                    + gates[:, 3 * g + 2:3 * g + 3] * o_w[:, sl].T)
    o_ref[0] = jnp.concatenate(outs, axis=-1)


def nsa_bias_tables(tbl, t, ncp):
    tq = NSA_TILE
    kvh, grp = tbl.shape[1], tbl.shape[2]
    dist_c = jnp.arange(t)[None, :] - (CMP_STRIDE * jnp.arange(ncp) + CMP_BLOCK - 1)[:, None]
    bc = tbl[rel_bucket(dist_c)]
    bc = bc.reshape(ncp, t // tq, tq, kvh, grp).transpose(3, 1, 0, 4, 2).reshape(kvh, t // tq, ncp, grp * tq)
    cr = jnp.arange(tq)[None, :] - jnp.arange(tq)[:, None]
    tiles = jnp.stack([cr, cr + tq, jnp.full_like(cr, REL_MAX_DIST)])
    bt = tbl[rel_bucket(tiles)]
    bt = bt.transpose(3, 0, 1, 4, 2).reshape(kvh, 3, tq, grp * tq)
    return bc, bt


def nsa_prompt_pallas(y, q_off, ngt, kc_r, vc_r, ks_r, vs_r, kw_r, vw_r, lw, bias_tables):
    b, t, _ = y.shape
    kvh, grp, hd, tq = NSA_KV_HEADS, NSA_GROUP, HEAD_DIM, NSA_TILE
    qb = q_off // (grp * hd)
    kc, vc = compress_pair(kc_r, vc_r, lw)
    ncp = kc.shape[2]
    nsb = t // SLC_BLOCK
    bc, bt = bias_tables
    gates = ngt.reshape(b, t, kvh, grp * 3).transpose(0, 2, 1, 3)
    ci = jnp.arange(ncp)[None, :] * CMP_STRIDE
    sj = jnp.arange(LANES)[:, None] * SLC_BLOCK
    ov = ((ci < sj + SLC_BLOCK) & (ci + CMP_BLOCK > sj) & (jnp.arange(ncp)[None, :] < ncp - 1)
          & (jnp.arange(LANES)[:, None] < nsb)).astype(BF16)
    row_spec = pl.BlockSpec((1, 1, t, hd), lambda i, j, k: (i, j, 0, 0))
    cmp_spec = pl.BlockSpec((1, 1, ncp, hd), lambda i, j, k: (i, j, 0, 0))
    return pl.pallas_call(
        functools.partial(_nsa_t_kernel, nsb=nsb),
        grid=(b, kvh, t // tq),
        in_specs=[pl.BlockSpec((1, tq, grp * hd), lambda i, j, k: (i, k, qb + j)),
                  pl.BlockSpec((1, 1, tq, grp * 3), lambda i, j, k: (i, j, k, 0)),
                  cmp_spec, cmp_spec, row_spec, row_spec, row_spec, row_spec,
                  pl.BlockSpec((1, 1, ncp, grp * tq), lambda i, j, k: (j, k, 0, 0)),
                  pl.BlockSpec((1, 3, tq, grp * tq), lambda i, j, k: (j, 0, 0, 0)),
                  pl.BlockSpec(ov.shape, lambda i, j, k: (0, 0))],
        out_specs=pl.BlockSpec((1, tq, grp * hd), lambda i, j, k: (i, k, j)),
        out_shape=jax.ShapeDtypeStruct((b, t, kvh * grp * hd), F32),
        scratch_shapes=[pltpu.VMEM((hd, grp * tq), F32)],
        compiler_params=pltpu.CompilerParams(vmem_limit_bytes=VMEM_LIMIT),
        name="nsa_prompt",
    )(y, gates, kc, vc, ks_r, vs_r, kw_r, vw_r, bc, bt, ov)


SMP_TILE = 512


def _nsa_sample_kernel(pt_ref, q_ref, g_ref, nkc_ref, nvc_ref, nks_ref, nvs_ref, nkw_ref, nvw_ref, wk_ref, wv_ref,
                       pek_ref, pev_ref, w1k_ref, w1v_ref, w2k_ref, w2v_ref, bcs_ref, bsl_ref, bwn_ref, bnew_ref,
                       bfar_ref, ov_ref, gm_ref, ck_hbm, cv_hbm, sk_hbm, sv_hbm, o_ref,
                       buf_a, buf_b, sel_ref, sem, *, layer, past, nt):
    b = pl.program_id(0)
    h = pl.program_id(1)
    grp, hd = NSA_GROUP, HEAD_DIM
    npages = past // PAGE_SIZE
    nch = past // CMP_STRIDE
    nblk = ov_ref.shape[0]

    def page_copy(pool, buf, p, slot):
        return pltpu.make_async_copy(pool.at[layer, pt_ref[b, p], h],
                                     buf.at[pl.ds(pl.multiple_of(p * PAGE_SIZE, PAGE_SIZE), PAGE_SIZE), :],
                                     sem.at[slot])

    def start_pages(pool_a, pool_b):
        def body(p, c):
            page_copy(pool_a, buf_a, p, 0).start()
            page_copy(pool_b, buf_b, p, 1).start()
            return c
        lax.fori_loop(0, npages, body, 0)

    def wait_pages(pool_a, pool_b):
        def body(p, c):
            page_copy(pool_a, buf_a, p, 0).wait()
            page_copy(pool_b, buf_b, p, 1).wait()
            return c
        lax.fori_loop(0, npages, body, 0)

    start_pages(ck_hbm, cv_hbm)

    q = q_ref[0] * (hd ** -0.5)
    q4 = jnp.concatenate([q[:, g * hd:(g + 1) * hd] for g in range(grp)]
                         + [jnp.zeros((LANES - grp * nt, hd), F32)], axis=0).astype(BF16)
    lane = lax.broadcasted_iota(jnp.int32, (1, LANES), 1)
    tok = lane & (nt - 1)
    qpos = past + tok

    wait_pages(ck_hbm, cv_hbm)

    rows = CMP_STRIDE * hd

    def summaries(buf, new_ref, pe_ref, w1_ref, w2_ref):
        w1 = w1_ref[...]
        a0 = jnp.zeros((nch, CMP_HIDDEN), F32)
        a1 = jnp.zeros((nch, CMP_HIDDEN), F32)
        for j in range(CMP_STRIDE):
            xj = buf[pl.ds(j, nch, stride=CMP_STRIDE), :].astype(BF16)
            a0 = a0 + _dot(xj, w1[j * hd:(j + 1) * hd, :])
            a1 = a1 + _dot(xj, w1[rows + j * hd:rows + (j + 1) * hd, :])
        new = new_ref[0, 0].astype(BF16)
        a1_new = jnp.zeros((1, CMP_HIDDEN), F32)
        for j in range(nt):
            a1_new = a1_new + _dot(new, w1[rows + j * hd:rows + (j + 1) * hd, :])[j:j + 1, :]
        pe = jnp.broadcast_to(pe_ref[...], (8, pe_ref.shape[1])).astype(BF16)
        rid = lax.broadcasted_iota(jnp.int32, (nch, 1), 0)
        hid = _dot(pe, w1)[0:1, :] + a0 + jnp.where(rid == nch - 1, a1_new, pltpu.roll(a1, nch - 1, axis=0))
        return _dot(_gelu_tanh(hid).astype(BF16), w2_ref[...]).astype(BF16)

    kc = summaries(buf_a, nkc_ref, pek_ref, w1k_ref, w2k_ref)
    vc = summaries(buf_b, nvc_ref, pev_ref, w1v_ref, w2v_ref)
    start_pages(sk_hbm, sv_hbm)

    cend = CMP_STRIDE * lax.broadcasted_iota(jnp.int32, (nch, 1), 0) + (CMP_BLOCK - 1)
    mask_c = cend <= qpos
    lc = jnp.where(mask_c, _dot_nt(kc, q4) + bcs_ref[0], NEG_INF)
    mc = jnp.max(lc, axis=0, keepdims=True)
    pc = jnp.where(mask_c, jnp.exp(lc - mc), 0.0)
    lsum = jnp.sum(pc, axis=0, keepdims=True)
    pc = pc / jnp.where(lsum > 0.0, lsum, 1.0)
    o_c = _dot_tn(vc, pc.astype(BF16))

    ov = ov_ref[...]
    hi, mid, lo = _split3(pc)
    imp = _dot(ov, hi) + _dot(ov, mid) + _dot(ov, lo)
    gm = gm_ref[...]
    hi, mid, lo = _split3(imp)
    imp = _dot(hi, gm) + _dot(mid, gm) + _dot(lo, gm)
    blk = lax.broadcasted_iota(jnp.int32, (nblk, 1), 0)
    cur = lax.shift_right_logical(qpos, 6)
    forced = (blk == 0) | (blk == cur) | (blk == cur - 1)
    visible = blk * SLC_BLOCK <= qpos
    score = jnp.where(forced, -NEG_INF, imp)
    score = jnp.where(visible, score, NEG_INF)
    sel = jnp.zeros((nblk, LANES), F32)
    for _ in range(SLC_TOPK):
        mx = jnp.max(score, axis=0, keepdims=True)
        first = jnp.min(jnp.where(score == mx, blk, nblk), axis=0, keepdims=True)
        pick = blk == first
        sel = jnp.where(pick & (mx > 0.5 * NEG_INF), 1.0, sel)
        score = jnp.where(pick, -3.0e38, score)
    sel_ref[...] = sel

    wait_pages(sk_hbm, sv_hbm)

    bpt = SMP_TILE // SLC_BLOCK
    ntile = past // SMP_TILE

    def past_tile(kt):
        k0 = pl.multiple_of(kt * SMP_TILE, SMP_TILE)
        k_t = buf_a[pl.ds(k0, SMP_TILE), :].astype(BF16)
        v_t = buf_b[pl.ds(k0, SMP_TILE), :].astype(BF16)
        sblk = sel_ref[pl.ds(pl.multiple_of(kt * bpt, bpt), bpt), :]
        mask = jnp.concatenate([jnp.broadcast_to(sblk[i:i + 1, :], (SLC_BLOCK, LANES)) for i in range(bpt)],
                               axis=0) > 0.5
        bias = jnp.where(kt == ntile - 1, bsl_ref[0], bfar_ref[0])
        return jnp.where(mask, _dot_nt(k_t, q4) + bias, NEG_INF), v_t

    def pair(a, bb, m, l, acc):
        (sa, va), (sb, vb) = a, bb
        m_new = jnp.maximum(m, jnp.maximum(jnp.max(sa, axis=0, keepdims=True), jnp.max(sb, axis=0, keepdims=True)))
        alpha = jnp.exp(m - m_new)
        pa, pb = jnp.exp(sa - m_new), jnp.exp(sb - m_new)
        l = alpha * l + jnp.sum(pa, axis=0, keepdims=True) + jnp.sum(pb, axis=0, keepdims=True)
        acc = alpha * acc + _dot_tn(va, pa.astype(BF16)) + _dot_tn(vb, pb.astype(BF16))
        return m_new, l, acc

    neg = jnp.full((1, LANES), NEG_INF, F32)
    zero = jnp.zeros((1, LANES), F32)
    m_s, l_s, acc_s = lax.fori_loop(0, ntile // 2, lambda j, c: pair(past_tile(2 * j), past_tile(2 * j + 1), *c),
                                    (neg, zero, jnp.zeros((hd, LANES), F32)))
    jrow = lax.broadcasted_iota(jnp.int32, (nt, 1), 0)
    new_vis = jrow <= tok
    new_sel = sel_ref[pl.ds(past // SLC_BLOCK, 1), :] > 0.5
    s_new = jnp.where(new_vis & new_sel, _dot_nt(nks_ref[0, 0].astype(BF16), q4) + bnew_ref[0], NEG_INF)
    m_new = jnp.maximum(m_s, jnp.max(s_new, axis=0, keepdims=True))
    alpha = jnp.exp(m_s - m_new)
    p_new = jnp.exp(s_new - m_new)
    l_s = alpha * l_s + jnp.sum(p_new, axis=0, keepdims=True)
    o_s = (alpha * acc_s + _dot_tn(nvs_ref[0, 0].astype(BF16), p_new.astype(BF16))) / l_s

    wb = wk_ref.shape[3]
    wrow = lax.broadcasted_iota(jnp.int32, (wb, 1), 0)
    s_wc = jnp.where(wrow > tok + (wb - WINDOW), _dot_nt(wk_ref[0, 0, 0].astype(BF16), q4) + bwn_ref[0], NEG_INF)
    s_wn = jnp.where(new_vis, _dot_nt(nkw_ref[0, 0].astype(BF16), q4) + bnew_ref[0], NEG_INF)
    m_w = jnp.maximum(jnp.max(s_wc, axis=0, keepdims=True), jnp.max(s_wn, axis=0, keepdims=True))
    p_wc, p_wn = jnp.exp(s_wc - m_w), jnp.exp(s_wn - m_w)
    l_w = jnp.sum(p_wc, axis=0, keepdims=True) + jnp.sum(p_wn, axis=0, keepdims=True)
    o_w = (_dot_tn(wv_ref[0, 0, 0].astype(BF16), p_wc.astype(BF16))
           + _dot_tn(nvw_ref[0, 0].astype(BF16), p_wn.astype(BF16))) / l_w

    gates = jax.nn.sigmoid(g_ref[0, 0])
    oc_t, os_t, ow_t = o_c.T, o_s.T, o_w.T
    outs = []
    for g in range(grp):
        sl = slice(g * nt, (g + 1) * nt)
        outs.append(gates[:, 3 * g:3 * g + 1] * oc_t[sl] + gates[:, 3 * g + 1:3 * g + 2] * os_t[sl]
                    + gates[:, 3 * g + 2:3 * g + 3] * ow_t[sl])
    o_ref[0] = jnp.concatenate(outs, axis=-1)


def nsa_sample_tables(tbl, past, nt, wb):
    kvh, grp = tbl.shape[1], tbl.shape[2]
    t = jnp.arange(nt)

    def table(key_pos):
        v = tbl[rel_bucket(past + t[None, :] - key_pos[:, None])]
        v = v.transpose(2, 0, 3, 1).reshape(kvh, key_pos.shape[0], grp * nt)
        return jnp.pad(v, ((0, 0), (0, 0), (0, LANES - grp * nt)))
    nch = past // CMP_STRIDE
    bcs = table(CMP_STRIDE * jnp.arange(nch) + CMP_BLOCK - 1)
    bsl = table(past - SMP_TILE + jnp.arange(SMP_TILE))
    bwn = table(past - wb + jnp.arange(wb))
    bnew = table(past + jnp.arange(nt))
    bfar = table(jnp.full((1,), past - SMP_TILE - 1))
    return bcs, bsl, bwn, bnew, bfar


def nsa_sample_pallas(y, ngt, new_rows, lw, tables, layer, pools, win_k, win_v, page_table):
    b, nt, _ = y.shape
    kvh, grp, hd = NSA_KV_HEADS, NSA_GROUP, HEAD_DIM
    past = page_table.shape[1] * PAGE_SIZE
    assert nt & (nt - 1) == 0 and grp * nt <= LANES and nt <= CMP_STRIDE and past % (2 * SMP_TILE) == 0
    wb = win_k.shape[3]
    nch = past // CMP_STRIDE
    nsb = past // SLC_BLOCK + 1
    nblk = -(-nsb // 8) * 8
    qb = PROJ_OFF['nq'] // (grp * hd)
    gates = ngt.reshape(b, nt, kvh, grp * 3).transpose(0, 2, 1, 3)
    ci = jnp.arange(nch)[None, :] * CMP_STRIDE
    sj = jnp.arange(nblk)[:, None] * SLC_BLOCK
    ov = ((ci < sj + SLC_BLOCK) & (ci + CMP_BLOCK > sj) & (jnp.arange(nblk)[:, None] < nsb)).astype(BF16)
    ln = jnp.arange(LANES)
    gm = (((ln[:, None] & (nt - 1)) == (ln[None, :] & (nt - 1))) & (ln[:, None] < grp * nt)).astype(BF16)
    bcs, bsl, bwn, bnew, bfar = tables
    new_spec = pl.BlockSpec((1, 1, nt, hd), lambda i, j, pt: (i, j, 0, 0))
    win_spec = pl.BlockSpec((1, 1, 1, wb, hd), lambda i, j, pt: (layer, i, j, 0, 0))
    full = lambda a: pl.BlockSpec(a.shape, lambda i, j, pt: (0,) * a.ndim)
    per_head = lambda a: pl.BlockSpec((1,) + a.shape[1:], lambda i, j, pt: (j,) + (0,) * (a.ndim - 1))
    hbm = pl.BlockSpec(memory_space=pl.ANY)
    pek, pev = lw['cmp_pe_k'].reshape(1, -1), lw['cmp_pe_v'].reshape(1, -1)
    w1k, w1v = lw['cmp_w1_k'].astype(BF16), lw['cmp_w1_v'].astype(BF16)
    w2k, w2v = lw['cmp_w2_k'].astype(BF16), lw['cmp_w2_v'].astype(BF16)
    return pl.pallas_call(
        functools.partial(_nsa_sample_kernel, layer=layer, past=past, nt=nt),
        grid_spec=pltpu.PrefetchScalarGridSpec(
            num_scalar_prefetch=1,
            grid=(b, kvh),
            in_specs=[pl.BlockSpec((1, nt, grp * hd), lambda i, j, pt: (i, 0, qb + j)),
                      pl.BlockSpec((1, 1, nt, grp * 3), lambda i, j, pt: (i, j, 0, 0)),
                      new_spec, new_spec, new_spec, new_spec, new_spec, new_spec, win_spec, win_spec,
                      full(pek), full(pev), full(w1k), full(w1v), full(w2k), full(w2v),
                      per_head(bcs), per_head(bsl), per_head(bwn), per_head(bnew), per_head(bfar),
                      full(ov), full(gm), hbm, hbm, hbm, hbm],
            out_specs=pl.BlockSpec((1, nt, grp * hd), lambda i, j, pt: (i, 0, j)),
            scratch_shapes=[pltpu.VMEM((past, hd), F32), pltpu.VMEM((past, hd), F32),
                            pltpu.VMEM((nblk, LANES), F32), pltpu.SemaphoreType.DMA((2,))]),
        out_shape=jax.ShapeDtypeStruct((b, nt, kvh * grp * hd), F32),
        compiler_params=pltpu.CompilerParams(vmem_limit_bytes=VMEM_LIMIT),
        name="nsa_sample",
    )(page_table, y, gates, *new_rows, win_k, win_v, pek, pev, w1k, w1v, w2k, w2v, bcs, bsl, bwn, bnew, bfar,
      ov, gm, *pools)


def layer_norm(x, g, b):
    xf = x.astype(jnp.float32)
    mu = xf.mean(-1, keepdims=True)
    var = jnp.square(xf - mu).mean(-1, keepdims=True)
    return ((xf - mu) * lax.rsqrt(var + LN_EPS) * g + b).astype(x.dtype)


def masked_softmax(logits, mask):
    p = jax.nn.softmax(jnp.where(mask, logits.astype(jnp.float32), NEG_INF), axis=-1)
    return jnp.where(mask, p, 0.0)


def rel_bucket(dist):
    n = jnp.maximum(dist, 0)
    exact = REL_BUCKETS // 2
    big = exact + (jnp.log(jnp.maximum(n, 1).astype(jnp.float32) / exact)
                   / math.log(REL_MAX_DIST / exact) * (REL_BUCKETS - exact)).astype(jnp.int32)
    return jnp.where(n < exact, n, jnp.minimum(big, REL_BUCKETS - 1))


def gla_scan(q, k, v, log_a, s0):
    b, t = q.shape[:2]
    c = math.gcd(t, GLA_CHUNK)
    nc = t // c
    causal = jnp.tril(jnp.ones((c, c), dtype=bool))

    def to_chunks(a):
        return a.reshape(b, nc, c, *a.shape[2:]).swapaxes(0, 1)

    def step(s, inp):
        qc, kc, vc, ac = inp
        cum = jnp.cumsum(ac, axis=1)
        inter = jnp.einsum('bthk,bhkv->bthv', qc * jnp.exp(cum), s)
        diff = cum[:, :, None] - cum[:, None, :]
        decay = jnp.exp(jnp.where(causal[None, :, :, None, None], diff, -jnp.inf))
        att = jnp.einsum('bthk,bshk,btshk->bhts', qc, kc, decay)
        intra = jnp.einsum('bhts,bshv->bthv', att, vc)
        last = cum[:, -1]
        s = jnp.exp(last)[..., None] * s + jnp.einsum('bshk,bshv->bhkv', kc * jnp.exp(last[:, None] - cum), vc)
        return s, inter + intra

    s_fin, o = lax.scan(step, s0, (to_chunks(q), to_chunks(k), to_chunks(v), to_chunks(log_a)))
    return o.swapaxes(0, 1).reshape(b, t, *v.shape[2:]), s_fin


def gla_output(o, gate, norm_g):
    mu = o.mean(-1, keepdims=True)
    var = jnp.square(o - mu).mean(-1, keepdims=True)
    on = ((o - mu) * lax.rsqrt(var + LN_EPS)).reshape(*o.shape[:2], -1)
    return on * norm_g * jax.nn.silu(gate.astype(jnp.float32))


def short_conv(b_gate, c_gate, x_in, prev, w):
    u = c_gate * x_in
    up = jnp.concatenate([prev, u], axis=1)
    t = u.shape[1]
    y = up[:, 0:t] * w[0]
    for j in range(1, CONV_W):
        y = y + up[:, j:j + t] * w[j]
    return b_gate * y, up[:, t:]


def compress(k, pe, w1, w2):
    b, kvh, length, hd = k.shape
    n_chunks = length // CMP_STRIDE
    pieces = CMP_BLOCK // CMP_STRIDE
    n_blocks = n_chunks - pieces + 1
    ch = k.reshape(b, kvh, n_chunks, CMP_STRIDE * hd)
    rows = CMP_STRIDE * hd
    hid = pe.reshape(-1) @ w1
    for r in range(pieces):
        hid = hid + (ch @ w1[r * rows:(r + 1) * rows])[:, :, r:r + n_blocks]
    return jax.nn.gelu(hid) @ w2


def slc_overlap(n_cmp, n_slc):
    start = jnp.arange(n_cmp) * CMP_STRIDE
    blk = jnp.arange(n_slc) * SLC_BLOCK
    return ((start[:, None] < blk[None, :] + SLC_BLOCK) & (start[:, None] + CMP_BLOCK > blk[None, :])).astype(jnp.float32)


def nsa_attend(q, qpos, gates, kc, vc, overlap, gather_slc, kw, vw, wpos, tbl):
    b, nq, kvh, grp, hd = q.shape
    scale = hd ** -0.5
    f32 = jnp.float32
    hi = jnp.arange(kvh)[:, None, None]
    cmp_end = CMP_STRIDE * jnp.arange(kc.shape[2]) + CMP_BLOCK - 1
    dist_c = qpos[:, None] - cmp_end[None, :]
    def bias(dist, spec):
        onehot = jax.nn.one_hot(rel_bucket(dist), REL_BUCKETS, dtype=f32)
        return jnp.einsum(spec, onehot, tbl, precision=lax.Precision.HIGHEST)

    lc = jnp.einsum('bqhgd,bhnd->bqhgn', q, kc).astype(f32) * scale + bias(dist_c, 'qnk,khg->qhgn')
    pc = masked_softmax(lc, (dist_c >= 0)[:, None, None, :])
    o_c = jnp.einsum('bqhgn,bhnd->bqhgd', pc, vc)
    imp = jnp.einsum('bqhn,ns->bqhs', pc.sum(3), overlap)
    blk = jnp.arange(overlap.shape[1])
    cur = (qpos // SLC_BLOCK)[:, None]
    forced = (blk[None] == 0) | (blk[None] == cur) | (blk[None] == cur - 1)
    visible = blk[None] * SLC_BLOCK <= qpos[:, None]
    score = jnp.where(forced[:, None], -NEG_INF, imp)
    score = jnp.where(visible[:, None], score, NEG_INF)
    top_s, idx = lax.top_k(score, min(SLC_TOPK, overlap.shape[1]))
    ks, vs = gather_slc(idx)
    kpos = idx[..., None] * SLC_BLOCK + jnp.arange(SLC_BLOCK)
    dist_s = qpos[None, :, None, None, None] - kpos
    mask_s = (dist_s >= 0) & (top_s > 0.5 * NEG_INF)[..., None]
    ls = jnp.einsum('bqhgd,bqhnsd->bqhgns', q, ks).astype(f32) * scale + bias(dist_s, 'bqhnsk,khg->bqhgns')
    ps = masked_softmax(ls.reshape(b, nq, kvh, grp, -1), mask_s.reshape(b, nq, kvh, 1, -1))
    o_s = jnp.einsum('bqhgm,bqhmd->bqhgd', ps, vs.reshape(b, nq, kvh, -1, hd))
    dist_w = qpos[:, None] - wpos[None, :]
    mask_w = (dist_w >= 0) & (dist_w < WINDOW) & (wpos >= 0)[None, :]
    lwin = jnp.einsum('bqhgd,bhwd->bqhgw', q, kw).astype(f32) * scale + bias(dist_w, 'qwk,khg->qhgw')
    pw = masked_softmax(lwin, mask_w[:, None, None, :])
    o_w = jnp.einsum('bqhgw,bhwd->bqhgd', pw, vw)
    o = gates[..., 0:1] * o_c + gates[..., 1:2] * o_s + gates[..., 2:3] * o_w
    return o.astype(q.dtype)


def nsa_prompt(q, gates, kc_r, vc_r, ks_r, vs_r, kw_r, vw_r, lw, tbl):
    b, s = q.shape[:2]
    kc = compress(kc_r, lw['cmp_pe_k'], lw['cmp_w1_k'], lw['cmp_w2_k'])
    vc = compress(vc_r, lw['cmp_pe_v'], lw['cmp_w1_v'], lw['cmp_w2_v'])
    nsb = s // SLC_BLOCK
    ks_blk = ks_r.reshape(b, NSA_KV_HEADS, nsb, SLC_BLOCK, HEAD_DIM)
    vs_blk = vs_r.reshape(b, NSA_KV_HEADS, nsb, SLC_BLOCK, HEAD_DIM)
    bi = jnp.arange(b)[:, None, None, None]
    hi = jnp.arange(NSA_KV_HEADS)[None, None, :, None]

    def gather_slc(idx):
        return ks_blk[bi, hi, idx], vs_blk[bi, hi, idx]

    overlap = slc_overlap(kc.shape[2], nsb)
    kw_pad = jnp.pad(kw_r, ((0, 0), (0, 0), (WINDOW, 0), (0, 0)))
    vw_pad = jnp.pad(vw_r, ((0, 0), (0, 0), (WINDOW, 0), (0, 0)))
    n_chunks = s // NSA_QCHUNK
    span = WINDOW + NSA_QCHUNK

    def chunks(a):
        return a.reshape(b, n_chunks, NSA_QCHUNK, *a.shape[2:]).swapaxes(0, 1)

    def body(inp):
        qc, gc, c0 = inp
        return nsa_attend(qc, c0 + jnp.arange(NSA_QCHUNK), gc, kc, vc, overlap, gather_slc,
                          lax.dynamic_slice_in_dim(kw_pad, c0, span, axis=2),
                          lax.dynamic_slice_in_dim(vw_pad, c0, span, axis=2),
                          c0 - WINDOW + jnp.arange(span), tbl)

    o = lax.map(body, (chunks(q), chunks(gates), jnp.arange(n_chunks) * NSA_QCHUNK))
    o = o.swapaxes(0, 1).reshape(b, s, *q.shape[2:])
    wb = min(WINDOW, s)
    return o, kw_r[:, :, s - wb:], vw_r[:, :, s - wb:]


def nsa_sample(q, gates, kc_r, vc_r, ks_r, vs_r, kw_r, vw_r, lw, tbl, layer, pool_ck, pool_cv, pool_sk, pool_sv, win_k, win_v, page_table):
    b, t = q.shape[:2]
    past = page_table.shape[1] * PAGE_SIZE
    t_cmp = -(-t // CMP_STRIDE) * CMP_STRIDE
    nnb = -(-t // SLC_BLOCK)
    npb = past // SLC_BLOCK
    bpp = PAGE_SIZE // SLC_BLOCK

    def past_rows(pool):
        g = pool[layer, page_table]
        return g.transpose(0, 2, 1, 3, 4).reshape(b, NSA_KV_HEADS, past, HEAD_DIM)

    def pad_t(a, n):
        return jnp.pad(a, ((0, 0), (0, 0), (0, n - a.shape[2]), (0, 0)))

    kc = compress(jnp.concatenate([past_rows(pool_ck), pad_t(kc_r, t_cmp)], axis=2), lw['cmp_pe_k'], lw['cmp_w1_k'], lw['cmp_w2_k'])
    vc = compress(jnp.concatenate([past_rows(pool_cv), pad_t(vc_r, t_cmp)], axis=2), lw['cmp_pe_v'], lw['cmp_w1_v'], lw['cmp_w2_v'])
    new_ks = pad_t(ks_r, nnb * SLC_BLOCK).reshape(b, NSA_KV_HEADS, nnb, SLC_BLOCK, HEAD_DIM)
    new_vs = pad_t(vs_r, nnb * SLC_BLOCK).reshape(b, NSA_KV_HEADS, nnb, SLC_BLOCK, HEAD_DIM)
    bi = jnp.arange(b)[:, None, None, None]
    hi = jnp.arange(NSA_KV_HEADS)[None, None, :, None]
    offs = jnp.arange(SLC_BLOCK)

    def gather_slc(idx):
        in_past = (idx < npb)[..., None, None]
        pidx = jnp.minimum(idx, npb - 1)
        page = page_table[bi, pidx // bpp][..., None]
        rows = ((pidx % bpp) * SLC_BLOCK)[..., None] + offs
        nidx = jnp.clip(idx - npb, 0, nnb - 1)
        ks = jnp.where(in_past, pool_sk[layer, page, hi[..., None], rows], new_ks[bi, hi, nidx])
        vs = jnp.where(in_past, pool_sv[layer, page, hi[..., None], rows], new_vs[bi, hi, nidx])
        return ks, vs

    kw = jnp.concatenate([win_k.astype(kw_r.dtype), kw_r], axis=2)
    vw = jnp.concatenate([win_v.astype(vw_r.dtype), vw_r], axis=2)
    wb = win_k.shape[2]
    o = nsa_attend(q, past + jnp.arange(t), gates, kc, vc, slc_overlap(kc.shape[2], npb + nnb), gather_slc,
                   kw, vw, past - wb + jnp.arange(wb + t), tbl)
    return o, kw[:, :, -wb:], vw[:, :, -wb:]


def token_mixer(h, lw, gla_s0, conv_prev, nsa_fn):
    b, t, _ = h.shape
    f32 = jnp.float32
    (gq, gk, gv, gg, ga, cb, cc, cx, nq, nkc, nvc, nks, nvs, nkw, nvw, ngt, mgt) = jnp.split(
        in_proj(h, lw['w_in']), np.cumsum(IN_SIZES)[:-1].tolist(), axis=-1)
    q = gq.reshape(b, t, GLA_HEADS, GLA_DK).astype(f32) * GLA_DK ** -0.5
    k = gk.reshape(b, t, GLA_HEADS, GLA_DK).astype(f32)
    v = gv.reshape(b, t, GLA_HEADS, GLA_DV).astype(f32)
    log_a = (jax.nn.log_sigmoid((ga @ lw['gla_wa2'] + lw['gla_ba']).astype(f32)) / GLA_TAU).reshape(b, t, GLA_HEADS, GLA_DK)
    o, s_fin = gla_scan(q, k, v, log_a, gla_s0.astype(f32))
    o_gla = gla_output(o, gg, lw['gla_norm_g']).astype(h.dtype)
    o_conv, conv_state = short_conv(cb, cc, cx, conv_prev.astype(h.dtype), lw['conv_w'])

    def rows(a):
        return a.reshape(b, t, NSA_KV_HEADS, HEAD_DIM).transpose(0, 2, 1, 3)
    kc_r, vc_r, ks_r, vs_r, kw_r, vw_r = [rows(a) for a in (nkc, nvc, nks, nvs, nkw, nvw)]
    q_n = nq.reshape(b, t, NSA_KV_HEADS, NSA_GROUP, HEAD_DIM)
    g_n = jax.nn.sigmoid(ngt).reshape(b, t, NSA_KV_HEADS, NSA_GROUP, 3)
    o_n, win_k, win_v = nsa_fn(q_n, g_n, kc_r, vc_r, ks_r, vs_r, kw_r, vw_r)
    s_a, s_b, s_c = jnp.split(jax.nn.sigmoid(mgt), 3, axis=-1)
    merged = (s_a * (o_gla @ lw['w_br_gla']) + s_b * (o_conv @ lw['w_br_conv'])
              + s_c * (o_n.reshape(b, t, NSA_Q) @ lw['w_br_nsa']))
    return merged @ lw['w_o'], (kc_r, vc_r, ks_r, vs_r, win_k, win_v, s_fin.astype(h.dtype), conv_state)


def moe_dispatch(hf, idx, wts, wg, wu, wd):
    n, d = hf.shape
    a = n * TOP_K
    fe = idx.reshape(-1)
    ft = jnp.arange(a) // TOP_K
    fw = wts.reshape(-1)
    order = jnp.argsort(fe)
    se, st, sw = fe[order], ft[order], fw[order]
    counts = jnp.zeros((N_EXPERTS,), jnp.int32).at[fe].add(1)
    starts = jnp.cumsum(counts) - counts
    padded = (counts + MOE_BLOCK - 1) // MOE_BLOCK * MOE_BLOCK
    pend = jnp.cumsum(padded)
    pstart = pend - padded
    dest = pstart[se] + jnp.arange(a) - starts[se]
    nb = -(-a // MOE_BLOCK) + N_EXPERTS
    rows = nb * MOE_BLOCK
    row_tok = jnp.full((rows,), n, jnp.int32).at[dest].set(st)
    row_w = jnp.zeros((rows,), hf.dtype).at[dest].set(sw)
    blk_e = jnp.minimum(jnp.searchsorted(pend, jnp.arange(nb) * MOE_BLOCK, side='right'), N_EXPERTS - 1)
    h_pad = jnp.concatenate([hf, jnp.zeros((1, d), hf.dtype)], axis=0)

    def expert_block(inp):
        tok, e = inp
        xb = h_pad[tok]
        return (jax.nn.silu(xb @ wg[e]) * (xb @ wu[e])) @ wd[e]

    y = lax.map(expert_block, (row_tok.reshape(nb, MOE_BLOCK), blk_e))
    return jnp.zeros((n + 1, d), y.dtype).at[row_tok].add(y.reshape(rows, d) * row_w[:, None])[:n]


def moe(h, lw):
    b, t, d = h.shape
    n = b * t
    hf = h.reshape(n, d)
    s = jax.nn.sigmoid((hf @ lw['w_router']).astype(jnp.float32))
    sel = s + lw['router_bias'].astype(jnp.float32)
    per = N_EXPERTS // N_GROUPS
    gscore = lax.top_k(sel.reshape(n, N_GROUPS, per), 2)[0].sum(-1)
    gidx = lax.top_k(gscore, TOPK_GROUPS)[1]
    gmask = jax.nn.one_hot(gidx, N_GROUPS).sum(1) > 0
    sel = jnp.where(jnp.repeat(gmask, per, axis=1), sel, NEG_INF)
    idx = lax.top_k(sel, TOP_K)[1]
    w = jnp.take_along_axis(s, idx, axis=1)
    w = w / w.sum(-1, keepdims=True) * ROUTED_SCALE
    routed = moe_dispatch(hf, idx, w.astype(h.dtype), lw['w_exp_gate'], lw['w_exp_up'], lw['w_exp_down'])
    shared = (jax.nn.silu(hf @ lw['w_sh_gate']) * (hf @ lw['w_sh_up'])) @ lw['w_sh_down']
    return (routed + shared).reshape(b, t, d)


def block(x, c, lw, mixer):
    mod = (jax.nn.silu(c) @ lw['w_ada'] + lw['b_ada'])[:, None, :]
    sh1, sc1, g1, sh2, sc2, g2 = jnp.split(mod, 6, axis=-1)
    mix, state = mixer(x * (1 + sc1) + sh1)
    x = layer_norm(DN_ALPHA * x + g1 * mix, lw['ln1_g'], lw['ln1_b'])
    x = layer_norm(DN_ALPHA * x + g2 * moe(x * (1 + sc2) + sh2, lw), lw['ln2_g'], lw['ln2_b'])
    return x, state


def kernel(x_prompt, x_sample, cache_cmp_k, cache_cmp_v, cache_slc_k, cache_slc_v, cache_win_k, cache_win_v,
           state_gla, state_conv, page_table, c_prompt, c_sample, w_ada, b_ada, w_in, gla_wa2, gla_ba,
           gla_norm_g, conv_w, cmp_pe_k, cmp_pe_v, cmp_w1_k, cmp_w2_k, cmp_w1_v, cmp_w2_v, rel_bias,
           w_br_gla, w_br_conv, w_br_nsa, w_o, ln1_g, ln1_b, ln2_g, ln2_b, w_router, router_bias,
           w_exp_gate, w_exp_up, w_exp_down, w_sh_gate, w_sh_up, w_sh_down):
    tbl = rel_bias.reshape(REL_BUCKETS, NSA_KV_HEADS, NSA_GROUP)
    xp, xs = x_prompt, x_sample
    p_states, s_states = [], []
    for l in range(DEPTH):
        lw = {
            'w_ada': w_ada[l], 'b_ada': b_ada[l], 'w_in': w_in[l], 'gla_wa2': gla_wa2[l], 'gla_ba': gla_ba[l],
            'gla_norm_g': gla_norm_g[l], 'conv_w': conv_w[l], 'cmp_pe_k': cmp_pe_k[l], 'cmp_pe_v': cmp_pe_v[l],
            'cmp_w1_k': cmp_w1_k[l], 'cmp_w2_k': cmp_w2_k[l], 'cmp_w1_v': cmp_w1_v[l], 'cmp_w2_v': cmp_w2_v[l],
            'w_br_gla': w_br_gla[l], 'w_br_conv': w_br_conv[l], 'w_br_nsa': w_br_nsa[l], 'w_o': w_o[l],
            'ln1_g': ln1_g[l], 'ln1_b': ln1_b[l], 'ln2_g': ln2_g[l], 'ln2_b': ln2_b[l],
            'w_router': w_router[l], 'router_bias': router_bias[l], 'w_exp_gate': w_exp_gate[l],
            'w_exp_up': w_exp_up[l], 'w_exp_down': w_exp_down[l], 'w_sh_gate': w_sh_gate[l],
            'w_sh_up': w_sh_up[l], 'w_sh_down': w_sh_down[l],
        }
        prompt_mixer = functools.partial(
            token_mixer, lw=lw,
            gla_s0=jnp.zeros((xp.shape[0], GLA_HEADS, GLA_DK, GLA_DV), jnp.float32),
            conv_prev=jnp.zeros((xp.shape[0], CONV_W - 1, CONV_DIM), xp.dtype),
            nsa_fn=functools.partial(nsa_prompt, lw=lw, tbl=tbl))
        sample_mixer = functools.partial(
            token_mixer, lw=lw, gla_s0=state_gla[l], conv_prev=state_conv[l],
            nsa_fn=functools.partial(nsa_sample, lw=lw, tbl=tbl, layer=l, pool_ck=cache_cmp_k, pool_cv=cache_cmp_v,
                                     pool_sk=cache_slc_k, pool_sv=cache_slc_v, win_k=cache_win_k[l],
                                     win_v=cache_win_v[l], page_table=page_table))
        xp, st = block(xp, c_prompt, lw, prompt_mixer)
        p_states.append(st)
        xs, st = block(xs, c_sample, lw, sample_mixer)
        s_states.append(st)
    p_out = [jnp.stack(z) for z in zip(*p_states)]
    s_out = [jnp.stack(z) for z in zip(*s_states)]
    return (xp, xs, *p_out, *s_out)


MERGE_ROWS = 256
CONV_HALO = 8


def _layer_norm_rows(z, g, b):
    mu = jnp.mean(z, axis=-1, keepdims=True)
    var = jnp.mean(jnp.square(z - mu), axis=-1, keepdims=True)
    return (z - mu) * lax.rsqrt(var + LN_EPS) * g + b


def _rank_rows(v):
    r = v.shape[0]
    rid = lax.broadcasted_iota(jnp.int32, (r, 1), 0)
    rank = jnp.zeros(v.shape, jnp.int32)
    for j in range(r):
        vj = v[j:j + 1, :]
        rank = rank + ((vj > v) | ((vj == v) & (rid > j))).astype(jnp.int32)
    return rank


def _route_cols(s, bias):
    n = s.shape[1]
    per = N_EXPERTS // N_GROUPS
    sel = s + bias
    gs = []
    for g in range(N_GROUPS):
        sg = sel[g * per:(g + 1) * per, :]
        gs.append(jnp.sum(jnp.where(_rank_rows(sg) < 2, sg, 0.0), axis=0, keepdims=True))
    gkeep = _rank_rows(jnp.concatenate(gs, axis=0)) < TOPK_GROUPS
    keep = jnp.concatenate([jnp.broadcast_to(gkeep[g:g + 1, :], (per, n)) for g in range(N_GROUPS)], axis=0)
    rank = _rank_rows(jnp.where(keep, sel, NEG_INF))
    eid = lax.broadcasted_iota(jnp.int32, (N_EXPERTS, 1), 0)
    ids, ws = [], []
    for k in range(TOP_K):
        hit = rank == k
        ids.append(jnp.sum(jnp.where(hit, eid, 0), axis=0, keepdims=True))
        ws.append(jnp.sum(jnp.where(hit, s, 0.0), axis=0, keepdims=True))
    w = jnp.concatenate(ws, axis=0)
    return jnp.concatenate(ids, axis=0), w / jnp.sum(w, axis=0, keepdims=True) * ROUTED_SCALE


def _merge_kernel(x_ref, cb_ref, cc_ref, cx_ref, ccp_ref, cxp_ref, prev_ref, ma_ref, mb_ref, mc_ref, og_ref, on_ref,
                  cw_ref, wg_ref, wc_ref, wn_ref, wo_ref, g1_ref, sc2_ref, sh2_ref, lng_ref, lnb_ref, wr_ref, rb_ref,
                  x1_ref, h2_ref, idx_ref, rw_ref, cst_ref):
    j = pl.program_id(1)
    tm = x_ref.shape[1]
    u = cc_ref[0] * cx_ref[0]
    halo = jnp.where(j == 0, prev_ref[0], ccp_ref[0] * cxp_ref[0])
    p1 = halo[CONV_HALO - 1:CONV_HALO, :]
    p2 = halo[CONV_HALO - 2:CONV_HALO - 1, :]
    rid = lax.broadcasted_iota(jnp.int32, (tm, 1), 0)
    u1 = jnp.where(rid == 0, p1, pltpu.roll(u, 1, axis=0))
    u2 = jnp.where(rid == 0, p2, jnp.where(rid == 1, p1, pltpu.roll(u, 2, axis=0)))
    cw = cw_ref[...]
    o_conv = cb_ref[0] * (u2 * cw[0:1, :] + u1 * cw[1:2, :] + u * cw[2:3, :])
    cst_ref[0] = u[tm - CONV_HALO:tm, :]
    merged = (jax.nn.sigmoid(ma_ref[0]) * _dot(og_ref[0].astype(BF16), wg_ref[...])
              + jax.nn.sigmoid(mb_ref[0]) * _dot(o_conv.astype(BF16), wc_ref[...])
              + jax.nn.sigmoid(mc_ref[0]) * _dot(on_ref[0].astype(BF16), wn_ref[...]))
    mix = _dot(merged.astype(BF16), wo_ref[...])
    x1 = _layer_norm_rows(DN_ALPHA * x_ref[0] + g1_ref[0] * mix, lng_ref[...], lnb_ref[...])
    x1_ref[0] = x1
    h2 = x1 * (1.0 + sc2_ref[0]) + sh2_ref[0]
    h2_ref[0] = h2.astype(h2_ref.dtype)
    s = jax.nn.sigmoid(lax.dot_general(wr_ref[...], h2, (((1,), (1,)), ((), ())),
                                       precision=lax.Precision.HIGHEST, preferred_element_type=F32))
    idx_ref[0], rw_ref[0] = _route_cols(s, rb_ref[...])


def merge_pallas(x, y, o_gla, o_n, conv_prev, g1, sc2, sh2, lw):
    b, t, d = x.shape
    tm = math.gcd(t, MERGE_ROWS)
    halo = CONV_HALO
    hb = tm // halo
    prev = jnp.concatenate([jnp.zeros((b, halo - (CONV_W - 1), d), F32), conv_prev.astype(F32)], axis=1)
    col = lambda name, k=0: PROJ_OFF[name] // d + k
    tile = lambda cb_: pl.BlockSpec((1, tm, d), lambda i, j: (i, j, cb_))
    halo_spec = lambda cb_: pl.BlockSpec((1, halo, d), lambda i, j: (i, jnp.maximum(j * hb - 1, 0), cb_))
    vec = pl.BlockSpec((1, 1, d), lambda i, j: (i, 0, 0))
    const = lambda a: pl.BlockSpec(a.shape, lambda i, j: (0,) * a.ndim)
    wts = [lw['w_br_gla'].astype(BF16), lw['w_br_conv'].astype(BF16), lw['w_br_nsa'].astype(BF16), lw['w_o'].astype(BF16)]
    lng, lnb = lw['ln1_g'].reshape(1, d), lw['ln1_b'].reshape(1, d)
    wr, rb = lw['w_router'].T, lw['router_bias'].reshape(N_EXPERTS, 1).astype(F32)
    topk = pl.BlockSpec((1, TOP_K, tm), lambda i, j: (i, 0, j))
    x1, h2, idx, rw, cst = pl.pallas_call(
        _merge_kernel,
        grid=(b, t // tm),
        in_specs=[tile(0), tile(col('cb')), tile(col('cc')), tile(col('cx')), halo_spec(col('cc')), halo_spec(col('cx')),
                  pl.BlockSpec((1, halo, d), lambda i, j: (i, 0, 0)),
                  tile(col('mgt', 0)), tile(col('mgt', 1)), tile(col('mgt', 2)), tile(0), tile(0),
                  const(lw['conv_w'])] + [const(w) for w in wts] + [vec, vec, vec, const(lng), const(lnb),
                                                                    const(wr), const(rb)],
        out_specs=[tile(0), tile(0), topk, topk, pl.BlockSpec((1, halo, d), lambda i, j: (i, 0, 0))],
        out_shape=[jax.ShapeDtypeStruct((b, t, d), F32), jax.ShapeDtypeStruct((b, t, d), BF16 if tm % 16 == 0 else F32),
                   jax.ShapeDtypeStruct((b, TOP_K, t), jnp.int32), jax.ShapeDtypeStruct((b, TOP_K, t), F32),
                   jax.ShapeDtypeStruct((b, halo, d), F32)],
        compiler_params=pltpu.CompilerParams(vmem_limit_bytes=VMEM_LIMIT),
        name="merge_ln1",
    )(x, y, y, y, y, y, prev, y, y, y, o_gla, o_n, lw['conv_w'], *wts,
      g1.reshape(b, 1, d), sc2.reshape(b, 1, d), sh2.reshape(b, 1, d), lng, lnb, wr, rb)
    to_rows = lambda a: a.transpose(0, 2, 1).reshape(b * t, TOP_K)
    return x1, h2, to_rows(idx), to_rows(rw), cst[:, halo - (CONV_W - 1):]


def _expert_act(xb, wg, wu):
    hg = _dot(xb, wg)
    return hg * jax.nn.sigmoid(hg) * _dot(xb, wu)


def _grouped_expert_kernel(be_ref, x_ref, wg_ref, wu_ref, wd_ref, o_ref):
    act = _expert_act(x_ref[...], wg_ref[0], wu_ref[0])
    o_ref[...] = _dot(act.astype(BF16), wd_ref[0]).astype(o_ref.dtype)


def moe_routed_sorted(h2, idx, w, lw):
    n, d = h2.shape
    a = n * TOP_K
    onehot = idx[:, :, None] == jnp.arange(N_EXPERTS, dtype=jnp.int32)[None, None, :]
    hit = onehot.any(axis=1).astype(F32)
    ct = math.gcd(n, 512)
    hit_t = hit.reshape(n // ct, ct, N_EXPERTS)
    within = jnp.einsum('ts,nse->nte', jnp.tril(jnp.ones((ct, ct), F32), -1), hit_t)
    tile_tot = hit_t.sum(1)
    before = jnp.cumsum(tile_tot, axis=0) - tile_tot
    pos = (within + before[:, None, :]).reshape(n, N_EXPERTS).astype(jnp.int32)
    counts = tile_tot.sum(0).astype(jnp.int32)
    padded = (counts + MOE_BLOCK - 1) // MOE_BLOCK * MOE_BLOCK
    pend = jnp.cumsum(padded)
    pstart = pend - padded
    slot = jnp.sum(jnp.where(onehot, (pos + pstart[None, :])[:, None, :], 0), axis=-1)
    nb = -(-a // MOE_BLOCK) + N_EXPERTS
    rows = nb * MOE_BLOCK
    row_tok = jnp.zeros((rows,), jnp.int32).at[slot.reshape(-1)].set(jnp.arange(a, dtype=jnp.int32) // TOP_K)
    blk_e = jnp.minimum(jnp.searchsorted(pend, jnp.arange(nb) * MOE_BLOCK, side='right'), N_EXPERTS - 1).astype(jnp.int32)
    xs = h2.astype(BF16)[row_tok]
    wg, wu, wd = lw['w_exp_gate'].astype(BF16), lw['w_exp_up'].astype(BF16), lw['w_exp_down'].astype(BF16)
    hdim = wg.shape[2]
    y = pl.pallas_call(
        _grouped_expert_kernel,
        grid_spec=pltpu.PrefetchScalarGridSpec(
            num_scalar_prefetch=1,
            grid=(nb,),
            in_specs=[pl.BlockSpec((MOE_BLOCK, d), lambda i, be: (i, 0)),
                      pl.BlockSpec((1, d, hdim), lambda i, be: (be[i], 0, 0)),
                      pl.BlockSpec((1, d, hdim), lambda i, be: (be[i], 0, 0)),
                      pl.BlockSpec((1, hdim, d), lambda i, be: (be[i], 0, 0))],
            out_specs=pl.BlockSpec((MOE_BLOCK, d), lambda i, be: (i, 0))),
        out_shape=jax.ShapeDtypeStruct((rows, d), BF16),
        name="moe_grouped",
    )(blk_e, xs, wg, wu, wd)
    return y[slot.T.reshape(-1)].reshape(TOP_K, n, d), w


def _dense_expert_kernel(x_ref, wt_ref, wg_ref, wu_ref, wd_ref, o_ref):
    @pl.when(pl.program_id(0) == 0)
    def _():
        o_ref[...] = jnp.zeros_like(o_ref)
    act = _expert_act(x_ref[...].astype(BF16), wg_ref[0].astype(BF16), wu_ref[0].astype(BF16))
    o_ref[...] += _dot((act * wt_ref[0]).astype(BF16), wd_ref[0].astype(BF16))


def moe_routed_dense(h2, idx, w, lw):
    n, d = h2.shape
    wdense = jnp.zeros((n, N_EXPERTS), F32).at[jnp.arange(n)[:, None], idx].add(w)
    wt = wdense.T.reshape(N_EXPERTS, n, 1)
    hdim = lw['w_exp_gate'].shape[2]
    routed = pl.pallas_call(
        _dense_expert_kernel,
        grid=(N_EXPERTS,),
        in_specs=[pl.BlockSpec((n, d), lambda e: (0, 0)),
                  pl.BlockSpec((1, n, 1), lambda e: (e, 0, 0)),
                  pl.BlockSpec((1, d, hdim), lambda e: (e, 0, 0)),
                  pl.BlockSpec((1, d, hdim), lambda e: (e, 0, 0)),
                  pl.BlockSpec((1, hdim, d), lambda e: (e, 0, 0))],
        out_specs=pl.BlockSpec((n, d), lambda e: (0, 0)),
        out_shape=jax.ShapeDtypeStruct((n, d), F32),
        name="moe_dense",
    )(h2, wt, lw['w_exp_gate'], lw['w_exp_up'], lw['w_exp_down'])
    return routed.reshape(1, n, d), jnp.ones((n, 1), F32)


def _moe_tail_kernel(x1_ref, h2_ref, rw_ref, g2_ref, wg_ref, wu_ref, wd_ref, lng_ref, lnb_ref, *rest):
    r_refs, o_ref = rest[:-1], rest[-1]
    rw = rw_ref[0]
    routed = rw[:, 0:1] * r_refs[0][0, 0].astype(F32)
    for k in range(1, len(r_refs)):
        routed = routed + rw[:, k:k + 1] * r_refs[k][0, 0].astype(F32)
    act = _expert_act(h2_ref[0].astype(BF16), wg_ref[...], wu_ref[...])
    shared = _dot(act.astype(BF16), wd_ref[...])
    o_ref[0] = _layer_norm_rows(DN_ALPHA * x1_ref[0] + g2_ref[0] * (routed + shared), lng_ref[...], lnb_ref[...])


def moe_tail_pallas(x1, h2, rows, rw, g2, lw):
    b, t, d = x1.shape
    nk = rw.shape[1]
    tm = math.gcd(t, MERGE_ROWS)
    tile = pl.BlockSpec((1, tm, d), lambda i, j: (i, j, 0))
    const = lambda a: pl.BlockSpec(a.shape, lambda i, j: (0,) * a.ndim)
    wg, wu, wd = lw['w_sh_gate'].astype(BF16), lw['w_sh_up'].astype(BF16), lw['w_sh_down'].astype(BF16)
    lng, lnb = lw['ln2_g'].reshape(1, d), lw['ln2_b'].reshape(1, d)
    rows = rows.reshape(nk, b, t, d)
    row_specs = [pl.BlockSpec((1, 1, tm, d), functools.partial(lambda i, j, k: (k, i, j, 0), k=k)) for k in range(nk)]
    return pl.pallas_call(
        _moe_tail_kernel,
        grid=(b, t // tm),
        in_specs=[tile, tile, pl.BlockSpec((1, tm, nk), lambda i, j: (i, j, 0)),
                  pl.BlockSpec((1, 1, d), lambda i, j: (i, 0, 0)),
                  const(wg), const(wu), const(wd), const(lng), const(lnb)] + row_specs,
        out_specs=tile,
        out_shape=jax.ShapeDtypeStruct((b, t, d), F32),
        compiler_params=pltpu.CompilerParams(vmem_limit_bytes=VMEM_LIMIT),
        name="moe_tail_ln2",
    )(x1, h2, rw.reshape(b, t, nk), g2.reshape(b, 1, d), wg, wu, wd, lng, lnb, *([rows] * nk))


def layer_block(x, c, lw, gla_s0, conv_prev, nsa_fn, sorted_moe):
    b, t, d = x.shape
    mod = jax.nn.silu(c) @ lw['w_ada'] + lw['b_ada']
    sh1, sc1, g1, sh2, sc2, g2 = jnp.split(mod, 6, axis=-1)
    if t % LANES == 0:
        y = in_proj(x, sc1, sh1, lw['w_in'])
    else:
        h1 = (x * (1.0 + sc1[:, None, :]) + sh1[:, None, :]).reshape(1, b * t, d)
        y = in_proj(h1, jnp.zeros((1, d), F32), jnp.zeros((1, d), F32), lw['w_in']).reshape(b, t, PROJ_COLS)
    o_gla, s_fin = gla_pallas(y, lw, gla_s0)

    def rows(name):
        return proj_cols(y, name).reshape(b, t, NSA_KV_HEADS, HEAD_DIM).transpose(0, 2, 1, 3)
    kv_rows = [rows(nm) for nm in ('nkc', 'nvc', 'nks', 'nvs', 'nkw', 'nvw')]
    o_n, win_k, win_v = nsa_fn(y, *kv_rows)
    x1, h2, idx, w, conv_state = merge_pallas(x, y, o_gla, o_n, conv_prev, g1, sc2, sh2, lw)
    rows_out, rw = (moe_routed_sorted if sorted_moe else moe_routed_dense)(h2.reshape(b * t, d), idx, w, lw)
    x2 = moe_tail_pallas(x1, h2, rows_out, rw, g2, lw)
    return x2, (*kv_rows[:4], win_k, win_v, s_fin, conv_state)


def kernel(x_prompt, x_sample, cache_cmp_k, cache_cmp_v, cache_slc_k, cache_slc_v, cache_win_k, cache_win_v,
           state_gla, state_conv, page_table, c_prompt, c_sample, w_ada, b_ada, w_in, gla_wa2, gla_ba,
           gla_norm_g, conv_w, cmp_pe_k, cmp_pe_v, cmp_w1_k, cmp_w2_k, cmp_w1_v, cmp_w2_v, rel_bias,
           w_br_gla, w_br_conv, w_br_nsa, w_o, ln1_g, ln1_b, ln2_g, ln2_b, w_router, router_bias,
           w_exp_gate, w_exp_up, w_exp_down, w_sh_gate, w_sh_up, w_sh_down):
    params = dict(w_ada=w_ada, b_ada=b_ada, w_in=w_in, gla_wa2=gla_wa2, gla_ba=gla_ba, gla_norm_g=gla_norm_g,
                  conv_w=conv_w, cmp_pe_k=cmp_pe_k, cmp_pe_v=cmp_pe_v, cmp_w1_k=cmp_w1_k, cmp_w2_k=cmp_w2_k,
                  cmp_w1_v=cmp_w1_v, cmp_w2_v=cmp_w2_v, w_br_gla=w_br_gla, w_br_conv=w_br_conv, w_br_nsa=w_br_nsa,
                  w_o=w_o, ln1_g=ln1_g, ln1_b=ln1_b, ln2_g=ln2_g, ln2_b=ln2_b, w_router=w_router,
                  router_bias=router_bias, w_exp_gate=w_exp_gate, w_exp_up=w_exp_up, w_exp_down=w_exp_down,
                  w_sh_gate=w_sh_gate, w_sh_up=w_sh_up, w_sh_down=w_sh_down)
    tbl = rel_bias.reshape(REL_BUCKETS, NSA_KV_HEADS, NSA_GROUP)
    bp, tp, _ = x_prompt.shape
    bs, ts, _ = x_sample.shape
    bias_tables = nsa_bias_tables(tbl, tp, tp // CMP_STRIDE)
    sample_tables = nsa_sample_tables(tbl, page_table.shape[1] * PAGE_SIZE, ts, cache_win_k.shape[3])
    xp, xs = x_prompt, x_sample
    p_states, s_states = [], []
    for l in range(DEPTH):
        lw = {k: v[l] for k, v in params.items()}

        def prompt_nsa(y, kc_r, vc_r, ks_r, vs_r, kw_r, vw_r):
            o_n = nsa_prompt_pallas(y, PROJ_OFF['nq'], proj_cols(y, 'ngt'), kc_r, vc_r, ks_r, vs_r, kw_r, vw_r,
                                    lw, bias_tables)
            wb = min(WINDOW, tp)
            return o_n, kw_r[:, :, tp - wb:], vw_r[:, :, tp - wb:]

        def sample_nsa(y, kc_r, vc_r, ks_r, vs_r, kw_r, vw_r):
            o_n = nsa_sample_pallas(y, proj_cols(y, 'ngt'), (kc_r, vc_r, ks_r, vs_r, kw_r, vw_r), lw, sample_tables, l,
                                    (cache_cmp_k, cache_cmp_v, cache_slc_k, cache_slc_v), cache_win_k, cache_win_v,
                                    page_table)
            wb = cache_win_k.shape[3]
            win_k = jnp.concatenate([cache_win_k[l], kw_r], axis=2)[:, :, -wb:]
            win_v = jnp.concatenate([cache_win_v[l], vw_r], axis=2)[:, :, -wb:]
            return o_n, win_k, win_v

        xp, st = layer_block(xp, c_prompt, lw, jnp.zeros((bp, GLA_HEADS, GLA_DK, GLA_DV), F32),
                             jnp.zeros((bp, CONV_W - 1, CONV_DIM), F32), prompt_nsa, True)
        p_states.append(st)
        xs, st = layer_block(xs, c_sample, lw, state_gla[l], state_conv[l], sample_nsa, False)
        s_states.append(st)
    p_out = [jnp.stack(z) for z in zip(*p_states)]
    s_out = [jnp.stack(z) for z in zip(*s_states)]
    return (xp, xs, *p_out, *s_out)
```

```python
import math, functools
import jax, jax.numpy as jnp
from jax import lax
import numpy as np
from jax.experimental import pallas as pl
from jax.experimental.pallas import tpu as pltpu

D_MODEL = 1024
DEPTH = 2
PAGE_SIZE = 128
GLA_HEADS = 4
GLA_DK = 128
GLA_DV = 256
GLA_RANK = 16
GLA_TAU = 16.0
GLA_CHUNK = 64
CONV_DIM = D_MODEL
CONV_W = 3
NSA_HEADS = 16
NSA_KV_HEADS = 4
NSA_GROUP = NSA_HEADS // NSA_KV_HEADS
HEAD_DIM = 64
CMP_BLOCK = 32
CMP_STRIDE = 16
CMP_HIDDEN = 128
SLC_BLOCK = 64
SLC_TOPK = 16
WINDOW = 512
NSA_QCHUNK = 16
REL_BUCKETS = 32
REL_MAX_DIST = 128
N_EXPERTS = 64
TOP_K = 8
N_GROUPS = 8
TOPK_GROUPS = 4
EXPERT_HIDDEN = 256
SHARED_HIDDEN = 256
ROUTED_SCALE = 2.5
MOE_BLOCK = 128
DN_ALPHA = (2 * DEPTH) ** 0.25
LN_EPS = 1e-5
NEG_INF = -1e30

GLA_QK = GLA_HEADS * GLA_DK
GLA_V = GLA_HEADS * GLA_DV
NSA_Q = NSA_HEADS * HEAD_DIM
NSA_KV = NSA_KV_HEADS * HEAD_DIM
IN_SIZES = (GLA_QK, GLA_QK, GLA_V, GLA_V, GLA_RANK, CONV_DIM, CONV_DIM, CONV_DIM, NSA_Q, NSA_KV, NSA_KV, NSA_KV, NSA_KV, NSA_KV, NSA_KV, 3 * NSA_HEADS, 3 * D_MODEL)
IN_TOTAL = sum(IN_SIZES)


BF16 = jnp.bfloat16
F32 = jnp.float32
LANES = 128
NSA_TILE = 128
GLA_ROWS = 256
VMEM_LIMIT = 48 * 1024 * 1024

PROJ_NAMES = ('gq', 'gk', 'gv', 'gg', 'ga', 'cb', 'cc', 'cx', 'nq', 'nkc', 'nvc', 'nks', 'nvs', 'nkw', 'nvw', 'ngt', 'mgt')
PROJ_ORDER = ('gv', 'gg', 'cb', 'cc', 'cx', 'nq', 'mgt', 'gq', 'gk', 'nkc', 'nvc', 'nks', 'nvs', 'nkw', 'nvw', 'ga', 'ngt')
PROJ_SIZE = dict(zip(PROJ_NAMES, IN_SIZES))
PROJ_SRC = dict(zip(PROJ_NAMES, np.cumsum((0,) + IN_SIZES[:-1]).tolist()))
PROJ_OFF = dict(zip(PROJ_ORDER, np.cumsum([0] + [PROJ_SIZE[n] for n in PROJ_ORDER[:-1]]).tolist()))
PROJ_TN = 512
PROJ_COLS = -(-IN_TOTAL // PROJ_TN) * PROJ_TN


def _dot_nt(a, b):
    return lax.dot_general(a, b, (((1,), (1,)), ((), ())), preferred_element_type=F32)


def _dot(a, b):
    return jnp.dot(a, b, preferred_element_type=F32)


def _dot_hi(a, b):
    return jnp.dot(a, b, precision=lax.Precision.HIGHEST, preferred_element_type=F32)


def _in_proj_kernel(x_ref, sc_ref, sh_ref, w_ref, o_ref, h_ref):
    @pl.when(pl.program_id(2) == 0)
    def _():
        h_ref[...] = (x_ref[0] * (1.0 + sc_ref[0]) + sh_ref[0]).astype(BF16)
    o_ref[0] = _dot(h_ref[...], w_ref[...])


def in_proj(x, sc, sh, w_in):
    b, t, d = x.shape
    w = jnp.concatenate([w_in[:, PROJ_SRC[n]:PROJ_SRC[n] + PROJ_SIZE[n]] for n in PROJ_ORDER]
                        + [jnp.zeros((d, PROJ_COLS - IN_TOTAL), w_in.dtype)], axis=1).astype(BF16)
    tm = math.gcd(t, 1024)
    return pl.pallas_call(
        _in_proj_kernel,
        grid=(b, t // tm, PROJ_COLS // PROJ_TN),
        in_specs=[pl.BlockSpec((1, tm, d), lambda i, j, k: (i, j, 0)),
                  pl.BlockSpec((1, 1, d), lambda i, j, k: (i, 0, 0)),
                  pl.BlockSpec((1, 1, d), lambda i, j, k: (i, 0, 0)),
                  pl.BlockSpec((d, PROJ_TN), lambda i, j, k: (0, k))],
        out_specs=pl.BlockSpec((1, tm, PROJ_TN), lambda i, j, k: (i, j, k)),
        out_shape=jax.ShapeDtypeStruct((b, t, PROJ_COLS), F32),
        scratch_shapes=[pltpu.VMEM((tm, d), BF16)],
        compiler_params=pltpu.CompilerParams(vmem_limit_bytes=VMEM_LIMIT),
        name="in_proj",
    )(x, sc.reshape(b, 1, d), sh.reshape(b, 1, d), w)


def proj_cols(y, name):
    return y[..., PROJ_OFF[name]:PROJ_OFF[name] + PROJ_SIZE[name]]


def _gla_kernel(q_ref, k_ref, v_ref, gg_ref, sm_ref, wa2_ref, ba_ref, ng_ref, s0_ref, o_ref, sfin_ref, st_ref,
                *, chunk, nchunks):
    tb = pl.program_id(2)
    c = chunk

    @pl.when(tb == 0)
    def _():
        st_ref[...] = s0_ref[0, 0].T

    row = lax.broadcasted_iota(jnp.int32, (c, c), 0)
    col = lax.broadcasted_iota(jnp.int32, (c, c), 1)
    causal = row >= col
    tri = jnp.where(causal, 1.0, 0.0)
    ga_off = PROJ_OFF['ga'] % LANES
    for ci in range(nchunks):
        sl = slice(ci * c, (ci + 1) * c)
        q = q_ref[0, sl, :] * (GLA_DK ** -0.5)
        k = k_ref[0, sl, :]
        v = v_ref[0, sl, :].astype(BF16)
        ga = sm_ref[0, sl, ga_off:ga_off + GLA_RANK]
        x = _dot_hi(ga, wa2_ref[...]) + ba_ref[...]
        log_a = (jnp.minimum(x, 0.0) - jnp.log1p(jnp.exp(-jnp.abs(x)))) * (1.0 / GLA_TAU)
        cum = _dot_hi(tri, log_a)
        last = cum[c - 1:c, :]
        mid = cum[c // 2 - 1:c // 2, :]
        st = st_ref[...]
        inter = _dot_nt((q * jnp.exp(cum)).astype(BF16), st.astype(BF16))
        att = _dot_nt((q * jnp.exp(cum - mid)).astype(BF16), (k * jnp.exp(mid - cum)).astype(BF16))
        att = jnp.where(causal, att, 0.0)
        o = inter + _dot(att.astype(BF16), v)
        kd = (k * jnp.exp(last - cum)).astype(BF16)
        st_ref[...] = st * jnp.exp(last) + lax.dot_general(v, kd, (((0,), (0,)), ((), ())),
                                                           preferred_element_type=F32)
        mu = jnp.mean(o, axis=-1, keepdims=True)
        var = jnp.mean(jnp.square(o - mu), axis=-1, keepdims=True)
        gate = gg_ref[0, sl, :]
        o_ref[0, sl, :] = (o - mu) * lax.rsqrt(var + LN_EPS) * ng_ref[...] * (gate * jax.nn.sigmoid(gate))

    @pl.when(tb == pl.num_programs(2) - 1)
    def _():
        sfin_ref[0, 0] = st_ref[...].T


def gla_pallas(y, lw, s0):
    b, t, _ = y.shape
    c = math.gcd(t, GLA_CHUNK)
    r = math.gcd(t, GLA_ROWS)
    qb, kb = PROJ_OFF['gq'] // GLA_DK, PROJ_OFF['gk'] // GLA_DK
    vb, gb = PROJ_OFF['gv'] // GLA_DV, PROJ_OFF['gg'] // GLA_DV
    sb = PROJ_OFF['ga'] // LANES
    return pl.pallas_call(
        functools.partial(_gla_kernel, chunk=c, nchunks=r // c),
        grid=(b, GLA_HEADS, t // r),
        in_specs=[pl.BlockSpec((1, r, GLA_DK), lambda i, h, j: (i, j, qb + h)),
                  pl.BlockSpec((1, r, GLA_DK), lambda i, h, j: (i, j, kb + h)),
                  pl.BlockSpec((1, r, GLA_DV), lambda i, h, j: (i, j, vb + h)),
                  pl.BlockSpec((1, r, GLA_DV), lambda i, h, j: (i, j, gb + h)),
                  pl.BlockSpec((1, r, LANES), lambda i, h, j: (i, j, sb)),
                  pl.BlockSpec((GLA_RANK, GLA_DK), lambda i, h, j: (0, h)),
                  pl.BlockSpec((1, GLA_DK), lambda i, h, j: (0, h)),
                  pl.BlockSpec((1, GLA_DV), lambda i, h, j: (0, h)),
                  pl.BlockSpec((1, 1, GLA_DK, GLA_DV), lambda i, h, j: (i, h, 0, 0))],
        out_specs=[pl.BlockSpec((1, r, GLA_DV), lambda i, h, j: (i, j, h)),
                   pl.BlockSpec((1, 1, GLA_DK, GLA_DV), lambda i, h, j: (i, h, 0, 0))],
        out_shape=[jax.ShapeDtypeStruct((b, t, GLA_V), F32),
                   jax.ShapeDtypeStruct((b, GLA_HEADS, GLA_DK, GLA_DV), F32)],
        scratch_shapes=[pltpu.VMEM((GLA_DV, GLA_DK), F32)],
        name="gla_scan",
    )(y, y, y, y, y, lw['gla_wa2'], lw['gla_ba'].reshape(1, -1), lw['gla_norm_g'].reshape(1, -1), s0.astype(F32))


def _split3(x):
    hi = x.astype(BF16)
    r = x - hi.astype(F32)
    mid = r.astype(BF16)
    lo = (r - mid.astype(F32)).astype(BF16)
    return hi, mid, lo


def _gelu_tanh(x):
    return 0.5 * x * (1.0 + jnp.tanh(math.sqrt(2.0 / math.pi) * (x + 0.044715 * (x * x * x))))


def _compress_kernel(chk_ref, chv_ref, pek_ref, pev_ref, w1k_ref, w1v_ref, w2k_ref, w2v_ref, ok_ref, ov_ref):
    rows = CMP_STRIDE * HEAD_DIM
    for ch_ref, pe_ref, w1_ref, w2_ref, o_ref in ((chk_ref, pek_ref, w1k_ref, w2k_ref, ok_ref),
                                                   (chv_ref, pev_ref, w1v_ref, w2v_ref, ov_ref)):
        ch = ch_ref[0, 0].astype(BF16)
        nch = ch.shape[0]
        w1 = w1_ref[...]
        pe = jnp.broadcast_to(pe_ref[...], (8, pe_ref.shape[1])).astype(BF16)
        hid = _dot(pe, w1)[0:1, :]
        a0 = _dot(ch, w1[0:rows, :])
        a1 = _dot(ch, w1[rows:2 * rows, :])
        hid = hid + a0 + pltpu.roll(a1, nch - 1, axis=0)
        o_ref[0, 0] = _dot(_gelu_tanh(hid).astype(BF16), w2_ref[...])


def compress_pair(kc_r, vc_r, lw):
    b, kvh, length, hd = kc_r.shape
    nch = length // CMP_STRIDE
    chk = kc_r.reshape(b, kvh, nch, CMP_STRIDE * hd)
    chv = vc_r.reshape(b, kvh, nch, CMP_STRIDE * hd)
    ch_spec = pl.BlockSpec((1, 1, nch, CMP_STRIDE * hd), lambda i, j: (i, j, 0, 0))
    full = lambda a: pl.BlockSpec(a.shape, lambda i, j: (0,) * a.ndim)
    pek = lw['cmp_pe_k'].reshape(1, -1)
    pev = lw['cmp_pe_v'].reshape(1, -1)
    w1k, w1v = lw['cmp_w1_k'].astype(BF16), lw['cmp_w1_v'].astype(BF16)
    w2k, w2v = lw['cmp_w2_k'].astype(BF16), lw['cmp_w2_v'].astype(BF16)
    o_spec = pl.BlockSpec((1, 1, nch, hd), lambda i, j: (i, j, 0, 0))
    o_shape = jax.ShapeDtypeStruct((b, kvh, nch, hd), F32)
    return pl.pallas_call(
        _compress_kernel,
        grid=(b, kvh),
        in_specs=[ch_spec, ch_spec, full(pek), full(pev), full(w1k), full(w1v), full(w2k), full(w2v)],
        out_specs=[o_spec, o_spec],
        out_shape=[o_shape, o_shape],
        name="nsa_compress",
    )(chk, chv, pek, pev, w1k, w1v, w2k, w2v)


def _dot_tn(a, b):
    return lax.dot_general(a, b, (((0,), (0,)), ((), ())), preferred_element_type=F32)


def _nsa_t_kernel(q_ref, g_ref, kc_ref, vc_ref, ks_ref, vs_ref, kw_ref, vw_ref, bc_ref, bt_ref, ov_ref, o_ref,
                  acc_ref, *, nsb):
    tq = NSA_TILE
    grp = NSA_GROUP
    hd = HEAD_DIM
    wq = grp * tq
    ncp = kc_ref.shape[2]
    qi = pl.program_id(2)
    q0 = qi * tq
    q = q_ref[0] * (hd ** -0.5)
    q4 = jnp.concatenate([q[:, g * hd:(g + 1) * hd] for g in range(grp)], axis=0).astype(BF16)
    lane_q = lax.broadcasted_iota(jnp.int32, (1, tq), 1)
    qpos1 = q0 + lane_q
    qpos = jnp.concatenate([qpos1] * grp, axis=1)

    kc = kc_ref[0, 0].astype(BF16)
    vc = vc_ref[0, 0].astype(BF16)
    cend = CMP_STRIDE * lax.broadcasted_iota(jnp.int32, (ncp, 1), 0) + (CMP_BLOCK - 1)
    mask_c = cend <= qpos
    lc = jnp.where(mask_c, _dot_nt(kc, q4) + bc_ref[0, 0], NEG_INF)
    mc = jnp.max(lc, axis=0, keepdims=True)
    pc = jnp.where(mask_c, jnp.exp(lc - mc), 0.0)
    lsum = jnp.sum(pc, axis=0, keepdims=True)
    pc = pc / jnp.where(lsum > 0.0, lsum, 1.0)
    o_c = _dot_tn(vc, pc.astype(BF16))
    psum = pc[:, 0:tq]
    for g in range(1, grp):
        psum = psum + pc[:, g * tq:(g + 1) * tq]

    ov = ov_ref[...]
    hi, mid, lo = _split3(psum)
    imp = (_dot(ov, hi) + _dot(ov, mid) + _dot(ov, lo))[0:nsb, :]
    blk = lax.broadcasted_iota(jnp.int32, (nsb, 1), 0)
    cur = lax.shift_right_logical(qpos1, 6)
    forced = (blk == 0) | (blk == cur) | (blk == cur - 1)
    visible = blk * SLC_BLOCK <= qpos1
    score = jnp.where(forced, -NEG_INF, imp)
    score = jnp.where(visible, score, NEG_INF)
    rank = jnp.zeros((nsb, tq), jnp.int32)
    for i in range(nsb):
        si = score[i:i + 1, :]
        beats = (si > score) | ((si == score) & (blk > i))
        rank = rank + beats.astype(jnp.int32)
    sel1 = jnp.where(visible & (rank < SLC_TOPK), 1.0, 0.0).astype(BF16)
    sel1 = jnp.concatenate([sel1, jnp.zeros((LANES - nsb, tq), BF16)], axis=0)
    sel = jnp.concatenate([sel1] * grp, axis=1)

    neg = jnp.full((1, wq), NEG_INF, F32)
    zero = jnp.zeros((1, wq), F32)
    krow = lax.broadcasted_iota(jnp.int32, (tq, 1), 0)
    lane4 = jnp.concatenate([lane_q] * grp, axis=1)

    def slc_logits(kt, causal, valid):
        k0 = pl.multiple_of(kt * tq, tq)
        k_t = ks_ref[0, 0, pl.ds(k0, tq), :].astype(BF16)
        v_t = vs_ref[0, 0, pl.ds(k0, tq), :].astype(BF16)
        kblk = lax.shift_right_logical(k0 + krow, 6)
        expand = jnp.where(kblk == lax.broadcasted_iota(jnp.int32, (1, LANES), 1), 1.0, 0.0).astype(BF16)
        mask = _dot(expand, sel) > 0.5
        if causal:
            mask = mask & (krow <= lane4)
        if valid is not None:
            mask = mask & valid
        return jnp.where(mask, _dot_nt(k_t, q4) + bt_ref[0, jnp.minimum(qi - kt, 2)], NEG_INF), v_t

    def slc_pair(a, b, m, l):
        (sa, va), (sb, vb) = a, b
        m_new = jnp.maximum(m, jnp.maximum(jnp.max(sa, axis=0, keepdims=True), jnp.max(sb, axis=0, keepdims=True)))
        alpha = jnp.exp(m - m_new)
        pa, pb = jnp.exp(sa - m_new), jnp.exp(sb - m_new)
        l = alpha * l + jnp.sum(pa, axis=0, keepdims=True) + jnp.sum(pb, axis=0, keepdims=True)
        acc_ref[...] = alpha * acc_ref[...] + _dot_tn(va, pa.astype(BF16)) + _dot_tn(vb, pb.astype(BF16))
        return m_new, l

    acc_ref[...] = jnp.zeros_like(acc_ref)
    m_s, l_s = lax.fori_loop(
        0, lax.shift_right_logical(qi, 1),
        lambda j, c: slc_pair(slc_logits(2 * j, False, None), slc_logits(2 * j + 1, False, None), *c), (neg, zero))
    m_s, l_s = slc_pair(slc_logits(jnp.maximum(qi - 1, 0), False, (qi & 1) == 1), slc_logits(qi, True, None), m_s, l_s)
    o_s = acc_ref[...] / l_s

    nwt = WINDOW // tq
    s_w, v_w = [], []
    for dt in range(nwt, -1, -1):
        kt = qi - dt
        k0 = pl.multiple_of(jnp.maximum(kt, 0) * tq, tq)
        k_t = kw_ref[0, 0, pl.ds(k0, tq), :].astype(BF16)
        v_w.append(vw_ref[0, 0, pl.ds(k0, tq), :].astype(BF16))
        s = _dot_nt(k_t, q4) + bt_ref[0, min(dt, 2)]
        if dt == nwt:
            s = jnp.where((krow > lane4) & (kt >= 0), s, NEG_INF)
        elif dt == 0:
            s = jnp.where(krow <= lane4, s, NEG_INF)
        else:
            s = jnp.where(kt >= 0, s, NEG_INF)
        s_w.append(s)
    m_w = functools.reduce(jnp.maximum, [jnp.max(s, axis=0, keepdims=True) for s in s_w])
    p_w = [jnp.exp(s - m_w) for s in s_w]
    l_w = functools.reduce(jnp.add, [jnp.sum(p, axis=0, keepdims=True) for p in p_w])
    o_w = functools.reduce(jnp.add, [_dot_tn(v, p.astype(BF16)) for v, p in zip(v_w, p_w)]) / l_w

    gates = jax.nn.sigmoid(g_ref[0, 0])
    outs = []
    for g in range(grp):
        sl = slice(g * tq, (g + 1) * tq)
        outs.append(gates[:, 3 * g:3 * g + 1] * o_c[:, sl].T + gates[:, 3 * g + 1:3 * g + 2] * o_s[:, sl].T
                    + gates[:, 3 * g + 2:3 * g + 3] * o_w[:, sl].T)
    o_ref[0] = jnp.concatenate(outs, axis=-1)


def nsa_bias_tables(tbl, t, ncp):
    tq = NSA_TILE
    kvh, grp = tbl.shape[1], tbl.shape[2]
    dist_c = jnp.arange(t)[None, :] - (CMP_STRIDE * jnp.arange(ncp) + CMP_BLOCK - 1)[:, None]
    bc = tbl[rel_bucket(dist_c)]
    bc = bc.reshape(ncp, t // tq, tq, kvh, grp).transpose(3, 1, 0, 4, 2).reshape(kvh, t // tq, ncp, grp * tq)
    cr = jnp.arange(tq)[None, :] - jnp.arange(tq)[:, None]
    tiles = jnp.stack([cr, cr + tq, jnp.full_like(cr, REL_MAX_DIST)])
    bt = tbl[rel_bucket(tiles)]
    bt = bt.transpose(3, 0, 1, 4, 2).reshape(kvh, 3, tq, grp * tq)
    return bc, bt


def nsa_prompt_pallas(y, q_off, ngt, kc_r, vc_r, ks_r, vs_r, kw_r, vw_r, lw, bias_tables):
    b, t, _ = y.shape
    kvh, grp, hd, tq = NSA_KV_HEADS, NSA_GROUP, HEAD_DIM, NSA_TILE
    qb = q_off // (grp * hd)
    kc, vc = compress_pair(kc_r, vc_r, lw)
    ncp = kc.shape[2]
    nsb = t // SLC_BLOCK
    bc, bt = bias_tables
    gates = ngt.reshape(b, t, kvh, grp * 3).transpose(0, 2, 1, 3)
    ci = jnp.arange(ncp)[None, :] * CMP_STRIDE
    sj = jnp.arange(LANES)[:, None] * SLC_BLOCK
    ov = ((ci < sj + SLC_BLOCK) & (ci + CMP_BLOCK > sj) & (jnp.arange(ncp)[None, :] < ncp - 1)
          & (jnp.arange(LANES)[:, None] < nsb)).astype(BF16)
    row_spec = pl.BlockSpec((1, 1, t, hd), lambda i, j, k: (i, j, 0, 0))
    cmp_spec = pl.BlockSpec((1, 1, ncp, hd), lambda i, j, k: (i, j, 0, 0))
    return pl.pallas_call(
        functools.partial(_nsa_t_kernel, nsb=nsb),
        grid=(b, kvh, t // tq),
        in_specs=[pl.BlockSpec((1, tq, grp * hd), lambda i, j, k: (i, k, qb + j)),
                  pl.BlockSpec((1, 1, tq, grp * 3), lambda i, j, k: (i, j, k, 0)),
                  cmp_spec, cmp_spec, row_spec, row_spec, row_spec, row_spec,
                  pl.BlockSpec((1, 1, ncp, grp * tq), lambda i, j, k: (j, k, 0, 0)),
                  pl.BlockSpec((1, 3, tq, grp * tq), lambda i, j, k: (j, 0, 0, 0)),
                  pl.BlockSpec(ov.shape, lambda i, j, k: (0, 0))],
        out_specs=pl.BlockSpec((1, tq, grp * hd), lambda i, j, k: (i, k, j)),
        out_shape=jax.ShapeDtypeStruct((b, t, kvh * grp * hd), F32),
        scratch_shapes=[pltpu.VMEM((hd, grp * tq), F32)],
        compiler_params=pltpu.CompilerParams(vmem_limit_bytes=VMEM_LIMIT),
        name="nsa_prompt",
    )(y, gates, kc, vc, ks_r, vs_r, kw_r, vw_r, bc, bt, ov)


SMP_TILE = 512


def _nsa_sample_kernel(pt_ref, q_ref, g_ref, nkc_ref, nvc_ref, nks_ref, nvs_ref, nkw_ref, nvw_ref, wk_ref, wv_ref,
                       pek_ref, pev_ref, w1k_ref, w1v_ref, w2k_ref, w2v_ref, bcs_ref, bsl_ref, bwn_ref, bnew_ref,
                       bfar_ref, ov_ref, gm_ref, ck_hbm, cv_hbm, sk_hbm, sv_hbm, o_ref,
                       buf_a, buf_b, stage_a, stage_b, sel_ref, sem, *, layer, past, nt):
    b = pl.program_id(0)
    h = pl.program_id(1)
    grp, hd = NSA_GROUP, HEAD_DIM
    npages = past // PAGE_SIZE
    nch = past // CMP_STRIDE
    nblk = ov_ref.shape[0]

    def page_copy(pool, stage, p, slot):
        return pltpu.make_async_copy(pool.at[layer, pt_ref[b, p], h],
                                     stage.at[:, pl.ds(pl.multiple_of(p * PAGE_SIZE, PAGE_SIZE), PAGE_SIZE)],
                                     sem.at[slot])

    def start_pages(pool_a, pool_b):
        def body(p, c):
            page_copy(pool_a, stage_a, p, 0).start()
            page_copy(pool_b, stage_b, p, 1).start()
            return c
        lax.fori_loop(0, npages, body, 0)

    def wait_pages(pool_a, pool_b):
        def body(p, c):
            page_copy(pool_a, stage_a, p, 0).wait()
            page_copy(pool_b, stage_b, p, 1).wait()
            return c
        lax.fori_loop(0, npages, body, 0)

    def land_pages():
        pad = jnp.zeros((PAGE_SIZE - hd, PAGE_SIZE), F32)

        def body(p, c):
            c0 = pl.multiple_of(p * PAGE_SIZE, PAGE_SIZE)
            for stage, buf in ((stage_a, buf_a), (stage_b, buf_b)):
                sq = jnp.concatenate([stage[:, pl.ds(c0, PAGE_SIZE)], pad], axis=0).T
                buf[pl.ds(c0, PAGE_SIZE), :] = sq[:, 0:hd]
            return c
        lax.fori_loop(0, npages, body, 0, unroll=8)

    start_pages(ck_hbm, cv_hbm)

    q = q_ref[0] * (hd ** -0.5)
    q4 = jnp.concatenate([q[:, g * hd:(g + 1) * hd] for g in range(grp)]
                         + [jnp.zeros((LANES - grp * nt, hd), F32)], axis=0).astype(BF16)
    lane = lax.broadcasted_iota(jnp.int32, (1, LANES), 1)
    tok = lane & (nt - 1)
    qpos = past + tok

    wait_pages(ck_hbm, cv_hbm)
    land_pages()
    start_pages(sk_hbm, sv_hbm)

    rows = CMP_STRIDE * hd

    def summaries(buf, new_ref, pe_ref, w1_ref, w2_ref):
        w1 = w1_ref[...]
        a0 = jnp.zeros((nch, CMP_HIDDEN), F32)
        a1 = jnp.zeros((nch, CMP_HIDDEN), F32)
        for j in range(CMP_STRIDE):
            xj = buf[pl.ds(j, nch, stride=CMP_STRIDE), :].astype(BF16)
            a0 = a0 + _dot(xj, w1[j * hd:(j + 1) * hd, :])
            a1 = a1 + _dot(xj, w1[rows + j * hd:rows + (j + 1) * hd, :])
        new = new_ref[0, 0].astype(BF16)
        a1_new = jnp.zeros((1, CMP_HIDDEN), F32)
        for j in range(nt):
            a1_new = a1_new + _dot(new, w1[rows + j * hd:rows + (j + 1) * hd, :])[j:j + 1, :]
        pe = jnp.broadcast_to(pe_ref[...], (8, pe_ref.shape[1])).astype(BF16)
        rid = lax.broadcasted_iota(jnp.int32, (nch, 1), 0)
        hid = _dot(pe, w1)[0:1, :] + a0 + jnp.where(rid == nch - 1, a1_new, pltpu.roll(a1, nch - 1, axis=0))
        return _dot(_gelu_tanh(hid).astype(BF16), w2_ref[...]).astype(BF16)

    kc = summaries(buf_a, nkc_ref, pek_ref, w1k_ref, w2k_ref)
    vc = summaries(buf_b, nvc_ref, pev_ref, w1v_ref, w2v_ref)

    cend = CMP_STRIDE * lax.broadcasted_iota(jnp.int32, (nch, 1), 0) + (CMP_BLOCK - 1)
    mask_c = cend <= qpos
    lc = jnp.where(mask_c, _dot_nt(kc, q4) + bcs_ref[0], NEG_INF)
    mc = jnp.max(lc, axis=0, keepdims=True)
    pc = jnp.where(mask_c, jnp.exp(lc - mc), 0.0)
    lsum = jnp.sum(pc, axis=0, keepdims=True)
    pc = pc / jnp.where(lsum > 0.0, lsum, 1.0)
    o_c = _dot_tn(vc, pc.astype(BF16))

    ov = ov_ref[...]
    hi, mid, lo = _split3(pc)
    imp = _dot(ov, hi) + _dot(ov, mid) + _dot(ov, lo)
    gm = gm_ref[...]
    hi, mid, lo = _split3(imp)
    imp = _dot(hi, gm) + _dot(mid, gm) + _dot(lo, gm)
    blk = lax.broadcasted_iota(jnp.int32, (nblk, 1), 0)
    cur = lax.shift_right_logical(qpos, 6)
    forced = (blk == 0) | (blk == cur) | (blk == cur - 1)
    visible = blk * SLC_BLOCK <= qpos
    score = jnp.where(forced, -NEG_INF, imp)
    score = jnp.where(visible, score, NEG_INF)
    sel = jnp.zeros((nblk, LANES), F32)
    for _ in range(SLC_TOPK):
        mx = jnp.max(score, axis=0, keepdims=True)
        first = jnp.min(jnp.where(score == mx, blk, nblk), axis=0, keepdims=True)
        pick = blk == first
        sel = jnp.where(pick & (mx > 0.5 * NEG_INF), 1.0, sel)
        score = jnp.where(pick, -3.0e38, score)
    sel_ref[...] = sel

    wait_pages(sk_hbm, sv_hbm)
    land_pages()

    bpt = SMP_TILE // SLC_BLOCK
    ntile = past // SMP_TILE

    def past_tile(kt):
        k0 = pl.multiple_of(kt * SMP_TILE, SMP_TILE)
        k_t = buf_a[pl.ds(k0, SMP_TILE), :].astype(BF16)
        v_t = buf_b[pl.ds(k0, SMP_TILE), :].astype(BF16)
        sblk = sel_ref[pl.ds(pl.multiple_of(kt * bpt, bpt), bpt), :]
        mask = jnp.concatenate([jnp.broadcast_to(sblk[i:i + 1, :], (SLC_BLOCK, LANES)) for i in range(bpt)],
                               axis=0) > 0.5
        bias = jnp.where(kt == ntile - 1, bsl_ref[0], bfar_ref[0])
        return jnp.where(mask, _dot_nt(k_t, q4) + bias, NEG_INF), v_t

    def pair(a, bb, m, l, acc):
        (sa, va), (sb, vb) = a, bb
        m_new = jnp.maximum(m, jnp.maximum(jnp.max(sa, axis=0, keepdims=True), jnp.max(sb, axis=0, keepdims=True)))
        alpha = jnp.exp(m - m_new)
        pa, pb = jnp.exp(sa - m_new), jnp.exp(sb - m_new)
        l = alpha * l + jnp.sum(pa, axis=0, keepdims=True) + jnp.sum(pb, axis=0, keepdims=True)
        acc = alpha * acc + _dot_tn(va, pa.astype(BF16)) + _dot_tn(vb, pb.astype(BF16))
        return m_new, l, acc

    neg = jnp.full((1, LANES), NEG_INF, F32)
    zero = jnp.zeros((1, LANES), F32)
    m_s, l_s, acc_s = lax.fori_loop(0, ntile // 2, lambda j, c: pair(past_tile(2 * j), past_tile(2 * j + 1), *c),
                                    (neg, zero, jnp.zeros((hd, LANES), F32)))
    jrow = lax.broadcasted_iota(jnp.int32, (nt, 1), 0)
    new_vis = jrow <= tok
    new_sel = sel_ref[pl.ds(past // SLC_BLOCK, 1), :] > 0.5
    s_new = jnp.where(new_vis & new_sel, _dot_nt(nks_ref[0, 0].astype(BF16), q4) + bnew_ref[0], NEG_INF)
    m_new = jnp.maximum(m_s, jnp.max(s_new, axis=0, keepdims=True))
    alpha = jnp.exp(m_s - m_new)
    p_new = jnp.exp(s_new - m_new)
    l_s = alpha * l_s + jnp.sum(p_new, axis=0, keepdims=True)
    o_s = (alpha * acc_s + _dot_tn(nvs_ref[0, 0].astype(BF16), p_new.astype(BF16))) / l_s

    wb = wk_ref.shape[3]
    wrow = lax.broadcasted_iota(jnp.int32, (wb, 1), 0)
    s_wc = jnp.where(wrow > tok + (wb - WINDOW), _dot_nt(wk_ref[0, 0, 0].astype(BF16), q4) + bwn_ref[0], NEG_INF)
    s_wn = jnp.where(new_vis, _dot_nt(nkw_ref[0, 0].astype(BF16), q4) + bnew_ref[0], NEG_INF)
    m_w = jnp.maximum(jnp.max(s_wc, axis=0, keepdims=True), jnp.max(s_wn, axis=0, keepdims=True))
    p_wc, p_wn = jnp.exp(s_wc - m_w), jnp.exp(s_wn - m_w)
    l_w = jnp.sum(p_wc, axis=0, keepdims=True) + jnp.sum(p_wn, axis=0, keepdims=True)
    o_w = (_dot_tn(wv_ref[0, 0, 0].astype(BF16), p_wc.astype(BF16))
           + _dot_tn(nvw_ref[0, 0].astype(BF16), p_wn.astype(BF16))) / l_w

    gates = jax.nn.sigmoid(g_ref[0, 0])
    oc_t, os_t, ow_t = o_c.T, o_s.T, o_w.T
    outs = []
    for g in range(grp):
        sl = slice(g * nt, (g + 1) * nt)
        outs.append(gates[:, 3 * g:3 * g + 1] * oc_t[sl] + gates[:, 3 * g + 1:3 * g + 2] * os_t[sl]
                    + gates[:, 3 * g + 2:3 * g + 3] * ow_t[sl])
    o_ref[0] = jnp.concatenate(outs, axis=-1)


def nsa_sample_tables(tbl, past, nt, wb):
    kvh, grp = tbl.shape[1], tbl.shape[2]
    t = jnp.arange(nt)

    def table(key_pos):
        v = tbl[rel_bucket(past + t[None, :] - key_pos[:, None])]
        v = v.transpose(2, 0, 3, 1).reshape(kvh, key_pos.shape[0], grp * nt)
        return jnp.pad(v, ((0, 0), (0, 0), (0, LANES - grp * nt)))
    nch = past // CMP_STRIDE
    bcs = table(CMP_STRIDE * jnp.arange(nch) + CMP_BLOCK - 1)
    bsl = table(past - SMP_TILE + jnp.arange(SMP_TILE))
    bwn = table(past - wb + jnp.arange(wb))
    bnew = table(past + jnp.arange(nt))
    bfar = table(jnp.full((1,), past - SMP_TILE - 1))
    return bcs, bsl, bwn, bnew, bfar


def nsa_sample_pallas(y, ngt, new_rows, lw, tables, layer, pools, win_k, win_v, page_table):
    b, nt, _ = y.shape
    kvh, grp, hd = NSA_KV_HEADS, NSA_GROUP, HEAD_DIM
    past = page_table.shape[1] * PAGE_SIZE
    assert nt & (nt - 1) == 0 and grp * nt <= LANES and nt <= CMP_STRIDE and past % (2 * SMP_TILE) == 0
    wb = win_k.shape[3]
    nch = past // CMP_STRIDE
    nsb = past // SLC_BLOCK + 1
    nblk = -(-nsb // 8) * 8
    qb = PROJ_OFF['nq'] // (grp * hd)
    gates = ngt.reshape(b, nt, kvh, grp * 3).transpose(0, 2, 1, 3)
    ci = jnp.arange(nch)[None, :] * CMP_STRIDE
    sj = jnp.arange(nblk)[:, None] * SLC_BLOCK
    ov = ((ci < sj + SLC_BLOCK) & (ci + CMP_BLOCK > sj) & (jnp.arange(nblk)[:, None] < nsb)).astype(BF16)
    ln = jnp.arange(LANES)
    gm = (((ln[:, None] & (nt - 1)) == (ln[None, :] & (nt - 1))) & (ln[:, None] < grp * nt)).astype(BF16)
    bcs, bsl, bwn, bnew, bfar = tables
    new_spec = pl.BlockSpec((1, 1, nt, hd), lambda i, j, pt: (i, j, 0, 0))
    win_spec = pl.BlockSpec((1, 1, 1, wb, hd), lambda i, j, pt: (layer, i, j, 0, 0))
    full = lambda a: pl.BlockSpec(a.shape, lambda i, j, pt: (0,) * a.ndim)
    per_head = lambda a: pl.BlockSpec((1,) + a.shape[1:], lambda i, j, pt: (j,) + (0,) * (a.ndim - 1))
    hbm = pl.BlockSpec(memory_space=pl.ANY)
    pek, pev = lw['cmp_pe_k'].reshape(1, -1), lw['cmp_pe_v'].reshape(1, -1)
    w1k, w1v = lw['cmp_w1_k'].astype(BF16), lw['cmp_w1_v'].astype(BF16)
    w2k, w2v = lw['cmp_w2_k'].astype(BF16), lw['cmp_w2_v'].astype(BF16)
    return pl.pallas_call(
        functools.partial(_nsa_sample_kernel, layer=layer, past=past, nt=nt),
        grid_spec=pltpu.PrefetchScalarGridSpec(
            num_scalar_prefetch=1,
            grid=(b, kvh),
            in_specs=[pl.BlockSpec((1, nt, grp * hd), lambda i, j, pt: (i, 0, qb + j)),
                      pl.BlockSpec((1, 1, nt, grp * 3), lambda i, j, pt: (i, j, 0, 0)),
                      new_spec, new_spec, new_spec, new_spec, new_spec, new_spec, win_spec, win_spec,
                      full(pek), full(pev), full(w1k), full(w1v), full(w2k), full(w2v),
                      per_head(bcs), per_head(bsl), per_head(bwn), per_head(bnew), per_head(bfar),
                      full(ov), full(gm), hbm, hbm, hbm, hbm],
            out_specs=pl.BlockSpec((1, nt, grp * hd), lambda i, j, pt: (i, 0, j)),
            scratch_shapes=[pltpu.VMEM((past, hd), F32), pltpu.VMEM((past, hd), F32),
                            pltpu.VMEM((hd, past), F32), pltpu.VMEM((hd, past), F32),
                            pltpu.VMEM((nblk, LANES), F32), pltpu.SemaphoreType.DMA((2,))]),
        out_shape=jax.ShapeDtypeStruct((b, nt, kvh * grp * hd), F32),
        compiler_params=pltpu.CompilerParams(vmem_limit_bytes=VMEM_LIMIT),
        name="nsa_sample",
    )(page_table, y, gates, *new_rows, win_k, win_v, pek, pev, w1k, w1v, w2k, w2v, bcs, bsl, bwn, bnew, bfar,
      ov, gm, *[jnp.swapaxes(p, 3, 4) for p in pools])


def layer_norm(x, g, b):
    xf = x.astype(jnp.float32)
    mu = xf.mean(-1, keepdims=True)
    var = jnp.square(xf - mu).mean(-1, keepdims=True)
    return ((xf - mu) * lax.rsqrt(var + LN_EPS) * g + b).astype(x.dtype)


def masked_softmax(logits, mask):
    p = jax.nn.softmax(jnp.where(mask, logits.astype(jnp.float32), NEG_INF), axis=-1)
    return jnp.where(mask, p, 0.0)


def rel_bucket(dist):
    n = jnp.maximum(dist, 0)
    exact = REL_BUCKETS // 2
    big = exact + (jnp.log(jnp.maximum(n, 1).astype(jnp.float32) / exact)
                   / math.log(REL_MAX_DIST / exact) * (REL_BUCKETS - exact)).astype(jnp.int32)
    return jnp.where(n < exact, n, jnp.minimum(big, REL_BUCKETS - 1))


def gla_scan(q, k, v, log_a, s0):
    b, t = q.shape[:2]
    c = math.gcd(t, GLA_CHUNK)
    nc = t // c
    causal = jnp.tril(jnp.ones((c, c), dtype=bool))

    def to_chunks(a):
        return a.reshape(b, nc, c, *a.shape[2:]).swapaxes(0, 1)

    def step(s, inp):
        qc, kc, vc, ac = inp
        cum = jnp.cumsum(ac, axis=1)
        inter = jnp.einsum('bthk,bhkv->bthv', qc * jnp.exp(cum), s)
        diff = cum[:, :, None] - cum[:, None, :]
        decay = jnp.exp(jnp.where(causal[None, :, :, None, None], diff, -jnp.inf))
        att = jnp.einsum('bthk,bshk,btshk->bhts', qc, kc, decay)
        intra = jnp.einsum('bhts,bshv->bthv', att, vc)
        last = cum[:, -1]
        s = jnp.exp(last)[..., None] * s + jnp.einsum('bshk,bshv->bhkv', kc * jnp.exp(last[:, None] - cum), vc)
        return s, inter + intra

    s_fin, o = lax.scan(step, s0, (to_chunks(q), to_chunks(k), to_chunks(v), to_chunks(log_a)))
    return o.swapaxes(0, 1).reshape(b, t, *v.shape[2:]), s_fin


def gla_output(o, gate, norm_g):
    mu = o.mean(-1, keepdims=True)
    var = jnp.square(o - mu).mean(-1, keepdims=True)
    on = ((o - mu) * lax.rsqrt(var + LN_EPS)).reshape(*o.shape[:2], -1)
    return on * norm_g * jax.nn.silu(gate.astype(jnp.float32))


def short_conv(b_gate, c_gate, x_in, prev, w):
    u = c_gate * x_in
    up = jnp.concatenate([prev, u], axis=1)
    t = u.shape[1]
    y = up[:, 0:t] * w[0]
    for j in range(1, CONV_W):
        y = y + up[:, j:j + t] * w[j]
    return b_gate * y, up[:, t:]


def compress(k, pe, w1, w2):
    b, kvh, length, hd = k.shape
    n_chunks = length // CMP_STRIDE
    pieces = CMP_BLOCK // CMP_STRIDE
    n_blocks = n_chunks - pieces + 1
    ch = k.reshape(b, kvh, n_chunks, CMP_STRIDE * hd)
    rows = CMP_STRIDE * hd
    hid = pe.reshape(-1) @ w1
    for r in range(pieces):
        hid = hid + (ch @ w1[r * rows:(r + 1) * rows])[:, :, r:r + n_blocks]
    return jax.nn.gelu(hid) @ w2


def slc_overlap(n_cmp, n_slc):
    start = jnp.arange(n_cmp) * CMP_STRIDE
    blk = jnp.arange(n_slc) * SLC_BLOCK
    return ((start[:, None] < blk[None, :] + SLC_BLOCK) & (start[:, None] + CMP_BLOCK > blk[None, :])).astype(jnp.float32)


def nsa_attend(q, qpos, gates, kc, vc, overlap, gather_slc, kw, vw, wpos, tbl):
    b, nq, kvh, grp, hd = q.shape
    scale = hd ** -0.5
    f32 = jnp.float32
    hi = jnp.arange(kvh)[:, None, None]
    cmp_end = CMP_STRIDE * jnp.arange(kc.shape[2]) + CMP_BLOCK - 1
    dist_c = qpos[:, None] - cmp_end[None, :]
    def bias(dist, spec):
        onehot = jax.nn.one_hot(rel_bucket(dist), REL_BUCKETS, dtype=f32)
        return jnp.einsum(spec, onehot, tbl, precision=lax.Precision.HIGHEST)

    lc = jnp.einsum('bqhgd,bhnd->bqhgn', q, kc).astype(f32) * scale + bias(dist_c, 'qnk,khg->qhgn')
    pc = masked_softmax(lc, (dist_c >= 0)[:, None, None, :])
    o_c = jnp.einsum('bqhgn,bhnd->bqhgd', pc, vc)
    imp = jnp.einsum('bqhn,ns->bqhs', pc.sum(3), overlap)
    blk = jnp.arange(overlap.shape[1])
    cur = (qpos // SLC_BLOCK)[:, None]
    forced = (blk[None] == 0) | (blk[None] == cur) | (blk[None] == cur - 1)
    visible = blk[None] * SLC_BLOCK <= qpos[:, None]
    score = jnp.where(forced[:, None], -NEG_INF, imp)
    score = jnp.where(visible[:, None], score, NEG_INF)
    top_s, idx = lax.top_k(score, min(SLC_TOPK, overlap.shape[1]))
    ks, vs = gather_slc(idx)
    kpos = idx[..., None] * SLC_BLOCK + jnp.arange(SLC_BLOCK)
    dist_s = qpos[None, :, None, None, None] - kpos
    mask_s = (dist_s >= 0) & (top_s > 0.5 * NEG_INF)[..., None]
    ls = jnp.einsum('bqhgd,bqhnsd->bqhgns', q, ks).astype(f32) * scale + bias(dist_s, 'bqhnsk,khg->bqhgns')
    ps = masked_softmax(ls.reshape(b, nq, kvh, grp, -1), mask_s.reshape(b, nq, kvh, 1, -1))
    o_s = jnp.einsum('bqhgm,bqhmd->bqhgd', ps, vs.reshape(b, nq, kvh, -1, hd))
    dist_w = qpos[:, None] - wpos[None, :]
    mask_w = (dist_w >= 0) & (dist_w < WINDOW) & (wpos >= 0)[None, :]
    lwin = jnp.einsum('bqhgd,bhwd->bqhgw', q, kw).astype(f32) * scale + bias(dist_w, 'qwk,khg->qhgw')
    pw = masked_softmax(lwin, mask_w[:, None, None, :])
    o_w = jnp.einsum('bqhgw,bhwd->bqhgd', pw, vw)
    o = gates[..., 0:1] * o_c + gates[..., 1:2] * o_s + gates[..., 2:3] * o_w
    return o.astype(q.dtype)


def nsa_prompt(q, gates, kc_r, vc_r, ks_r, vs_r, kw_r, vw_r, lw, tbl):
    b, s = q.shape[:2]
    kc = compress(kc_r, lw['cmp_pe_k'], lw['cmp_w1_k'], lw['cmp_w2_k'])
    vc = compress(vc_r, lw['cmp_pe_v'], lw['cmp_w1_v'], lw['cmp_w2_v'])
    nsb = s // SLC_BLOCK
    ks_blk = ks_r.reshape(b, NSA_KV_HEADS, nsb, SLC_BLOCK, HEAD_DIM)
    vs_blk = vs_r.reshape(b, NSA_KV_HEADS, nsb, SLC_BLOCK, HEAD_DIM)
    bi = jnp.arange(b)[:, None, None, None]
    hi = jnp.arange(NSA_KV_HEADS)[None, None, :, None]

    def gather_slc(idx):
        return ks_blk[bi, hi, idx], vs_blk[bi, hi, idx]

    overlap = slc_overlap(kc.shape[2], nsb)
    kw_pad = jnp.pad(kw_r, ((0, 0), (0, 0), (WINDOW, 0), (0, 0)))
    vw_pad = jnp.pad(vw_r, ((0, 0), (0, 0), (WINDOW, 0), (0, 0)))
    n_chunks = s // NSA_QCHUNK
    span = WINDOW + NSA_QCHUNK

    def chunks(a):
        return a.reshape(b, n_chunks, NSA_QCHUNK, *a.shape[2:]).swapaxes(0, 1)

    def body(inp):
        qc, gc, c0 = inp
        return nsa_attend(qc, c0 + jnp.arange(NSA_QCHUNK), gc, kc, vc, overlap, gather_slc,
                          lax.dynamic_slice_in_dim(kw_pad, c0, span, axis=2),
                          lax.dynamic_slice_in_dim(vw_pad, c0, span, axis=2),
                          c0 - WINDOW + jnp.arange(span), tbl)

    o = lax.map(body, (chunks(q), chunks(gates), jnp.arange(n_chunks) * NSA_QCHUNK))
    o = o.swapaxes(0, 1).reshape(b, s, *q.shape[2:])
    wb = min(WINDOW, s)
    return o, kw_r[:, :, s - wb:], vw_r[:, :, s - wb:]


def nsa_sample(q, gates, kc_r, vc_r, ks_r, vs_r, kw_r, vw_r, lw, tbl, layer, pool_ck, pool_cv, pool_sk, pool_sv, win_k, win_v, page_table):
    b, t = q.shape[:2]
    past = page_table.shape[1] * PAGE_SIZE
    t_cmp = -(-t // CMP_STRIDE) * CMP_STRIDE
    nnb = -(-t // SLC_BLOCK)
    npb = past // SLC_BLOCK
    bpp = PAGE_SIZE // SLC_BLOCK

    def past_rows(pool):
        g = pool[layer, page_table]
        return g.transpose(0, 2, 1, 3, 4).reshape(b, NSA_KV_HEADS, past, HEAD_DIM)

    def pad_t(a, n):
        return jnp.pad(a, ((0, 0), (0, 0), (0, n - a.shape[2]), (0, 0)))

    kc = compress(jnp.concatenate([past_rows(pool_ck), pad_t(kc_r, t_cmp)], axis=2), lw['cmp_pe_k'], lw['cmp_w1_k'], lw['cmp_w2_k'])
    vc = compress(jnp.concatenate([past_rows(pool_cv), pad_t(vc_r, t_cmp)], axis=2), lw['cmp_pe_v'], lw['cmp_w1_v'], lw['cmp_w2_v'])
    new_ks = pad_t(ks_r, nnb * SLC_BLOCK).reshape(b, NSA_KV_HEADS, nnb, SLC_BLOCK, HEAD_DIM)
    new_vs = pad_t(vs_r, nnb * SLC_BLOCK).reshape(b, NSA_KV_HEADS, nnb, SLC_BLOCK, HEAD_DIM)
    bi = jnp.arange(b)[:, None, None, None]
    hi = jnp.arange(NSA_KV_HEADS)[None, None, :, None]
    offs = jnp.arange(SLC_BLOCK)

    def gather_slc(idx):
        in_past = (idx < npb)[..., None, None]
        pidx = jnp.minimum(idx, npb - 1)
        page = page_table[bi, pidx // bpp][..., None]
        rows = ((pidx % bpp) * SLC_BLOCK)[..., None] + offs
        nidx = jnp.clip(idx - npb, 0, nnb - 1)
        ks = jnp.where(in_past, pool_sk[layer, page, hi[..., None], rows], new_ks[bi, hi, nidx])
        vs = jnp.where(in_past, pool_sv[layer, page, hi[..., None], rows], new_vs[bi, hi, nidx])
        return ks, vs

    kw = jnp.concatenate([win_k.astype(kw_r.dtype), kw_r], axis=2)
    vw = jnp.concatenate([win_v.astype(vw_r.dtype), vw_r], axis=2)
    wb = win_k.shape[2]
    o = nsa_attend(q, past + jnp.arange(t), gates, kc, vc, slc_overlap(kc.shape[2], npb + nnb), gather_slc,
                   kw, vw, past - wb + jnp.arange(wb + t), tbl)
    return o, kw[:, :, -wb:], vw[:, :, -wb:]


def token_mixer(h, lw, gla_s0, conv_prev, nsa_fn):
    b, t, _ = h.shape
    f32 = jnp.float32
    (gq, gk, gv, gg, ga, cb, cc, cx, nq, nkc, nvc, nks, nvs, nkw, nvw, ngt, mgt) = jnp.split(
        in_proj(h, lw['w_in']), np.cumsum(IN_SIZES)[:-1].tolist(), axis=-1)
    q = gq.reshape(b, t, GLA_HEADS, GLA_DK).astype(f32) * GLA_DK ** -0.5
    k = gk.reshape(b, t, GLA_HEADS, GLA_DK).astype(f32)
    v = gv.reshape(b, t, GLA_HEADS, GLA_DV).astype(f32)
    log_a = (jax.nn.log_sigmoid((ga @ lw['gla_wa2'] + lw['gla_ba']).astype(f32)) / GLA_TAU).reshape(b, t, GLA_HEADS, GLA_DK)
    o, s_fin = gla_scan(q, k, v, log_a, gla_s0.astype(f32))
    o_gla = gla_output(o, gg, lw['gla_norm_g']).astype(h.dtype)
    o_conv, conv_state = short_conv(cb, cc, cx, conv_prev.astype(h.dtype), lw['conv_w'])

    def rows(a):
        return a.reshape(b, t, NSA_KV_HEADS, HEAD_DIM).transpose(0, 2, 1, 3)
    kc_r, vc_r, ks_r, vs_r, kw_r, vw_r = [rows(a) for a in (nkc, nvc, nks, nvs, nkw, nvw)]
    q_n = nq.reshape(b, t, NSA_KV_HEADS, NSA_GROUP, HEAD_DIM)
    g_n = jax.nn.sigmoid(ngt).reshape(b, t, NSA_KV_HEADS, NSA_GROUP, 3)
    o_n, win_k, win_v = nsa_fn(q_n, g_n, kc_r, vc_r, ks_r, vs_r, kw_r, vw_r)
    s_a, s_b, s_c = jnp.split(jax.nn.sigmoid(mgt), 3, axis=-1)
    merged = (s_a * (o_gla @ lw['w_br_gla']) + s_b * (o_conv @ lw['w_br_conv'])
              + s_c * (o_n.reshape(b, t, NSA_Q) @ lw['w_br_nsa']))
    return merged @ lw['w_o'], (kc_r, vc_r, ks_r, vs_r, win_k, win_v, s_fin.astype(h.dtype), conv_state)


def moe_dispatch(hf, idx, wts, wg, wu, wd):
    n, d = hf.shape
    a = n * TOP_K
    fe = idx.reshape(-1)
    ft = jnp.arange(a) // TOP_K
    fw = wts.reshape(-1)
    order = jnp.argsort(fe)
    se, st, sw = fe[order], ft[order], fw[order]
    counts = jnp.zeros((N_EXPERTS,), jnp.int32).at[fe].add(1)
    starts = jnp.cumsum(counts) - counts
    padded = (counts + MOE_BLOCK - 1) // MOE_BLOCK * MOE_BLOCK
    pend = jnp.cumsum(padded)
    pstart = pend - padded
    dest = pstart[se] + jnp.arange(a) - starts[se]
    nb = -(-a // MOE_BLOCK) + N_EXPERTS
    rows = nb * MOE_BLOCK
    row_tok = jnp.full((rows,), n, jnp.int32).at[dest].set(st)
    row_w = jnp.zeros((rows,), hf.dtype).at[dest].set(sw)
    blk_e = jnp.minimum(jnp.searchsorted(pend, jnp.arange(nb) * MOE_BLOCK, side='right'), N_EXPERTS - 1)
    h_pad = jnp.concatenate([hf, jnp.zeros((1, d), hf.dtype)], axis=0)

    def expert_block(inp):
        tok, e = inp
        xb = h_pad[tok]
        return (jax.nn.silu(xb @ wg[e]) * (xb @ wu[e])) @ wd[e]

    y = lax.map(expert_block, (row_tok.reshape(nb, MOE_BLOCK), blk_e))
    return jnp.zeros((n + 1, d), y.dtype).at[row_tok].add(y.reshape(rows, d) * row_w[:, None])[:n]


def moe(h, lw):
    b, t, d = h.shape
    n = b * t
    hf = h.reshape(n, d)
    s = jax.nn.sigmoid((hf @ lw['w_router']).astype(jnp.float32))
    sel = s + lw['router_bias'].astype(jnp.float32)
    per = N_EXPERTS // N_GROUPS
    gscore = lax.top_k(sel.reshape(n, N_GROUPS, per), 2)[0].sum(-1)
    gidx = lax.top_k(gscore, TOPK_GROUPS)[1]
    gmask = jax.nn.one_hot(gidx, N_GROUPS).sum(1) > 0
    sel = jnp.where(jnp.repeat(gmask, per, axis=1), sel, NEG_INF)
    idx = lax.top_k(sel, TOP_K)[1]
    w = jnp.take_along_axis(s, idx, axis=1)
    w = w / w.sum(-1, keepdims=True) * ROUTED_SCALE
    routed = moe_dispatch(hf, idx, w.astype(h.dtype), lw['w_exp_gate'], lw['w_exp_up'], lw['w_exp_down'])
    shared = (jax.nn.silu(hf @ lw['w_sh_gate']) * (hf @ lw['w_sh_up'])) @ lw['w_sh_down']
    return (routed + shared).reshape(b, t, d)


def block(x, c, lw, mixer):
    mod = (jax.nn.silu(c) @ lw['w_ada'] + lw['b_ada'])[:, None, :]
    sh1, sc1, g1, sh2, sc2, g2 = jnp.split(mod, 6, axis=-1)
    mix, state = mixer(x * (1 + sc1) + sh1)
    x = layer_norm(DN_ALPHA * x + g1 * mix, lw['ln1_g'], lw['ln1_b'])
    x = layer_norm(DN_ALPHA * x + g2 * moe(x * (1 + sc2) + sh2, lw), lw['ln2_g'], lw['ln2_b'])
    return x, state


def kernel(x_prompt, x_sample, cache_cmp_k, cache_cmp_v, cache_slc_k, cache_slc_v, cache_win_k, cache_win_v,
           state_gla, state_conv, page_table, c_prompt, c_sample, w_ada, b_ada, w_in, gla_wa2, gla_ba,
           gla_norm_g, conv_w, cmp_pe_k, cmp_pe_v, cmp_w1_k, cmp_w2_k, cmp_w1_v, cmp_w2_v, rel_bias,
           w_br_gla, w_br_conv, w_br_nsa, w_o, ln1_g, ln1_b, ln2_g, ln2_b, w_router, router_bias,
           w_exp_gate, w_exp_up, w_exp_down, w_sh_gate, w_sh_up, w_sh_down):
    tbl = rel_bias.reshape(REL_BUCKETS, NSA_KV_HEADS, NSA_GROUP)
    xp, xs = x_prompt, x_sample
    p_states, s_states = [], []
    for l in range(DEPTH):
        lw = {
            'w_ada': w_ada[l], 'b_ada': b_ada[l], 'w_in': w_in[l], 'gla_wa2': gla_wa2[l], 'gla_ba': gla_ba[l],
            'gla_norm_g': gla_norm_g[l], 'conv_w': conv_w[l], 'cmp_pe_k': cmp_pe_k[l], 'cmp_pe_v': cmp_pe_v[l],
            'cmp_w1_k': cmp_w1_k[l], 'cmp_w2_k': cmp_w2_k[l], 'cmp_w1_v': cmp_w1_v[l], 'cmp_w2_v': cmp_w2_v[l],
            'w_br_gla': w_br_gla[l], 'w_br_conv': w_br_conv[l], 'w_br_nsa': w_br_nsa[l], 'w_o': w_o[l],
            'ln1_g': ln1_g[l], 'ln1_b': ln1_b[l], 'ln2_g': ln2_g[l], 'ln2_b': ln2_b[l],
            'w_router': w_router[l], 'router_bias': router_bias[l], 'w_exp_gate': w_exp_gate[l],
            'w_exp_up': w_exp_up[l], 'w_exp_down': w_exp_down[l], 'w_sh_gate': w_sh_gate[l],
            'w_sh_up': w_sh_up[l], 'w_sh_down': w_sh_down[l],
        }
        prompt_mixer = functools.partial(
            token_mixer, lw=lw,
            gla_s0=jnp.zeros((xp.shape[0], GLA_HEADS, GLA_DK, GLA_DV), jnp.float32),
            conv_prev=jnp.zeros((xp.shape[0], CONV_W - 1, CONV_DIM), xp.dtype),
            nsa_fn=functools.partial(nsa_prompt, lw=lw, tbl=tbl))
        sample_mixer = functools.partial(
            token_mixer, lw=lw, gla_s0=state_gla[l], conv_prev=state_conv[l],
            nsa_fn=functools.partial(nsa_sample, lw=lw, tbl=tbl, layer=l, pool_ck=cache_cmp_k, pool_cv=cache_cmp_v,
                                     pool_sk=cache_slc_k, pool_sv=cache_slc_v, win_k=cache_win_k[l],
                                     win_v=cache_win_v[l], page_table=page_table))
        xp, st = block(xp, c_prompt, lw, prompt_mixer)
        p_states.append(st)
        xs, st = block(xs, c_sample, lw, sample_mixer)
        s_states.append(st)
    p_out = [jnp.stack(z) for z in zip(*p_states)]
    s_out = [jnp.stack(z) for z in zip(*s_states)]
    return (xp, xs, *p_out, *s_out)


MERGE_ROWS = 256
CONV_HALO = 8


def _layer_norm_rows(z, g, b):
    mu = jnp.mean(z, axis=-1, keepdims=True)
    var = jnp.mean(jnp.square(z - mu), axis=-1, keepdims=True)
    return (z - mu) * lax.rsqrt(var + LN_EPS) * g + b


def _rank_rows(v):
    r = v.shape[0]
    rid = lax.broadcasted_iota(jnp.int32, (r, 1), 0)
    rank = jnp.zeros(v.shape, jnp.int32)
    for j in range(r):
        vj = v[j:j + 1, :]
        rank = rank + ((vj > v) | ((vj == v) & (rid > j))).astype(jnp.int32)
    return rank


def _route_cols(s, bias):
    n = s.shape[1]
    per = N_EXPERTS // N_GROUPS
    sel = s + bias
    gs = []
    for g in range(N_GROUPS):
        sg = sel[g * per:(g + 1) * per, :]
        gs.append(jnp.sum(jnp.where(_rank_rows(sg) < 2, sg, 0.0), axis=0, keepdims=True))
    gkeep = _rank_rows(jnp.concatenate(gs, axis=0)) < TOPK_GROUPS
    keep = jnp.concatenate([jnp.broadcast_to(gkeep[g:g + 1, :], (per, n)) for g in range(N_GROUPS)], axis=0)
    rank = _rank_rows(jnp.where(keep, sel, NEG_INF))
    eid = lax.broadcasted_iota(jnp.int32, (N_EXPERTS, 1), 0)
    ids, ws = [], []
    for k in range(TOP_K):
        hit = rank == k
        ids.append(jnp.sum(jnp.where(hit, eid, 0), axis=0, keepdims=True))
        ws.append(jnp.sum(jnp.where(hit, s, 0.0), axis=0, keepdims=True))
    w = jnp.concatenate(ws, axis=0)
    return jnp.concatenate(ids, axis=0), w / jnp.sum(w, axis=0, keepdims=True) * ROUTED_SCALE


def _merge_kernel(x_ref, cb_ref, cc_ref, cx_ref, ccp_ref, cxp_ref, prev_ref, ma_ref, mb_ref, mc_ref, og_ref, on_ref,
                  cw_ref, wg_ref, wc_ref, wn_ref, wo_ref, g1_ref, sc2_ref, sh2_ref, lng_ref, lnb_ref, wr_ref, rb_ref,
                  x1_ref, h2_ref, idx_ref, rw_ref, cst_ref):
    j = pl.program_id(1)
    tm = x_ref.shape[1]
    u = cc_ref[0] * cx_ref[0]
    halo = jnp.where(j == 0, prev_ref[0], ccp_ref[0] * cxp_ref[0])
    p1 = halo[CONV_HALO - 1:CONV_HALO, :]
    p2 = halo[CONV_HALO - 2:CONV_HALO - 1, :]
    rid = lax.broadcasted_iota(jnp.int32, (tm, 1), 0)
    u1 = jnp.where(rid == 0, p1, pltpu.roll(u, 1, axis=0))
    u2 = jnp.where(rid == 0, p2, jnp.where(rid == 1, p1, pltpu.roll(u, 2, axis=0)))
    cw = cw_ref[...]
    o_conv = cb_ref[0] * (u2 * cw[0:1, :] + u1 * cw[1:2, :] + u * cw[2:3, :])
    cst_ref[0] = u[tm - CONV_HALO:tm, :]
    merged = (jax.nn.sigmoid(ma_ref[0]) * _dot(og_ref[0].astype(BF16), wg_ref[...])
              + jax.nn.sigmoid(mb_ref[0]) * _dot(o_conv.astype(BF16), wc_ref[...])
              + jax.nn.sigmoid(mc_ref[0]) * _dot(on_ref[0].astype(BF16), wn_ref[...]))
    mix = _dot(merged.astype(BF16), wo_ref[...])
    x1 = _layer_norm_rows(DN_ALPHA * x_ref[0] + g1_ref[0] * mix, lng_ref[...], lnb_ref[...])
    x1_ref[0] = x1
    h2 = x1 * (1.0 + sc2_ref[0]) + sh2_ref[0]
    h2_ref[0] = h2.astype(h2_ref.dtype)
    s = jax.nn.sigmoid(lax.dot_general(wr_ref[...], h2, (((1,), (1,)), ((), ())),
                                       precision=lax.Precision.HIGHEST, preferred_element_type=F32))
    idx_ref[0], rw_ref[0] = _route_cols(s, rb_ref[...])


def merge_pallas(x, y, o_gla, o_n, conv_prev, g1, sc2, sh2, lw):
    b, t, d = x.shape
    tm = math.gcd(t, MERGE_ROWS)
    halo = CONV_HALO
    hb = tm // halo
    prev = jnp.concatenate([jnp.zeros((b, halo - (CONV_W - 1), d), F32), conv_prev.astype(F32)], axis=1)
    col = lambda name, k=0: PROJ_OFF[name] // d + k
    tile = lambda cb_: pl.BlockSpec((1, tm, d), lambda i, j: (i, j, cb_))
    halo_spec = lambda cb_: pl.BlockSpec((1, halo, d), lambda i, j: (i, jnp.maximum(j * hb - 1, 0), cb_))
    vec = pl.BlockSpec((1, 1, d), lambda i, j: (i, 0, 0))
    const = lambda a: pl.BlockSpec(a.shape, lambda i, j: (0,) * a.ndim)
    wts = [lw['w_br_gla'].astype(BF16), lw['w_br_conv'].astype(BF16), lw['w_br_nsa'].astype(BF16), lw['w_o'].astype(BF16)]
    lng, lnb = lw['ln1_g'].reshape(1, d), lw['ln1_b'].reshape(1, d)
    wr, rb = lw['w_router'].T, lw['router_bias'].reshape(N_EXPERTS, 1).astype(F32)
    topk = pl.BlockSpec((1, TOP_K, tm), lambda i, j: (i, 0, j))
    x1, h2, idx, rw, cst = pl.pallas_call(
        _merge_kernel,
        grid=(b, t // tm),
        in_specs=[tile(0), tile(col('cb')), tile(col('cc')), tile(col('cx')), halo_spec(col('cc')), halo_spec(col('cx')),
                  pl.BlockSpec((1, halo, d), lambda i, j: (i, 0, 0)),
                  tile(col('mgt', 0)), tile(col('mgt', 1)), tile(col('mgt', 2)), tile(0), tile(0),
                  const(lw['conv_w'])] + [const(w) for w in wts] + [vec, vec, vec, const(lng), const(lnb),
                                                                    const(wr), const(rb)],
        out_specs=[tile(0), tile(0), topk, topk, pl.BlockSpec((1, halo, d), lambda i, j: (i, 0, 0))],
        out_shape=[jax.ShapeDtypeStruct((b, t, d), F32), jax.ShapeDtypeStruct((b, t, d), BF16 if tm % 16 == 0 else F32),
                   jax.ShapeDtypeStruct((b, TOP_K, t), jnp.int32), jax.ShapeDtypeStruct((b, TOP_K, t), F32),
                   jax.ShapeDtypeStruct((b, halo, d), F32)],
        compiler_params=pltpu.CompilerParams(vmem_limit_bytes=VMEM_LIMIT),
        name="merge_ln1",
    )(x, y, y, y, y, y, prev, y, y, y, o_gla, o_n, lw['conv_w'], *wts,
      g1.reshape(b, 1, d), sc2.reshape(b, 1, d), sh2.reshape(b, 1, d), lng, lnb, wr, rb)
    to_rows = lambda a: a.transpose(0, 2, 1).reshape(b * t, TOP_K)
    return x1, h2, to_rows(idx), to_rows(rw), cst[:, halo - (CONV_W - 1):]


def _expert_act(xb, wg, wu):
    hg = _dot(xb, wg)
    return hg * jax.nn.sigmoid(hg) * _dot(xb, wu)


def _grouped_expert_kernel(be_ref, x_ref, wg_ref, wu_ref, wd_ref, o_ref):
    act = _expert_act(x_ref[...], wg_ref[0], wu_ref[0])
    o_ref[...] = _dot(act.astype(BF16), wd_ref[0]).astype(o_ref.dtype)


def moe_routed_sorted(h2, idx, w, lw):
    n, d = h2.shape
    a = n * TOP_K
    onehot = idx[:, :, None] == jnp.arange(N_EXPERTS, dtype=jnp.int32)[None, None, :]
    hit = onehot.any(axis=1).astype(F32)
    ct = math.gcd(n, 512)
    hit_t = hit.reshape(n // ct, ct, N_EXPERTS)
    within = jnp.einsum('ts,nse->nte', jnp.tril(jnp.ones((ct, ct), F32), -1), hit_t)
    tile_tot = hit_t.sum(1)
    before = jnp.cumsum(tile_tot, axis=0) - tile_tot
    pos = (within + before[:, None, :]).reshape(n, N_EXPERTS).astype(jnp.int32)
    counts = tile_tot.sum(0).astype(jnp.int32)
    padded = (counts + MOE_BLOCK - 1) // MOE_BLOCK * MOE_BLOCK
    pend = jnp.cumsum(padded)
    pstart = pend - padded
    slot = jnp.sum(jnp.where(onehot, (pos + pstart[None, :])[:, None, :], 0), axis=-1)
    nb = -(-a // MOE_BLOCK) + N_EXPERTS
    rows = nb * MOE_BLOCK
    row_tok = jnp.zeros((rows,), jnp.int32).at[slot.reshape(-1)].set(jnp.arange(a, dtype=jnp.int32) // TOP_K)
    blk_e = jnp.minimum(jnp.sum(pend[None, :] <= (jnp.arange(nb) * MOE_BLOCK)[:, None], axis=1), N_EXPERTS - 1).astype(jnp.int32)
    xs = h2.astype(BF16)[row_tok]
    wg, wu, wd = lw['w_exp_gate'].astype(BF16), lw['w_exp_up'].astype(BF16), lw['w_exp_down'].astype(BF16)
    hdim = wg.shape[2]
    y = pl.pallas_call(
        _grouped_expert_kernel,
        grid_spec=pltpu.PrefetchScalarGridSpec(
            num_scalar_prefetch=1,
            grid=(nb,),
            in_specs=[pl.BlockSpec((MOE_BLOCK, d), lambda i, be: (i, 0)),
                      pl.BlockSpec((1, d, hdim), lambda i, be: (be[i], 0, 0)),
                      pl.BlockSpec((1, d, hdim), lambda i, be: (be[i], 0, 0)),
                      pl.BlockSpec((1, hdim, d), lambda i, be: (be[i], 0, 0))],
            out_specs=pl.BlockSpec((MOE_BLOCK, d), lambda i, be: (i, 0))),
        out_shape=jax.ShapeDtypeStruct((rows, d), BF16),
        name="moe_grouped",
    )(blk_e, xs, wg, wu, wd)
    return y[slot.T.reshape(-1)].reshape(TOP_K, n, d), w


def _dense_expert_kernel(x_ref, wt_ref, wg_ref, wu_ref, wd_ref, o_ref):
    @pl.when(pl.program_id(0) == 0)
    def _():
        o_ref[...] = jnp.zeros_like(o_ref)
    act = _expert_act(x_ref[...].astype(BF16), wg_ref[0].astype(BF16), wu_ref[0].astype(BF16))
    o_ref[...] += _dot((act * wt_ref[0]).astype(BF16), wd_ref[0].astype(BF16))


def moe_routed_dense(h2, idx, w, lw):
    n, d = h2.shape
    wdense = jnp.zeros((n, N_EXPERTS), F32).at[jnp.arange(n)[:, None], idx].add(w)
    wt = wdense.T.reshape(N_EXPERTS, n, 1)
    hdim = lw['w_exp_gate'].shape[2]
    routed = pl.pallas_call(
        _dense_expert_kernel,
        grid=(N_EXPERTS,),
        in_specs=[pl.BlockSpec((n, d), lambda e: (0, 0)),
                  pl.BlockSpec((1, n, 1), lambda e: (e, 0, 0)),
                  pl.BlockSpec((1, d, hdim), lambda e: (e, 0, 0)),
                  pl.BlockSpec((1, d, hdim), lambda e: (e, 0, 0)),
                  pl.BlockSpec((1, hdim, d), lambda e: (e, 0, 0))],
        out_specs=pl.BlockSpec((n, d), lambda e: (0, 0)),
        out_shape=jax.ShapeDtypeStruct((n, d), F32),
        name="moe_dense",
    )(h2, wt, lw['w_exp_gate'], lw['w_exp_up'], lw['w_exp_down'])
    return routed.reshape(1, n, d), jnp.ones((n, 1), F32)


def _moe_tail_kernel(x1_ref, h2_ref, rw_ref, g2_ref, wg_ref, wu_ref, wd_ref, lng_ref, lnb_ref, *rest):
    r_refs, o_ref = rest[:-1], rest[-1]
    rw = rw_ref[0]
    routed = rw[:, 0:1] * r_refs[0][0, 0].astype(F32)
    for k in range(1, len(r_refs)):
        routed = routed + rw[:, k:k + 1] * r_refs[k][0, 0].astype(F32)
    act = _expert_act(h2_ref[0].astype(BF16), wg_ref[...], wu_ref[...])
    shared = _dot(act.astype(BF16), wd_ref[...])
    o_ref[0] = _layer_norm_rows(DN_ALPHA * x1_ref[0] + g2_ref[0] * (routed + shared), lng_ref[...], lnb_ref[...])


def moe_tail_pallas(x1, h2, rows, rw, g2, lw):
    b, t, d = x1.shape
    nk = rw.shape[1]
    tm = math.gcd(t, MERGE_ROWS)
    tile = pl.BlockSpec((1, tm, d), lambda i, j: (i, j, 0))
    const = lambda a: pl.BlockSpec(a.shape, lambda i, j: (0,) * a.ndim)
    wg, wu, wd = lw['w_sh_gate'].astype(BF16), lw['w_sh_up'].astype(BF16), lw['w_sh_down'].astype(BF16)
    lng, lnb = lw['ln2_g'].reshape(1, d), lw['ln2_b'].reshape(1, d)
    rows = rows.reshape(nk, b, t, d)
    row_specs = [pl.BlockSpec((1, 1, tm, d), functools.partial(lambda i, j, k: (k, i, j, 0), k=k)) for k in range(nk)]
    return pl.pallas_call(
        _moe_tail_kernel,
        grid=(b, t // tm),
        in_specs=[tile, tile, pl.BlockSpec((1, tm, nk), lambda i, j: (i, j, 0)),
                  pl.BlockSpec((1, 1, d), lambda i, j: (i, 0, 0)),
                  const(wg), const(wu), const(wd), const(lng), const(lnb)] + row_specs,
        out_specs=tile,
        out_shape=jax.ShapeDtypeStruct((b, t, d), F32),
        compiler_params=pltpu.CompilerParams(vmem_limit_bytes=VMEM_LIMIT),
        name="moe_tail_ln2",
    )(x1, h2, rw.reshape(b, t, nk), g2.reshape(b, 1, d), wg, wu, wd, lng, lnb, *([rows] * nk))


def layer_block(x, c, lw, gla_s0, conv_prev, nsa_fn, sorted_moe):
    b, t, d = x.shape
    mod = jax.nn.silu(c) @ lw['w_ada'] + lw['b_ada']
    sh1, sc1, g1, sh2, sc2, g2 = jnp.split(mod, 6, axis=-1)
    if t % LANES == 0:
        y = in_proj(x, sc1, sh1, lw['w_in'])
    else:
        h1 = (x * (1.0 + sc1[:, None, :]) + sh1[:, None, :]).reshape(1, b * t, d)
        y = in_proj(h1, jnp.zeros((1, d), F32), jnp.zeros((1, d), F32), lw['w_in']).reshape(b, t, PROJ_COLS)
    o_gla, s_fin = gla_pallas(y, lw, gla_s0)

    def rows(name):
        return proj_cols(y, name).reshape(b, t, NSA_KV_HEADS, HEAD_DIM).transpose(0, 2, 1, 3)
    kv_rows = [rows(nm) for nm in ('nkc', 'nvc', 'nks', 'nvs', 'nkw', 'nvw')]
    o_n, win_k, win_v = nsa_fn(y, *kv_rows)
    x1, h2, idx, w, conv_state = merge_pallas(x, y, o_gla, o_n, conv_prev, g1, sc2, sh2, lw)
    rows_out, rw = (moe_routed_sorted if sorted_moe else moe_routed_dense)(h2.reshape(b * t, d), idx, w, lw)
    x2 = moe_tail_pallas(x1, h2, rows_out, rw, g2, lw)
    return x2, (*kv_rows[:4], win_k, win_v, s_fin, conv_state)


def kernel(x_prompt, x_sample, cache_cmp_k, cache_cmp_v, cache_slc_k, cache_slc_v, cache_win_k, cache_win_v,
           state_gla, state_conv, page_table, c_prompt, c_sample, w_ada, b_ada, w_in, gla_wa2, gla_ba,
           gla_norm_g, conv_w, cmp_pe_k, cmp_pe_v, cmp_w1_k, cmp_w2_k, cmp_w1_v, cmp_w2_v, rel_bias,
           w_br_gla, w_br_conv, w_br_nsa, w_o, ln1_g, ln1_b, ln2_g, ln2_b, w_router, router_bias,
           w_exp_gate, w_exp_up, w_exp_down, w_sh_gate, w_sh_up, w_sh_down):
    params = dict(w_ada=w_ada, b_ada=b_ada, w_in=w_in, gla_wa2=gla_wa2, gla_ba=gla_ba, gla_norm_g=gla_norm_g,
                  conv_w=conv_w, cmp_pe_k=cmp_pe_k, cmp_pe_v=cmp_pe_v, cmp_w1_k=cmp_w1_k, cmp_w2_k=cmp_w2_k,
                  cmp_w1_v=cmp_w1_v, cmp_w2_v=cmp_w2_v, w_br_gla=w_br_gla, w_br_conv=w_br_conv, w_br_nsa=w_br_nsa,
                  w_o=w_o, ln1_g=ln1_g, ln1_b=ln1_b, ln2_g=ln2_g, ln2_b=ln2_b, w_router=w_router,
                  router_bias=router_bias, w_exp_gate=w_exp_gate, w_exp_up=w_exp_up, w_exp_down=w_exp_down,
                  w_sh_gate=w_sh_gate, w_sh_up=w_sh_up, w_sh_down=w_sh_down)
    tbl = rel_bias.reshape(REL_BUCKETS, NSA_KV_HEADS, NSA_GROUP)
    bp, tp, _ = x_prompt.shape
    bs, ts, _ = x_sample.shape
    bias_tables = nsa_bias_tables(tbl, tp, tp // CMP_STRIDE)
    sample_tables = nsa_sample_tables(tbl, page_table.shape[1] * PAGE_SIZE, ts, cache_win_k.shape[3])
    xp, xs = x_prompt, x_sample
    p_states, s_states = [], []
    for l in range(DEPTH):
        lw = {k: v[l] for k, v in params.items()}

        def prompt_nsa(y, kc_r, vc_r, ks_r, vs_r, kw_r, vw_r):
            o_n = nsa_prompt_pallas(y, PROJ_OFF['nq'], proj_cols(y, 'ngt'), kc_r, vc_r, ks_r, vs_r, kw_r, vw_r,
                                    lw, bias_tables)
            wb = min(WINDOW, tp)
            return o_n, kw_r[:, :, tp - wb:], vw_r[:, :, tp - wb:]

        def sample_nsa(y, kc_r, vc_r, ks_r, vs_r, kw_r, vw_r):
            o_n = nsa_sample_pallas(y, proj_cols(y, 'ngt'), (kc_r, vc_r, ks_r, vs_r, kw_r, vw_r), lw, sample_tables, l,
                                    (cache_cmp_k, cache_cmp_v, cache_slc_k, cache_slc_v), cache_win_k, cache_win_v,
                                    page_table)
            wb = cache_win_k.shape[3]
            win_k = jnp.concatenate([cache_win_k[l], kw_r], axis=2)[:, :, -wb:]
            win_v = jnp.concatenate([cache_win_v[l], vw_r], axis=2)[:, :, -wb:]
            return o_n, win_k, win_v

        xp, st = layer_block(xp, c_prompt, lw, jnp.zeros((bp, GLA_HEADS, GLA_DK, GLA_DV), F32),
                             jnp.zeros((bp, CONV_W - 1, CONV_DIM), F32), prompt_nsa, True)
        p_states.append(st)
        xs, st = layer_block(xs, c_sample, lw, state_gla[l], state_conv[l], sample_nsa, False)
        s_states.append(st)
    p_out = [jnp.stack(z) for z in zip(*p_states)]
    s_out = [jnp.stack(z) for z in zip(*s_states)]
    return (xp, xs, *p_out, *s_out)
```

```python
import math, functools
import jax, jax.numpy as jnp
from jax import lax
import numpy as np
from jax.experimental import pallas as pl
from jax.experimental.pallas import tpu as pltpu

D_MODEL = 1024
DEPTH = 2
PAGE_SIZE = 128
GLA_HEADS = 4
GLA_DK = 128
GLA_DV = 256
GLA_RANK = 16
GLA_TAU = 16.0
GLA_CHUNK = 64
CONV_DIM = D_MODEL
CONV_W = 3
NSA_HEADS = 16
NSA_KV_HEADS = 4
NSA_GROUP = NSA_HEADS // NSA_KV_HEADS
HEAD_DIM = 64
CMP_BLOCK = 32
CMP_STRIDE = 16
CMP_HIDDEN = 128
SLC_BLOCK = 64
SLC_TOPK = 16
WINDOW = 512
NSA_QCHUNK = 16
REL_BUCKETS = 32
REL_MAX_DIST = 128
N_EXPERTS = 64
TOP_K = 8
N_GROUPS = 8
TOPK_GROUPS = 4
EXPERT_HIDDEN = 256
SHARED_HIDDEN = 256
ROUTED_SCALE = 2.5
MOE_BLOCK = 256
DN_ALPHA = (2 * DEPTH) ** 0.25
LN_EPS = 1e-5
NEG_INF = -1e30

GLA_QK = GLA_HEADS * GLA_DK
GLA_V = GLA_HEADS * GLA_DV
NSA_Q = NSA_HEADS * HEAD_DIM
NSA_KV = NSA_KV_HEADS * HEAD_DIM
IN_SIZES = (GLA_QK, GLA_QK, GLA_V, GLA_V, GLA_RANK, CONV_DIM, CONV_DIM, CONV_DIM, NSA_Q, NSA_KV, NSA_KV, NSA_KV, NSA_KV, NSA_KV, NSA_KV, 3 * NSA_HEADS, 3 * D_MODEL)
IN_TOTAL = sum(IN_SIZES)


BF16 = jnp.bfloat16
F32 = jnp.float32
LANES = 128
NSA_TILE = 128
GLA_ROWS = 256
VMEM_LIMIT = 48 * 1024 * 1024

PROJ_NAMES = ('gq', 'gk', 'gv', 'gg', 'ga', 'cb', 'cc', 'cx', 'nq', 'nkc', 'nvc', 'nks', 'nvs', 'nkw', 'nvw', 'ngt', 'mgt')
PROJ_ORDER = ('gv', 'gg', 'cb', 'cc', 'cx', 'nq', 'mgt', 'gq', 'gk', 'nkc', 'nvc', 'nks', 'nvs', 'nkw', 'nvw', 'ga', 'ngt')
PROJ_SIZE = dict(zip(PROJ_NAMES, IN_SIZES))
PROJ_SRC = dict(zip(PROJ_NAMES, np.cumsum((0,) + IN_SIZES[:-1]).tolist()))
PROJ_OFF = dict(zip(PROJ_ORDER, np.cumsum([0] + [PROJ_SIZE[n] for n in PROJ_ORDER[:-1]]).tolist()))
PROJ_TN = 1024
PROJ_COLS = -(-IN_TOTAL // PROJ_TN) * PROJ_TN


def _dot_nt(a, b):
    return lax.dot_general(a, b, (((1,), (1,)), ((), ())), preferred_element_type=F32)


def _dot(a, b):
    return jnp.dot(a, b, preferred_element_type=F32)


def _dot_hi(a, b):
    return jnp.dot(a, b, precision=lax.Precision.HIGHEST, preferred_element_type=F32)


def _in_proj_kernel(x_ref, sc_ref, sh_ref, w_ref, o_ref, h_ref):
    @pl.when(pl.program_id(2) == 0)
    def _():
        h_ref[...] = (x_ref[0] * (1.0 + sc_ref[0]) + sh_ref[0]).astype(BF16)
    o_ref[0] = _dot(h_ref[...], w_ref[...])


def in_proj(x, sc, sh, w_in):
    b, t, d = x.shape
    w = jnp.concatenate([w_in[:, PROJ_SRC[n]:PROJ_SRC[n] + PROJ_SIZE[n]] for n in PROJ_ORDER]
                        + [jnp.zeros((d, PROJ_COLS - IN_TOTAL), w_in.dtype)], axis=1).astype(BF16)
    tm = math.gcd(t, 1024)
    return pl.pallas_call(
        _in_proj_kernel,
        grid=(b, t // tm, PROJ_COLS // PROJ_TN),
        in_specs=[pl.BlockSpec((1, tm, d), lambda i, j, k: (i, j, 0)),
                  pl.BlockSpec((1, 1, d), lambda i, j, k: (i, 0, 0)),
                  pl.BlockSpec((1, 1, d), lambda i, j, k: (i, 0, 0)),
                  pl.BlockSpec((d, PROJ_TN), lambda i, j, k: (0, k))],
        out_specs=pl.BlockSpec((1, tm, PROJ_TN), lambda i, j, k: (i, j, k)),
        out_shape=jax.ShapeDtypeStruct((b, t, PROJ_COLS), F32),
        scratch_shapes=[pltpu.VMEM((tm, d), BF16)],
        compiler_params=pltpu.CompilerParams(vmem_limit_bytes=VMEM_LIMIT),
        name="in_proj",
    )(x, sc.reshape(b, 1, d), sh.reshape(b, 1, d), w)


def proj_cols(y, name):
    return y[..., PROJ_OFF[name]:PROJ_OFF[name] + PROJ_SIZE[name]]


def _gla_kernel(q_ref, k_ref, v_ref, gg_ref, sm_ref, wa2_ref, ba_ref, ng_ref, s0_ref, o_ref, sfin_ref, st_ref,
                *, chunk, nchunks):
    tb = pl.program_id(2)
    c = chunk

    @pl.when(tb == 0)
    def _():
        st_ref[...] = s0_ref[0, 0].T

    row = lax.broadcasted_iota(jnp.int32, (c, c), 0)
    col = lax.broadcasted_iota(jnp.int32, (c, c), 1)
    causal = row >= col
    tri = jnp.where(causal, 1.0, 0.0)
    ga_off = PROJ_OFF['ga'] % LANES
    for ci in range(nchunks):
        sl = slice(ci * c, (ci + 1) * c)
        q = q_ref[0, sl, :] * (GLA_DK ** -0.5)
        k = k_ref[0, sl, :]
        v = v_ref[0, sl, :].astype(BF16)
        ga = sm_ref[0, sl, ga_off:ga_off + GLA_RANK]
        x = _dot_hi(ga, wa2_ref[...]) + ba_ref[...]
        log_a = (jnp.minimum(x, 0.0) - jnp.log1p(jnp.exp(-jnp.abs(x)))) * (1.0 / GLA_TAU)
        cum = _dot_hi(tri, log_a)
        last = cum[c - 1:c, :]
        mid = cum[c // 2 - 1:c // 2, :]
        st = st_ref[...]
        inter = _dot_nt((q * jnp.exp(cum)).astype(BF16), st.astype(BF16))
        att = _dot_nt((q * jnp.exp(cum - mid)).astype(BF16), (k * jnp.exp(mid - cum)).astype(BF16))
        att = jnp.where(causal, att, 0.0)
        o = inter + _dot(att.astype(BF16), v)
        kd = (k * jnp.exp(last - cum)).astype(BF16)
        st_ref[...] = st * jnp.exp(last) + lax.dot_general(v, kd, (((0,), (0,)), ((), ())),
                                                           preferred_element_type=F32)
        mu = jnp.mean(o, axis=-1, keepdims=True)
        var = jnp.mean(jnp.square(o - mu), axis=-1, keepdims=True)
        gate = gg_ref[0, sl, :]
        o_ref[0, sl, :] = (o - mu) * lax.rsqrt(var + LN_EPS) * ng_ref[...] * (gate * jax.nn.sigmoid(gate))

    @pl.when(tb == pl.num_programs(2) - 1)
    def _():
        sfin_ref[0, 0] = st_ref[...].T


def gla_pallas(y, lw, s0):
    b, t, _ = y.shape
    c = math.gcd(t, GLA_CHUNK)
    r = math.gcd(t, GLA_ROWS)
    qb, kb = PROJ_OFF['gq'] // GLA_DK, PROJ_OFF['gk'] // GLA_DK
    vb, gb = PROJ_OFF['gv'] // GLA_DV, PROJ_OFF['gg'] // GLA_DV
    sb = PROJ_OFF['ga'] // LANES
    return pl.pallas_call(
        functools.partial(_gla_kernel, chunk=c, nchunks=r // c),
        grid=(b, GLA_HEADS, t // r),
        in_specs=[pl.BlockSpec((1, r, GLA_DK), lambda i, h, j: (i, j, qb + h)),
                  pl.BlockSpec((1, r, GLA_DK), lambda i, h, j: (i, j, kb + h)),
                  pl.BlockSpec((1, r, GLA_DV), lambda i, h, j: (i, j, vb + h)),
                  pl.BlockSpec((1, r, GLA_DV), lambda i, h, j: (i, j, gb + h)),
                  pl.BlockSpec((1, r, LANES), lambda i, h, j: (i, j, sb)),
                  pl.BlockSpec((GLA_RANK, GLA_DK), lambda i, h, j: (0, h)),
                  pl.BlockSpec((1, GLA_DK), lambda i, h, j: (0, h)),
                  pl.BlockSpec((1, GLA_DV), lambda i, h, j: (0, h)),
                  pl.BlockSpec((1, 1, GLA_DK, GLA_DV), lambda i, h, j: (i, h, 0, 0))],
        out_specs=[pl.BlockSpec((1, r, GLA_DV), lambda i, h, j: (i, j, h)),
                   pl.BlockSpec((1, 1, GLA_DK, GLA_DV), lambda i, h, j: (i, h, 0, 0))],
        out_shape=[jax.ShapeDtypeStruct((b, t, GLA_V), F32),
                   jax.ShapeDtypeStruct((b, GLA_HEADS, GLA_DK, GLA_DV), F32)],
        scratch_shapes=[pltpu.VMEM((GLA_DV, GLA_DK), F32)],
        name="gla_scan",
    )(y, y, y, y, y, lw['gla_wa2'], lw['gla_ba'].reshape(1, -1), lw['gla_norm_g'].reshape(1, -1), s0.astype(F32))


def _split3(x):
    hi = x.astype(BF16)
    r = x - hi.astype(F32)
    mid = r.astype(BF16)
    lo = (r - mid.astype(F32)).astype(BF16)
    return hi, mid, lo


def _gelu_tanh(x):
    return 0.5 * x * (1.0 + jnp.tanh(math.sqrt(2.0 / math.pi) * (x + 0.044715 * (x * x * x))))


def _compress_kernel(chk_ref, chv_ref, pek_ref, pev_ref, w1k_ref, w1v_ref, w2k_ref, w2v_ref, ok_ref, ov_ref):
    rows = CMP_STRIDE * HEAD_DIM
    for ch_ref, pe_ref, w1_ref, w2_ref, o_ref in ((chk_ref, pek_ref, w1k_ref, w2k_ref, ok_ref),
                                                   (chv_ref, pev_ref, w1v_ref, w2v_ref, ov_ref)):
        ch = ch_ref[0, 0].astype(BF16)
        nch = ch.shape[0]
        w1 = w1_ref[...]
        pe = jnp.broadcast_to(pe_ref[...], (8, pe_ref.shape[1])).astype(BF16)
        hid = _dot(pe, w1)[0:1, :]
        a0 = _dot(ch, w1[0:rows, :])
        a1 = _dot(ch, w1[rows:2 * rows, :])
        hid = hid + a0 + pltpu.roll(a1, nch - 1, axis=0)
        o_ref[0, 0] = _dot(_gelu_tanh(hid).astype(BF16), w2_ref[...])


def compress_pair(kc_r, vc_r, lw):
    b, kvh, length, hd = kc_r.shape
    nch = length // CMP_STRIDE
    chk = kc_r.reshape(b, kvh, nch, CMP_STRIDE * hd)
    chv = vc_r.reshape(b, kvh, nch, CMP_STRIDE * hd)
    ch_spec = pl.BlockSpec((1, 1, nch, CMP_STRIDE * hd), lambda i, j: (i, j, 0, 0))
    full = lambda a: pl.BlockSpec(a.shape, lambda i, j: (0,) * a.ndim)
    pek = lw['cmp_pe_k'].reshape(1, -1)
    pev = lw['cmp_pe_v'].reshape(1, -1)
    w1k, w1v = lw['cmp_w1_k'].astype(BF16), lw['cmp_w1_v'].astype(BF16)
    w2k, w2v = lw['cmp_w2_k'].astype(BF16), lw['cmp_w2_v'].astype(BF16)
    o_spec = pl.BlockSpec((1, 1, nch, hd), lambda i, j: (i, j, 0, 0))
    o_shape = jax.ShapeDtypeStruct((b, kvh, nch, hd), F32)
    return pl.pallas_call(
        _compress_kernel,
        grid=(b, kvh),
        in_specs=[ch_spec, ch_spec, full(pek), full(pev), full(w1k), full(w1v), full(w2k), full(w2v)],
        out_specs=[o_spec, o_spec],
        out_shape=[o_shape, o_shape],
        name="nsa_compress",
    )(chk, chv, pek, pev, w1k, w1v, w2k, w2v)


def _dot_tn(a, b):
    return lax.dot_general(a, b, (((0,), (0,)), ((), ())), preferred_element_type=F32)


def _nsa_t_kernel(q_ref, g_ref, kc_ref, vc_ref, ks_ref, vs_ref, kw_ref, vw_ref, bc_ref, bt_ref, ov_ref, o_ref,
                  acc_ref, *, nsb):
    tq = NSA_TILE
    grp = NSA_GROUP
    hd = HEAD_DIM
    wq = grp * tq
    ncp = kc_ref.shape[2]
    qi = pl.program_id(2)
    q0 = qi * tq
    q = q_ref[0] * (hd ** -0.5)
    q4 = jnp.concatenate([q[:, g * hd:(g + 1) * hd] for g in range(grp)], axis=0).astype(BF16)
    lane_q = lax.broadcasted_iota(jnp.int32, (1, tq), 1)
    qpos1 = q0 + lane_q
    qpos = jnp.concatenate([qpos1] * grp, axis=1)

    kc = kc_ref[0, 0].astype(BF16)
    vc = vc_ref[0, 0].astype(BF16)
    cend = CMP_STRIDE * lax.broadcasted_iota(jnp.int32, (ncp, 1), 0) + (CMP_BLOCK - 1)
    mask_c = cend <= qpos
    lc = jnp.where(mask_c, _dot_nt(kc, q4) + bc_ref[0, 0], NEG_INF)
    mc = jnp.max(lc, axis=0, keepdims=True)
    pc = jnp.where(mask_c, jnp.exp(lc - mc), 0.0)
    lsum = jnp.sum(pc, axis=0, keepdims=True)
    pc = pc / jnp.where(lsum > 0.0, lsum, 1.0)
    o_c = _dot_tn(vc, pc.astype(BF16))
    psum = pc[:, 0:tq]
    for g in range(1, grp):
        psum = psum + pc[:, g * tq:(g + 1) * tq]

    ov = ov_ref[...]
    hi, mid, lo = _split3(psum)
    imp = (_dot(ov, hi) + _dot(ov, mid) + _dot(ov, lo))[0:nsb, :]
    blk = lax.broadcasted_iota(jnp.int32, (nsb, 1), 0)
    cur = lax.shift_right_logical(qpos1, 6)
    forced = (blk == 0) | (blk == cur) | (blk == cur - 1)
    visible = blk * SLC_BLOCK <= qpos1
    score = jnp.where(forced, -NEG_INF, imp)
    score = jnp.where(visible, score, NEG_INF)
    rank = jnp.zeros((nsb, tq), jnp.int32)
    for i in range(nsb):
        si = score[i:i + 1, :]
        beats = (si > score) | ((si == score) & (blk > i))
        rank = rank + beats.astype(jnp.int32)
    sel1 = jnp.where(visible & (rank < SLC_TOPK), 1.0, 0.0).astype(BF16)
    sel1 = jnp.concatenate([sel1, jnp.zeros((LANES - nsb, tq), BF16)], axis=0)
    sel = jnp.concatenate([sel1] * grp, axis=1)

    neg = jnp.full((1, wq), NEG_INF, F32)
    zero = jnp.zeros((1, wq), F32)
    krow = lax.broadcasted_iota(jnp.int32, (tq, 1), 0)
    lane4 = jnp.concatenate([lane_q] * grp, axis=1)

    def slc_logits(kt, causal, valid):
        k0 = pl.multiple_of(kt * tq, tq)
        k_t = ks_ref[0, 0, pl.ds(k0, tq), :].astype(BF16)
        v_t = vs_ref[0, 0, pl.ds(k0, tq), :].astype(BF16)
        kblk = lax.shift_right_logical(k0 + krow, 6)
        expand = jnp.where(kblk == lax.broadcasted_iota(jnp.int32, (1, LANES), 1), 1.0, 0.0).astype(BF16)
        mask = _dot(expand, sel) > 0.5
        if causal:
            mask = mask & (krow <= lane4)
        if valid is not None:
            mask = mask & valid
        return jnp.where(mask, _dot_nt(k_t, q4) + bt_ref[0, jnp.minimum(qi - kt, 2)], NEG_INF), v_t

    def slc_pair(a, b, m, l):
        (sa, va), (sb, vb) = a, b
        m_new = jnp.maximum(m, jnp.maximum(jnp.max(sa, axis=0, keepdims=True), jnp.max(sb, axis=0, keepdims=True)))
        alpha = jnp.exp(m - m_new)
        pa, pb = jnp.exp(sa - m_new), jnp.exp(sb - m_new)
        l = alpha * l + jnp.sum(pa, axis=0, keepdims=True) + jnp.sum(pb, axis=0, keepdims=True)
        acc_ref[...] = alpha * acc_ref[...] + _dot_tn(va, pa.astype(BF16)) + _dot_tn(vb, pb.astype(BF16))
        return m_new, l

    acc_ref[...] = jnp.zeros_like(acc_ref)
    m_s, l_s = lax.fori_loop(
        0, lax.shift_right_logical(qi, 1),
        lambda j, c: slc_pair(slc_logits(2 * j, False, None), slc_logits(2 * j + 1, False, None), *c), (neg, zero))
    m_s, l_s = slc_pair(slc_logits(jnp.maximum(qi - 1, 0), False, (qi & 1) == 1), slc_logits(qi, True, None), m_s, l_s)
    o_s = acc_ref[...] / l_s

    nwt = WINDOW // tq
    s_w, v_w = [], []
    for dt in range(nwt, -1, -1):
        kt = qi - dt
        k0 = pl.multiple_of(jnp.maximum(kt, 0) * tq, tq)
        k_t = kw_ref[0, 0, pl.ds(k0, tq), :].astype(BF16)
        v_w.append(vw_ref[0, 0, pl.ds(k0, tq), :].astype(BF16))
        s = _dot_nt(k_t, q4) + bt_ref[0, min(dt, 2)]
        if dt == nwt:
            s = jnp.where((krow > lane4) & (kt >= 0), s, NEG_INF)
        elif dt == 0:
            s = jnp.where(krow <= lane4, s, NEG_INF)
        else:
            s = jnp.where(kt >= 0, s, NEG_INF)
        s_w.append(s)
    m_w = functools.reduce(jnp.maximum, [jnp.max(s, axis=0, keepdims=True) for s in s_w])
    p_w = [jnp.exp(s - m_w) for s in s_w]
    l_w = functools.reduce(jnp.add, [jnp.sum(p, axis=0, keepdims=True) for p in p_w])
    o_w = functools.reduce(jnp.add, [_dot_tn(v, p.astype(BF16)) for v, p in zip(v_w, p_w)]) / l_w

    gates = jax.nn.sigmoid(g_ref[0, 0])
    outs = []
    for g in range(grp):
        sl = slice(g * tq, (g + 1) * tq)
        outs.append(gates[:, 3 * g:3 * g + 1] * o_c[:, sl].T + gates[:, 3 * g + 1:3 * g + 2] * o_s[:, sl].T
                    + gates[:, 3 * g + 2:3 * g + 3] * o_w[:, sl].T)
    o_ref[0] = jnp.concatenate(outs, axis=-1)


def nsa_bias_tables(tbl, t, ncp):
    tq = NSA_TILE
    kvh, grp = tbl.shape[1], tbl.shape[2]
    dist_c = jnp.arange(t)[None, :] - (CMP_STRIDE * jnp.arange(ncp) + CMP_BLOCK - 1)[:, None]
    bc = tbl[rel_bucket(dist_c)]
    bc = bc.reshape(ncp, t // tq, tq, kvh, grp).transpose(3, 1, 0, 4, 2).reshape(kvh, t // tq, ncp, grp * tq)
    cr = jnp.arange(tq)[None, :] - jnp.arange(tq)[:, None]
    tiles = jnp.stack([cr, cr + tq, jnp.full_like(cr, REL_MAX_DIST)])
    bt = tbl[rel_bucket(tiles)]
    bt = bt.transpose(3, 0, 1, 4, 2).reshape(kvh, 3, tq, grp * tq)
    return bc, bt


def nsa_prompt_pallas(y, q_off, ngt, kc_r, vc_r, ks_r, vs_r, kw_r, vw_r, lw, bias_tables):
    b, t, _ = y.shape
    kvh, grp, hd, tq = NSA_KV_HEADS, NSA_GROUP, HEAD_DIM, NSA_TILE
    qb = q_off // (grp * hd)
    kc, vc = compress_pair(kc_r, vc_r, lw)
    ncp = kc.shape[2]
    nsb = t // SLC_BLOCK
    bc, bt = bias_tables
    gates = ngt.reshape(b, t, kvh, grp * 3).transpose(0, 2, 1, 3)
    ci = jnp.arange(ncp)[None, :] * CMP_STRIDE
    sj = jnp.arange(LANES)[:, None] * SLC_BLOCK
    ov = ((ci < sj + SLC_BLOCK) & (ci + CMP_BLOCK > sj) & (jnp.arange(ncp)[None, :] < ncp - 1)
          & (jnp.arange(LANES)[:, None] < nsb)).astype(BF16)
    row_spec = pl.BlockSpec((1, 1, t, hd), lambda i, j, k: (i, j, 0, 0))
    cmp_spec = pl.BlockSpec((1, 1, ncp, hd), lambda i, j, k: (i, j, 0, 0))
    return pl.pallas_call(
        functools.partial(_nsa_t_kernel, nsb=nsb),
        grid=(b, kvh, t // tq),
        in_specs=[pl.BlockSpec((1, tq, grp * hd), lambda i, j, k: (i, k, qb + j)),
                  pl.BlockSpec((1, 1, tq, grp * 3), lambda i, j, k: (i, j, k, 0)),
                  cmp_spec, cmp_spec, row_spec, row_spec, row_spec, row_spec,
                  pl.BlockSpec((1, 1, ncp, grp * tq), lambda i, j, k: (j, k, 0, 0)),
                  pl.BlockSpec((1, 3, tq, grp * tq), lambda i, j, k: (j, 0, 0, 0)),
                  pl.BlockSpec(ov.shape, lambda i, j, k: (0, 0))],
        out_specs=pl.BlockSpec((1, tq, grp * hd), lambda i, j, k: (i, k, j)),
        out_shape=jax.ShapeDtypeStruct((b, t, kvh * grp * hd), F32),
        scratch_shapes=[pltpu.VMEM((hd, grp * tq), F32)],
        compiler_params=pltpu.CompilerParams(vmem_limit_bytes=VMEM_LIMIT),
        name="nsa_prompt",
    )(y, gates, kc, vc, ks_r, vs_r, kw_r, vw_r, bc, bt, ov)


SMP_TILE = 512


def _nsa_sample_kernel(pt_ref, q_ref, g_ref, nkc_ref, nvc_ref, nks_ref, nvs_ref, nkw_ref, nvw_ref, wk_ref, wv_ref,
                       pek_ref, pev_ref, w1k_ref, w1v_ref, w2k_ref, w2v_ref, bcs_ref, bsl_ref, bwn_ref, bnew_ref,
                       bfar_ref, ov_ref, gm_ref, ck_hbm, cv_hbm, sk_hbm, sv_hbm, o_ref,
                       buf_a, buf_b, stage_a, stage_b, sel_ref, sem, *, layer, past, nt):
    b = pl.program_id(0)
    h = pl.program_id(1)
    grp, hd = NSA_GROUP, HEAD_DIM
    npages = past // PAGE_SIZE
    nch = past // CMP_STRIDE
    nblk = ov_ref.shape[0]

    def page_copy(pool, stage, p, slot):
        return pltpu.make_async_copy(pool.at[layer, pt_ref[b, p], h],
                                     stage.at[:, pl.ds(pl.multiple_of(p * PAGE_SIZE, PAGE_SIZE), PAGE_SIZE)],
                                     sem.at[slot])

    def start_pages(pool_a, pool_b):
        def body(p, c):
            page_copy(pool_a, stage_a, p, 0).start()
            page_copy(pool_b, stage_b, p, 1).start()
            return c
        lax.fori_loop(0, npages, body, 0)

    def wait_pages(pool_a, pool_b):
        def body(p, c):
            page_copy(pool_a, stage_a, p, 0).wait()
            page_copy(pool_b, stage_b, p, 1).wait()
            return c
        lax.fori_loop(0, npages, body, 0)

    def land_pages():
        pad = jnp.zeros((PAGE_SIZE - hd, PAGE_SIZE), F32)

        def body(p, c):
            c0 = pl.multiple_of(p * PAGE_SIZE, PAGE_SIZE)
            for stage, buf in ((stage_a, buf_a), (stage_b, buf_b)):
                sq = jnp.concatenate([stage[:, pl.ds(c0, PAGE_SIZE)], pad], axis=0).T
                buf[pl.ds(c0, PAGE_SIZE), :] = sq[:, 0:hd]
            return c
        lax.fori_loop(0, npages, body, 0, unroll=8)

    start_pages(ck_hbm, cv_hbm)

    q = q_ref[0] * (hd ** -0.5)
    q4 = jnp.concatenate([q[:, g * hd:(g + 1) * hd] for g in range(grp)]
                         + [jnp.zeros((LANES - grp * nt, hd), F32)], axis=0).astype(BF16)
    lane = lax.broadcasted_iota(jnp.int32, (1, LANES), 1)
    tok = lane & (nt - 1)
    qpos = past + tok

    wait_pages(ck_hbm, cv_hbm)
    land_pages()
    start_pages(sk_hbm, sv_hbm)

    rows = CMP_STRIDE * hd

    def summaries(buf, new_ref, pe_ref, w1_ref, w2_ref):
        w1 = w1_ref[...]
        a0 = jnp.zeros((nch, CMP_HIDDEN), F32)
        a1 = jnp.zeros((nch, CMP_HIDDEN), F32)
        for j in range(CMP_STRIDE):
            xj = buf[pl.ds(j, nch, stride=CMP_STRIDE), :].astype(BF16)
            a0 = a0 + _dot(xj, w1[j * hd:(j + 1) * hd, :])
            a1 = a1 + _dot(xj, w1[rows + j * hd:rows + (j + 1) * hd, :])
        new = new_ref[0, 0].astype(BF16)
        a1_new = jnp.zeros((1, CMP_HIDDEN), F32)
        for j in range(nt):
            a1_new = a1_new + _dot(new, w1[rows + j * hd:rows + (j + 1) * hd, :])[j:j + 1, :]
        pe = jnp.broadcast_to(pe_ref[...], (8, pe_ref.shape[1])).astype(BF16)
        rid = lax.broadcasted_iota(jnp.int32, (nch, 1), 0)
        hid = _dot(pe, w1)[0:1, :] + a0 + jnp.where(rid == nch - 1, a1_new, pltpu.roll(a1, nch - 1, axis=0))
        return _dot(_gelu_tanh(hid).astype(BF16), w2_ref[...]).astype(BF16)

    kc = summaries(buf_a, nkc_ref, pek_ref, w1k_ref, w2k_ref)
    vc = summaries(buf_b, nvc_ref, pev_ref, w1v_ref, w2v_ref)

    cend = CMP_STRIDE * lax.broadcasted_iota(jnp.int32, (nch, 1), 0) + (CMP_BLOCK - 1)
    mask_c = cend <= qpos
    lc = jnp.where(mask_c, _dot_nt(kc, q4) + bcs_ref[0], NEG_INF)
    mc = jnp.max(lc, axis=0, keepdims=True)
    pc = jnp.where(mask_c, jnp.exp(lc - mc), 0.0)
    lsum = jnp.sum(pc, axis=0, keepdims=True)
    pc = pc / jnp.where(lsum > 0.0, lsum, 1.0)
    o_c = _dot_tn(vc, pc.astype(BF16))

    ov = ov_ref[...]
    hi, mid, lo = _split3(pc)
    imp = _dot(ov, hi) + _dot(ov, mid) + _dot(ov, lo)
    gm = gm_ref[...]
    hi, mid, lo = _split3(imp)
    imp = _dot(hi, gm) + _dot(mid, gm) + _dot(lo, gm)
    blk = lax.broadcasted_iota(jnp.int32, (nblk, 1), 0)
    cur = lax.shift_right_logical(qpos, 6)
    forced = (blk == 0) | (blk == cur) | (blk == cur - 1)
    visible = blk * SLC_BLOCK <= qpos
    score = jnp.where(forced, -NEG_INF, imp)
    score = jnp.where(visible, score, NEG_INF)
    sel = jnp.zeros((nblk, LANES), F32)
    for _ in range(SLC_TOPK):
        mx = jnp.max(score, axis=0, keepdims=True)
        first = jnp.min(jnp.where(score == mx, blk, nblk), axis=0, keepdims=True)
        pick = blk == first
        sel = jnp.where(pick & (mx > 0.5 * NEG_INF), 1.0, sel)
        score = jnp.where(pick, -3.0e38, score)
    sel_ref[...] = sel

    wait_pages(sk_hbm, sv_hbm)
    land_pages()

    bpt = SMP_TILE // SLC_BLOCK
    ntile = past // SMP_TILE

    def past_tile(kt):
        k0 = pl.multiple_of(kt * SMP_TILE, SMP_TILE)
        k_t = buf_a[pl.ds(k0, SMP_TILE), :].astype(BF16)
        v_t = buf_b[pl.ds(k0, SMP_TILE), :].astype(BF16)
        sblk = sel_ref[pl.ds(pl.multiple_of(kt * bpt, bpt), bpt), :]
        mask = jnp.concatenate([jnp.broadcast_to(sblk[i:i + 1, :], (SLC_BLOCK, LANES)) for i in range(bpt)],
                               axis=0) > 0.5
        bias = jnp.where(kt == ntile - 1, bsl_ref[0], bfar_ref[0])
        return jnp.where(mask, _dot_nt(k_t, q4) + bias, NEG_INF), v_t

    def pair(a, bb, m, l, acc):
        (sa, va), (sb, vb) = a, bb
        m_new = jnp.maximum(m, jnp.maximum(jnp.max(sa, axis=0, keepdims=True), jnp.max(sb, axis=0, keepdims=True)))
        alpha = jnp.exp(m - m_new)
        pa, pb = jnp.exp(sa - m_new), jnp.exp(sb - m_new)
        l = alpha * l + jnp.sum(pa, axis=0, keepdims=True) + jnp.sum(pb, axis=0, keepdims=True)
        acc = alpha * acc + _dot_tn(va, pa.astype(BF16)) + _dot_tn(vb, pb.astype(BF16))
        return m_new, l, acc

    neg = jnp.full((1, LANES), NEG_INF, F32)
    zero = jnp.zeros((1, LANES), F32)
    m_s, l_s, acc_s = lax.fori_loop(0, ntile // 2, lambda j, c: pair(past_tile(2 * j), past_tile(2 * j + 1), *c),
                                    (neg, zero, jnp.zeros((hd, LANES), F32)))
    jrow = lax.broadcasted_iota(jnp.int32, (nt, 1), 0)
    new_vis = jrow <= tok
    new_sel = sel_ref[pl.ds(past // SLC_BLOCK, 1), :] > 0.5
    s_new = jnp.where(new_vis & new_sel, _dot_nt(nks_ref[0, 0].astype(BF16), q4) + bnew_ref[0], NEG_INF)
    m_new = jnp.maximum(m_s, jnp.max(s_new, axis=0, keepdims=True))
    alpha = jnp.exp(m_s - m_new)
    p_new = jnp.exp(s_new - m_new)
    l_s = alpha * l_s + jnp.sum(p_new, axis=0, keepdims=True)
    o_s = (alpha * acc_s + _dot_tn(nvs_ref[0, 0].astype(BF16), p_new.astype(BF16))) / l_s

    wb = wk_ref.shape[3]
    wrow = lax.broadcasted_iota(jnp.int32, (wb, 1), 0)
    s_wc = jnp.where(wrow > tok + (wb - WINDOW), _dot_nt(wk_ref[0, 0, 0].astype(BF16), q4) + bwn_ref[0], NEG_INF)
    s_wn = jnp.where(new_vis, _dot_nt(nkw_ref[0, 0].astype(BF16), q4) + bnew_ref[0], NEG_INF)
    m_w = jnp.maximum(jnp.max(s_wc, axis=0, keepdims=True), jnp.max(s_wn, axis=0, keepdims=True))
    p_wc, p_wn = jnp.exp(s_wc - m_w), jnp.exp(s_wn - m_w)
    l_w = jnp.sum(p_wc, axis=0, keepdims=True) + jnp.sum(p_wn, axis=0, keepdims=True)
    o_w = (_dot_tn(wv_ref[0, 0, 0].astype(BF16), p_wc.astype(BF16))
           + _dot_tn(nvw_ref[0, 0].astype(BF16), p_wn.astype(BF16))) / l_w

    gates = jax.nn.sigmoid(g_ref[0, 0])
    oc_t, os_t, ow_t = o_c.T, o_s.T, o_w.T
    outs = []
    for g in range(grp):
        sl = slice(g * nt, (g + 1) * nt)
        outs.append(gates[:, 3 * g:3 * g + 1] * oc_t[sl] + gates[:, 3 * g + 1:3 * g + 2] * os_t[sl]
                    + gates[:, 3 * g + 2:3 * g + 3] * ow_t[sl])
    o_ref[0] = jnp.concatenate(outs, axis=-1)


def nsa_sample_tables(tbl, past, nt, wb):
    kvh, grp = tbl.shape[1], tbl.shape[2]
    t = jnp.arange(nt)

    def table(key_pos):
        v = tbl[rel_bucket(past + t[None, :] - key_pos[:, None])]
        v = v.transpose(2, 0, 3, 1).reshape(kvh, key_pos.shape[0], grp * nt)
        return jnp.pad(v, ((0, 0), (0, 0), (0, LANES - grp * nt)))
    nch = past // CMP_STRIDE
    bcs = table(CMP_STRIDE * jnp.arange(nch) + CMP_BLOCK - 1)
    bsl = table(past - SMP_TILE + jnp.arange(SMP_TILE))
    bwn = table(past - wb + jnp.arange(wb))
    bnew = table(past + jnp.arange(nt))
    bfar = table(jnp.full((1,), past - SMP_TILE - 1))
    return bcs, bsl, bwn, bnew, bfar


def nsa_sample_pallas(y, ngt, new_rows, lw, tables, layer, pools, win_k, win_v, page_table):
    b, nt, _ = y.shape
    kvh, grp, hd = NSA_KV_HEADS, NSA_GROUP, HEAD_DIM
    past = page_table.shape[1] * PAGE_SIZE
    assert nt & (nt - 1) == 0 and grp * nt <= LANES and nt <= CMP_STRIDE and past % (2 * SMP_TILE) == 0
    wb = win_k.shape[3]
    nch = past // CMP_STRIDE
    nsb = past // SLC_BLOCK + 1
    nblk = -(-nsb // 8) * 8
    qb = PROJ_OFF['nq'] // (grp * hd)
    gates = ngt.reshape(b, nt, kvh, grp * 3).transpose(0, 2, 1, 3)
    ci = jnp.arange(nch)[None, :] * CMP_STRIDE
    sj = jnp.arange(nblk)[:, None] * SLC_BLOCK
    ov = ((ci < sj + SLC_BLOCK) & (ci + CMP_BLOCK > sj) & (jnp.arange(nblk)[:, None] < nsb)).astype(BF16)
    ln = jnp.arange(LANES)
    gm = (((ln[:, None] & (nt - 1)) == (ln[None, :] & (nt - 1))) & (ln[:, None] < grp * nt)).astype(BF16)
    bcs, bsl, bwn, bnew, bfar = tables
    new_spec = pl.BlockSpec((1, 1, nt, hd), lambda i, j, pt: (i, j, 0, 0))
    win_spec = pl.BlockSpec((1, 1, 1, wb, hd), lambda i, j, pt: (layer, i, j, 0, 0))
    full = lambda a: pl.BlockSpec(a.shape, lambda i, j, pt: (0,) * a.ndim)
    per_head = lambda a: pl.BlockSpec((1,) + a.shape[1:], lambda i, j, pt: (j,) + (0,) * (a.ndim - 1))
    hbm = pl.BlockSpec(memory_space=pl.ANY)
    pek, pev = lw['cmp_pe_k'].reshape(1, -1), lw['cmp_pe_v'].reshape(1, -1)
    w1k, w1v = lw['cmp_w1_k'].astype(BF16), lw['cmp_w1_v'].astype(BF16)
    w2k, w2v = lw['cmp_w2_k'].astype(BF16), lw['cmp_w2_v'].astype(BF16)
    return pl.pallas_call(
        functools.partial(_nsa_sample_kernel, layer=layer, past=past, nt=nt),
        grid_spec=pltpu.PrefetchScalarGridSpec(
            num_scalar_prefetch=1,
            grid=(b, kvh),
            in_specs=[pl.BlockSpec((1, nt, grp * hd), lambda i, j, pt: (i, 0, qb + j)),
                      pl.BlockSpec((1, 1, nt, grp * 3), lambda i, j, pt: (i, j, 0, 0)),
                      new_spec, new_spec, new_spec, new_spec, new_spec, new_spec, win_spec, win_spec,
                      full(pek), full(pev), full(w1k), full(w1v), full(w2k), full(w2v),
                      per_head(bcs), per_head(bsl), per_head(bwn), per_head(bnew), per_head(bfar),
                      full(ov), full(gm), hbm, hbm, hbm, hbm],
            out_specs=pl.BlockSpec((1, nt, grp * hd), lambda i, j, pt: (i, 0, j)),
            scratch_shapes=[pltpu.VMEM((past, hd), F32), pltpu.VMEM((past, hd), F32),
                            pltpu.VMEM((hd, past), F32), pltpu.VMEM((hd, past), F32),
                            pltpu.VMEM((nblk, LANES), F32), pltpu.SemaphoreType.DMA((2,))]),
        out_shape=jax.ShapeDtypeStruct((b, nt, kvh * grp * hd), F32),
        compiler_params=pltpu.CompilerParams(vmem_limit_bytes=VMEM_LIMIT),
        name="nsa_sample",
    )(page_table, y, gates, *new_rows, win_k, win_v, pek, pev, w1k, w1v, w2k, w2v, bcs, bsl, bwn, bnew, bfar,
      ov, gm, *[jnp.swapaxes(p, 3, 4) for p in pools])


def rel_bucket(dist):
    n = jnp.maximum(dist, 0)
    exact = REL_BUCKETS // 2
    big = exact + (jnp.log(jnp.maximum(n, 1).astype(jnp.float32) / exact)
                   / math.log(REL_MAX_DIST / exact) * (REL_BUCKETS - exact)).astype(jnp.int32)
    return jnp.where(n < exact, n, jnp.minimum(big, REL_BUCKETS - 1))


MERGE_ROWS = 256
CONV_HALO = 8


def _layer_norm_rows(z, g, b):
    mu = jnp.mean(z, axis=-1, keepdims=True)
    var = jnp.mean(jnp.square(z - mu), axis=-1, keepdims=True)
    return (z - mu) * lax.rsqrt(var + LN_EPS) * g + b


def _rank_rows(v):
    r = v.shape[0]
    rid = lax.broadcasted_iota(jnp.int32, (r, 1), 0)
    rank = jnp.zeros(v.shape, jnp.int32)
    for j in range(r):
        vj = v[j:j + 1, :]
        rank = rank + ((vj > v) | ((vj == v) & (rid > j))).astype(jnp.int32)
    return rank


def _route_cols(s, bias):
    n = s.shape[1]
    per = N_EXPERTS // N_GROUPS
    sel = s + bias
    gs = []
    for g in range(N_GROUPS):
        sg = sel[g * per:(g + 1) * per, :]
        gs.append(jnp.sum(jnp.where(_rank_rows(sg) < 2, sg, 0.0), axis=0, keepdims=True))
    gkeep = _rank_rows(jnp.concatenate(gs, axis=0)) < TOPK_GROUPS
    keep = jnp.concatenate([jnp.broadcast_to(gkeep[g:g + 1, :], (per, n)) for g in range(N_GROUPS)], axis=0)
    rank = _rank_rows(jnp.where(keep, sel, NEG_INF))
    eid = lax.broadcasted_iota(jnp.int32, (N_EXPERTS, 1), 0)
    ids, ws = [], []
    for k in range(TOP_K):
        hit = rank == k
        ids.append(jnp.sum(jnp.where(hit, eid, 0), axis=0, keepdims=True))
        ws.append(jnp.sum(jnp.where(hit, s, 0.0), axis=0, keepdims=True))
    w = jnp.concatenate(ws, axis=0)
    return jnp.concatenate(ids, axis=0), w / jnp.sum(w, axis=0, keepdims=True) * ROUTED_SCALE


def _merge_kernel(x_ref, cb_ref, cc_ref, cx_ref, ccp_ref, cxp_ref, prev_ref, ma_ref, mb_ref, mc_ref, og_ref, on_ref,
                  cw_ref, wg_ref, wc_ref, wn_ref, wo_ref, g1_ref, sc2_ref, sh2_ref, lng_ref, lnb_ref, wr_ref, rb_ref,
                  x1_ref, h2_ref, idx_ref, rw_ref, cst_ref):
    j = pl.program_id(1)
    tm = x_ref.shape[1]
    u = cc_ref[0] * cx_ref[0]
    halo = jnp.where(j == 0, prev_ref[0], ccp_ref[0] * cxp_ref[0])
    p1 = halo[CONV_HALO - 1:CONV_HALO, :]
    p2 = halo[CONV_HALO - 2:CONV_HALO - 1, :]
    rid = lax.broadcasted_iota(jnp.int32, (tm, 1), 0)
    u1 = jnp.where(rid == 0, p1, pltpu.roll(u, 1, axis=0))
    u2 = jnp.where(rid == 0, p2, jnp.where(rid == 1, p1, pltpu.roll(u, 2, axis=0)))
    cw = cw_ref[...]
    o_conv = cb_ref[0] * (u2 * cw[0:1, :] + u1 * cw[1:2, :] + u * cw[2:3, :])
    cst_ref[0] = u[tm - CONV_HALO:tm, :]
    merged = (jax.nn.sigmoid(ma_ref[0]) * _dot(og_ref[0].astype(BF16), wg_ref[...])
              + jax.nn.sigmoid(mb_ref[0]) * _dot(o_conv.astype(BF16), wc_ref[...])
              + jax.nn.sigmoid(mc_ref[0]) * _dot(on_ref[0].astype(BF16), wn_ref[...]))
    mix = _dot(merged.astype(BF16), wo_ref[...])
    x1 = _layer_norm_rows(DN_ALPHA * x_ref[0] + g1_ref[0] * mix, lng_ref[...], lnb_ref[...])
    x1_ref[0] = x1
    h2 = x1 * (1.0 + sc2_ref[0]) + sh2_ref[0]
    h2_ref[0] = h2.astype(h2_ref.dtype)
    s = jax.nn.sigmoid(lax.dot_general(wr_ref[...], h2, (((1,), (1,)), ((), ())),
                                       precision=lax.Precision.HIGHEST, preferred_element_type=F32))
    idx_ref[0], rw_ref[0] = _route_cols(s, rb_ref[...])


def merge_pallas(x, y, o_gla, o_n, conv_prev, g1, sc2, sh2, lw):
    b, t, d = x.shape
    tm = math.gcd(t, MERGE_ROWS)
    halo = CONV_HALO
    hb = tm // halo
    prev = jnp.concatenate([jnp.zeros((b, halo - (CONV_W - 1), d), F32), conv_prev.astype(F32)], axis=1)
    col = lambda name, k=0: PROJ_OFF[name] // d + k
    tile = lambda cb_: pl.BlockSpec((1, tm, d), lambda i, j: (i, j, cb_))
    halo_spec = lambda cb_: pl.BlockSpec((1, halo, d), lambda i, j: (i, jnp.maximum(j * hb - 1, 0), cb_))
    vec = pl.BlockSpec((1, 1, d), lambda i, j: (i, 0, 0))
    const = lambda a: pl.BlockSpec(a.shape, lambda i, j: (0,) * a.ndim)
    wts = [lw['w_br_gla'].astype(BF16), lw['w_br_conv'].astype(BF16), lw['w_br_nsa'].astype(BF16), lw['w_o'].astype(BF16)]
    lng, lnb = lw['ln1_g'].reshape(1, d), lw['ln1_b'].reshape(1, d)
    wr, rb = lw['w_router'].T, lw['router_bias'].reshape(N_EXPERTS, 1).astype(F32)
    topk = pl.BlockSpec((1, TOP_K, tm), lambda i, j: (i, 0, j))
    x1, h2, idx, rw, cst = pl.pallas_call(
        _merge_kernel,
        grid=(b, t // tm),
        in_specs=[tile(0), tile(col('cb')), tile(col('cc')), tile(col('cx')), halo_spec(col('cc')), halo_spec(col('cx')),
                  pl.BlockSpec((1, halo, d), lambda i, j: (i, 0, 0)),
                  tile(col('mgt', 0)), tile(col('mgt', 1)), tile(col('mgt', 2)), tile(0), tile(0),
                  const(lw['conv_w'])] + [const(w) for w in wts] + [vec, vec, vec, const(lng), const(lnb),
                                                                    const(wr), const(rb)],
        out_specs=[tile(0), tile(0), topk, topk, pl.BlockSpec((1, halo, d), lambda i, j: (i, 0, 0))],
        out_shape=[jax.ShapeDtypeStruct((b, t, d), F32), jax.ShapeDtypeStruct((b, t, d), BF16 if tm % 16 == 0 else F32),
                   jax.ShapeDtypeStruct((b, TOP_K, t), jnp.int32), jax.ShapeDtypeStruct((b, TOP_K, t), F32),
                   jax.ShapeDtypeStruct((b, halo, d), F32)],
        compiler_params=pltpu.CompilerParams(vmem_limit_bytes=VMEM_LIMIT),
        name="merge_ln1",
    )(x, y, y, y, y, y, prev, y, y, y, o_gla, o_n, lw['conv_w'], *wts,
      g1.reshape(b, 1, d), sc2.reshape(b, 1, d), sh2.reshape(b, 1, d), lng, lnb, wr, rb)
    to_rows = lambda a: a.transpose(0, 2, 1).reshape(b * t, TOP_K)
    return x1, h2, to_rows(idx), to_rows(rw), cst[:, halo - (CONV_W - 1):]


def _expert_act(xb, wg, wu):
    hg = _dot(xb, wg)
    return hg * jax.nn.sigmoid(hg) * _dot(xb, wu)


def _grouped_expert_kernel(be_ref, x_ref, wg_ref, wu_ref, wd_ref, o_ref):
    act = _expert_act(x_ref[...], wg_ref[0], wu_ref[0])
    o_ref[...] = _dot(act.astype(BF16), wd_ref[0]).astype(o_ref.dtype)


def moe_routed_sorted(h2, idx, w, lw):
    n, d = h2.shape
    a = n * TOP_K
    onehot = idx[:, :, None] == jnp.arange(N_EXPERTS, dtype=jnp.int32)[None, None, :]
    hit = onehot.any(axis=1).astype(F32)
    ct = math.gcd(n, 512)
    hit_t = hit.reshape(n // ct, ct, N_EXPERTS)
    within = jnp.einsum('ts,nse->nte', jnp.tril(jnp.ones((ct, ct), F32), -1), hit_t)
    tile_tot = hit_t.sum(1)
    before = jnp.cumsum(tile_tot, axis=0) - tile_tot
    pos = (within + before[:, None, :]).reshape(n, N_EXPERTS).astype(jnp.int32)
    counts = tile_tot.sum(0).astype(jnp.int32)
    padded = (counts + MOE_BLOCK - 1) // MOE_BLOCK * MOE_BLOCK
    pend = jnp.cumsum(padded)
    pstart = pend - padded
    slot = jnp.sum(jnp.where(onehot, (pos + pstart[None, :])[:, None, :], 0), axis=-1)
    nb = -(-a // MOE_BLOCK) + N_EXPERTS
    rows = nb * MOE_BLOCK
    row_tok = jnp.zeros((rows,), jnp.int32).at[slot.reshape(-1)].set(jnp.arange(a, dtype=jnp.int32) // TOP_K)
    blk_e = jnp.minimum(jnp.sum(pend[None, :] <= (jnp.arange(nb) * MOE_BLOCK)[:, None], axis=1), N_EXPERTS - 1).astype(jnp.int32)
    xs = h2.astype(BF16)[row_tok]
    wg, wu, wd = lw['w_exp_gate'].astype(BF16), lw['w_exp_up'].astype(BF16), lw['w_exp_down'].astype(BF16)
    hdim = wg.shape[2]
    y = pl.pallas_call(
        _grouped_expert_kernel,
        grid_spec=pltpu.PrefetchScalarGridSpec(
            num_scalar_prefetch=1,
            grid=(nb,),
            in_specs=[pl.BlockSpec((MOE_BLOCK, d), lambda i, be: (i, 0)),
                      pl.BlockSpec((1, d, hdim), lambda i, be: (be[i], 0, 0)),
                      pl.BlockSpec((1, d, hdim), lambda i, be: (be[i], 0, 0)),
                      pl.BlockSpec((1, hdim, d), lambda i, be: (be[i], 0, 0))],
            out_specs=pl.BlockSpec((MOE_BLOCK, d), lambda i, be: (i, 0))),
        out_shape=jax.ShapeDtypeStruct((rows, d), BF16),
        name="moe_grouped",
    )(blk_e, xs, wg, wu, wd)
    return y[slot.T.reshape(-1)].reshape(TOP_K, n, d), w


def _dense_expert_kernel(x_ref, wt_ref, wg_ref, wu_ref, wd_ref, o_ref):
    @pl.when(pl.program_id(0) == 0)
    def _():
        o_ref[...] = jnp.zeros_like(o_ref)
    act = _expert_act(x_ref[...].astype(BF16), wg_ref[0].astype(BF16), wu_ref[0].astype(BF16))
    o_ref[...] += _dot((act * wt_ref[0]).astype(BF16), wd_ref[0].astype(BF16))


def moe_routed_dense(h2, idx, w, lw):
    n, d = h2.shape
    wdense = jnp.zeros((n, N_EXPERTS), F32).at[jnp.arange(n)[:, None], idx].add(w)
    wt = wdense.T.reshape(N_EXPERTS, n, 1)
    hdim = lw['w_exp_gate'].shape[2]
    routed = pl.pallas_call(
        _dense_expert_kernel,
        grid=(N_EXPERTS,),
        in_specs=[pl.BlockSpec((n, d), lambda e: (0, 0)),
                  pl.BlockSpec((1, n, 1), lambda e: (e, 0, 0)),
                  pl.BlockSpec((1, d, hdim), lambda e: (e, 0, 0)),
                  pl.BlockSpec((1, d, hdim), lambda e: (e, 0, 0)),
                  pl.BlockSpec((1, hdim, d), lambda e: (e, 0, 0))],
        out_specs=pl.BlockSpec((n, d), lambda e: (0, 0)),
        out_shape=jax.ShapeDtypeStruct((n, d), F32),
        name="moe_dense",
    )(h2, wt, lw['w_exp_gate'], lw['w_exp_up'], lw['w_exp_down'])
    return routed.reshape(1, n, d), jnp.ones((n, 1), F32)


def _moe_tail_kernel(x1_ref, h2_ref, rw_ref, g2_ref, wg_ref, wu_ref, wd_ref, lng_ref, lnb_ref, *rest):
    r_refs, o_ref = rest[:-1], rest[-1]
    rw = rw_ref[0]
    routed = rw[:, 0:1] * r_refs[0][0, 0].astype(F32)
    for k in range(1, len(r_refs)):
        routed = routed + rw[:, k:k + 1] * r_refs[k][0, 0].astype(F32)
    act = _expert_act(h2_ref[0].astype(BF16), wg_ref[...], wu_ref[...])
    shared = _dot(act.astype(BF16), wd_ref[...])
    o_ref[0] = _layer_norm_rows(DN_ALPHA * x1_ref[0] + g2_ref[0] * (routed + shared), lng_ref[...], lnb_ref[...])


def moe_tail_pallas(x1, h2, rows, rw, g2, lw):
    b, t, d = x1.shape
    nk = rw.shape[1]
    tm = math.gcd(t, MERGE_ROWS)
    tile = pl.BlockSpec((1, tm, d), lambda i, j: (i, j, 0))
    const = lambda a: pl.BlockSpec(a.shape, lambda i, j: (0,) * a.ndim)
    wg, wu, wd = lw['w_sh_gate'].astype(BF16), lw['w_sh_up'].astype(BF16), lw['w_sh_down'].astype(BF16)
    lng, lnb = lw['ln2_g'].reshape(1, d), lw['ln2_b'].reshape(1, d)
    rows = rows.reshape(nk, b, t, d)
    row_specs = [pl.BlockSpec((1, 1, tm, d), functools.partial(lambda i, j, k: (k, i, j, 0), k=k)) for k in range(nk)]
    return pl.pallas_call(
        _moe_tail_kernel,
        grid=(b, t // tm),
        in_specs=[tile, tile, pl.BlockSpec((1, tm, nk), lambda i, j: (i, j, 0)),
                  pl.BlockSpec((1, 1, d), lambda i, j: (i, 0, 0)),
                  const(wg), const(wu), const(wd), const(lng), const(lnb)] + row_specs,
        out_specs=tile,
        out_shape=jax.ShapeDtypeStruct((b, t, d), F32),
        compiler_params=pltpu.CompilerParams(vmem_limit_bytes=VMEM_LIMIT),
        name="moe_tail_ln2",
    )(x1, h2, rw.reshape(b, t, nk), g2.reshape(b, 1, d), wg, wu, wd, lng, lnb, *([rows] * nk))


def layer_block(x, c, lw, gla_s0, conv_prev, nsa_fn, sorted_moe):
    b, t, d = x.shape
    mod = jax.nn.silu(c) @ lw['w_ada'] + lw['b_ada']
    sh1, sc1, g1, sh2, sc2, g2 = jnp.split(mod, 6, axis=-1)
    if t % LANES == 0:
        y = in_proj(x, sc1, sh1, lw['w_in'])
    else:
        h1 = (x * (1.0 + sc1[:, None, :]) + sh1[:, None, :]).reshape(1, b * t, d)
        y = in_proj(h1, jnp.zeros((1, d), F32), jnp.zeros((1, d), F32), lw['w_in']).reshape(b, t, PROJ_COLS)
    o_gla, s_fin = gla_pallas(y, lw, gla_s0)

    def rows(name):
        return proj_cols(y, name).reshape(b, t, NSA_KV_HEADS, HEAD_DIM).transpose(0, 2, 1, 3)
    kv_rows = [rows(nm) for nm in ('nkc', 'nvc', 'nks', 'nvs', 'nkw', 'nvw')]
    o_n, win_k, win_v = nsa_fn(y, *kv_rows)
    x1, h2, idx, w, conv_state = merge_pallas(x, y, o_gla, o_n, conv_prev, g1, sc2, sh2, lw)
    rows_out, rw = (moe_routed_sorted if sorted_moe else moe_routed_dense)(h2.reshape(b * t, d), idx, w, lw)
    x2 = moe_tail_pallas(x1, h2, rows_out, rw, g2, lw)
    return x2, (*kv_rows[:4], win_k, win_v, s_fin, conv_state)


def kernel(x_prompt, x_sample, cache_cmp_k, cache_cmp_v, cache_slc_k, cache_slc_v, cache_win_k, cache_win_v,
           state_gla, state_conv, page_table, c_prompt, c_sample, w_ada, b_ada, w_in, gla_wa2, gla_ba,
           gla_norm_g, conv_w, cmp_pe_k, cmp_pe_v, cmp_w1_k, cmp_w2_k, cmp_w1_v, cmp_w2_v, rel_bias,
           w_br_gla, w_br_conv, w_br_nsa, w_o, ln1_g, ln1_b, ln2_g, ln2_b, w_router, router_bias,
           w_exp_gate, w_exp_up, w_exp_down, w_sh_gate, w_sh_up, w_sh_down):
    params = dict(w_ada=w_ada, b_ada=b_ada, w_in=w_in, gla_wa2=gla_wa2, gla_ba=gla_ba, gla_norm_g=gla_norm_g,
                  conv_w=conv_w, cmp_pe_k=cmp_pe_k, cmp_pe_v=cmp_pe_v, cmp_w1_k=cmp_w1_k, cmp_w2_k=cmp_w2_k,
                  cmp_w1_v=cmp_w1_v, cmp_w2_v=cmp_w2_v, w_br_gla=w_br_gla, w_br_conv=w_br_conv, w_br_nsa=w_br_nsa,
                  w_o=w_o, ln1_g=ln1_g, ln1_b=ln1_b, ln2_g=ln2_g, ln2_b=ln2_b, w_router=w_router,
                  router_bias=router_bias, w_exp_gate=w_exp_gate, w_exp_up=w_exp_up, w_exp_down=w_exp_down,
                  w_sh_gate=w_sh_gate, w_sh_up=w_sh_up, w_sh_down=w_sh_down)
    tbl = rel_bias.reshape(REL_BUCKETS, NSA_KV_HEADS, NSA_GROUP)
    bp, tp, _ = x_prompt.shape
    bs, ts, _ = x_sample.shape
    bias_tables = nsa_bias_tables(tbl, tp, tp // CMP_STRIDE)
    sample_tables = nsa_sample_tables(tbl, page_table.shape[1] * PAGE_SIZE, ts, cache_win_k.shape[3])
    xp, xs = x_prompt, x_sample
    p_states, s_states = [], []
    for l in range(DEPTH):
        lw = {k: v[l] for k, v in params.items()}

        def prompt_nsa(y, kc_r, vc_r, ks_r, vs_r, kw_r, vw_r):
            o_n = nsa_prompt_pallas(y, PROJ_OFF['nq'], proj_cols(y, 'ngt'), kc_r, vc_r, ks_r, vs_r, kw_r, vw_r,
                                    lw, bias_tables)
            wb = min(WINDOW, tp)
            return o_n, kw_r[:, :, tp - wb:], vw_r[:, :, tp - wb:]

        def sample_nsa(y, kc_r, vc_r, ks_r, vs_r, kw_r, vw_r):
            o_n = nsa_sample_pallas(y, proj_cols(y, 'ngt'), (kc_r, vc_r, ks_r, vs_r, kw_r, vw_r), lw, sample_tables, l,
                                    (cache_cmp_k, cache_cmp_v, cache_slc_k, cache_slc_v), cache_win_k, cache_win_v,
                                    page_table)
            wb = cache_win_k.shape[3]
            win_k = jnp.concatenate([cache_win_k[l], kw_r], axis=2)[:, :, -wb:]
            win_v = jnp.concatenate([cache_win_v[l], vw_r], axis=2)[:, :, -wb:]
            return o_n, win_k, win_v

        xp, st = layer_block(xp, c_prompt, lw, jnp.zeros((bp, GLA_HEADS, GLA_DK, GLA_DV), F32),
                             jnp.zeros((bp, CONV_W - 1, CONV_DIM), F32), prompt_nsa, True)
        p_states.append(st)
        xs, st = layer_block(xs, c_sample, lw, state_gla[l], state_conv[l], sample_nsa, False)
        s_states.append(st)
    p_out = [jnp.stack(z) for z in zip(*p_states)]
    s_out = [jnp.stack(z) for z in zip(*s_states)]
    return (xp, xs, *p_out, *s_out)
```
